```python
import math
import jax
import jax.numpy as jnp
from jax import lax
import numpy as np

D_MODEL = 2048
BATCH = 4
SEQ = 2048
DEPTH = 4
DEC_BATCH = 8
DEC_SEQ = 8
PAST_LEN = 16384
PAGE_SIZE = 128

HEAD_DIM = 128
QBLK = 128
A_GROUPS = ((128, 1), (512, 4), (2048, 16))
A_NG = 3
A_HPG = 8
A_QKV = A_NG * A_HPG * HEAD_DIM
A_WIDTH = A_HPG * HEAD_DIM
B_WIDTH = 1024
CONV_W = 31
EVEN_MIX = A_WIDTH + B_WIDTH
EVEN_IN = 3 * A_QKV + A_WIDTH + 3 * B_WIDTH
C_HEADS = 16
C_KV_HEADS = 2
C_HPG = C_HEADS // C_KV_HEADS
C_WIDTH = C_HEADS * HEAD_DIM
C_KV = C_KV_HEADS * HEAD_DIM
CMP_BLK = 32
CMP_STRIDE = 16
CMP_HID = 128
SEL_BLK = 64
SEL_N = 16
WIN = 512
ODD_IN = 2 * C_WIDTH + 6 * C_KV + 3 * C_HEADS
NUM_BUCKETS = 32
MAX_DIST = 2048
BIAS_HEADS = A_NG * A_HPG
EPS = 1e-6
N_EVEN = (DEPTH + 1) // 2
N_ODD = DEPTH // 2

kernel_name = "hybrid_dilated_conv_nsa_decoder_step"


def rms_norm(x, g):
    x32 = x.astype(jnp.float32)
    y = x32 * lax.rsqrt(jnp.mean(x32 * x32, axis=-1, keepdims=True) + EPS)
    return (y * g.astype(jnp.float32)).astype(x.dtype)


def t5_bucket(dist):
    max_exact = NUM_BUCKETS // 2
    d = jnp.maximum(dist, 0)
    ratio = jnp.log(jnp.maximum(d, 1).astype(jnp.float32) / max_exact) / math.log(MAX_DIST / max_exact)
    large = jnp.minimum(max_exact + (ratio * (NUM_BUCKETS - max_exact)).astype(jnp.int32), NUM_BUCKETS - 1)
    return jnp.where(d < max_exact, d, large)


def masked_softmax(s, valid, axes):
    s = jnp.where(valid, s, -jnp.inf)
    m = jnp.max(s, axis=axes, keepdims=True)
    m = jnp.where(jnp.isfinite(m), m, 0.0)
    e = jnp.where(valid, jnp.exp(s - m), 0.0)
    den = jnp.sum(e, axis=axes, keepdims=True)
    return e / jnp.where(den > 0.0, den, 1.0)


def split_cols(z, widths):
    out, off = [], 0
    for w in widths:
        out.append(z[..., off:off + w])
        off += w
    return out


def over_query_blocks(fn, q):
    b, t = q.shape[:2]
    qb = QBLK if t % QBLK == 0 else t
    nb = t // qb
    q_blocks = jnp.swapaxes(q.reshape((b, nb, qb) + q.shape[2:]), 0, 1)
    starts = jnp.arange(nb, dtype=jnp.int32) * qb
    out = lax.map(lambda args: fn(args[0], args[1]), (q_blocks, starts))
    out = jnp.swapaxes(out, 0, 1)
    return out.reshape((b, t) + out.shape[3:])


def dilated_attention(q, kv_full, hist_lens, rel_bias):
    scale = HEAD_DIM ** -0.5
    dt = q.dtype

    def block(qb, start):
        tq = qb.shape[1]
        lses, outs = [], []
        for g, (win, dil) in enumerate(A_GROUPS):
            steps = jnp.arange(win // dil + 1, dtype=jnp.int32) * dil
            qidx = hist_lens[g] + start + jnp.arange(tq, dtype=jnp.int32)
            kidx = qidx[:, None] - steps[None, :]
            valid = (kidx >= 0)[None, :, None, :]
            kv = kv_full[g][:, jnp.maximum(kidx, 0)]
            bias = rel_bias[t5_bucket(steps), g * A_HPG:(g + 1) * A_HPG]
            s = jnp.einsum('bqhd,bqjhd->bqhj', qb[:, :, g], kv[:, :, :, 0]).astype(jnp.float32) * scale
            s = jnp.where(valid, s + bias.T.astype(jnp.float32), -jnp.inf)
            m = jnp.max(s, axis=-1, keepdims=True)
            e = jnp.exp(s - m)
            den = jnp.sum(e, axis=-1, keepdims=True)
            outs.append(jnp.einsum('bqhj,bqjhd->bqhd', (e / den).astype(dt), kv[:, :, :, 1]))
            lses.append((m + jnp.log(den))[..., 0])
        w = jax.nn.softmax(jnp.stack(lses, axis=-1), axis=-1)
        return jnp.einsum('bqhg,gbqhd->bqhd', w.astype(dt), jnp.stack(outs))

    return over_query_blocks(block, q)


def conv_module(glu_in, hist, conv_w, conv_b, ln_g, ln_b, pw_w, pw_b):
    a, gt = glu_in[..., :B_WIDTH], glu_in[..., B_WIDTH:]
    u = a * jax.nn.sigmoid(gt)
    full = jnp.concatenate([hist.astype(u.dtype), u], axis=1)
    y = lax.conv_general_dilated(full, conv_w[:, None, :].astype(u.dtype), (1,), 'VALID',
                                 dimension_numbers=('NWC', 'WIO', 'NWC'),
                                 feature_group_count=B_WIDTH) + conv_b
    y32 = y.astype(jnp.float32)
    mu = jnp.mean(y32, axis=-1, keepdims=True)
    var = jnp.mean(jnp.square(y32 - mu), axis=-1, keepdims=True)
    y = (y32 - mu) * lax.rsqrt(var + EPS) * ln_g.astype(jnp.float32) + ln_b.astype(jnp.float32)
    y = jax.nn.silu(y).astype(u.dtype)
    return y @ pw_w + pw_b, full[:, full.shape[1] - (CONV_W - 1):]


def compress(x, pos, w1, b1, w2):
    b, l = x.shape[:2]
    n_ch = l // CMP_STRIDE
    ch = x[:, :n_ch * CMP_STRIDE].reshape(b, n_ch, CMP_STRIDE, C_KV_HEADS, HEAD_DIM)
    blocks = jnp.concatenate([ch[:, :-1], ch[:, 1:]], axis=2) + pos[None, None, :, None, :]
    h = jax.nn.silu(jnp.einsum('bnlgd,lde->bnge', blocks, w1) + b1)
    return jnp.einsum('bnge,ed->bngd', h, w2)


def nsa_attention(q, cs_full, win_full, win_hist_len, rel_bias, cmp_pos, cmp_w1, cmp_b1, cmp_w2):
    b, t = q.shape[:2]
    l = cs_full.shape[1]
    q_pos0 = l - t
    dt = q.dtype
    scale = HEAD_DIM ** -0.5
    k_cmp = compress(cs_full[:, :, 0], cmp_pos[0], cmp_w1[0], cmp_b1[0], cmp_w2[0])
    v_cmp = compress(cs_full[:, :, 1], cmp_pos[1], cmp_w1[1], cmp_b1[1], cmp_w2[1])
    n_cmp = k_cmp.shape[1]
    cmp_end = jnp.arange(n_cmp, dtype=jnp.int32) * CMP_STRIDE + (CMP_BLK - 1)
    n_sel = -(-l // SEL_BLK)
    sel = jnp.pad(cs_full[:, :, 2:4], ((0, 0), (0, n_sel * SEL_BLK - l), (0, 0), (0, 0), (0, 0)))
    sel = sel.reshape(b, n_sel, SEL_BLK, 2, C_KV_HEADS, HEAD_DIM).transpose(0, 4, 1, 2, 3, 5)
    ci = np.arange(n_cmp)[:, None]
    si = np.arange(n_sel)[None, :]
    overlap = jnp.asarray(((ci * CMP_STRIDE < (si + 1) * SEL_BLK) &
                           (ci * CMP_STRIDE + CMP_BLK > si * SEL_BLK)).astype(np.float32))
    n_top = min(SEL_N, n_sel)
    tab = rel_bias[:, :C_HEADS].reshape(NUM_BUCKETS, C_KV_HEADS, C_HPG).astype(jnp.float32)
    tab_g = tab.transpose(1, 0, 2)
    win_keys = jnp.pad(win_full, ((0, 0), (WIN, 0), (0, 0), (0, 0), (0, 0)))
    bidx = jnp.arange(b)[:, None, None, None]
    gidx = jnp.arange(C_KV_HEADS)[None, :, None, None]

    def block(qb, start):
        tq = qb.shape[1]
        qpos = q_pos0 + start + jnp.arange(tq, dtype=jnp.int32)
        s_c = jnp.einsum('bqghd,bngd->bqghn', qb, k_cmp).astype(jnp.float32) * scale
        ok_c = (cmp_end[None, :] <= qpos[:, None])[None, :, None, None, :]
        p_c = masked_softmax(s_c, ok_c, (-1,))
        o_c = jnp.einsum('bqghn,bngd->bqghd', p_c.astype(dt), v_cmp)
        imp = jnp.einsum('bqghn,ns->bqgs', p_c, overlap)
        blk = jnp.arange(n_sel, dtype=jnp.int32)[None, :]
        qblk = (qpos // SEL_BLK)[:, None]
        forced = (blk == 0) | (blk == qblk) | (blk == qblk - 1)
        score = jnp.where(forced[None, :, None, :], jnp.inf, imp)
        score = jnp.where((blk <= qblk)[None, :, None, :], score, -jnp.inf)
        top_v, top_i = lax.top_k(score, n_top)
        top_i = top_i.transpose(0, 2, 1, 3)
        top_ok = (top_v > -jnp.inf).transpose(0, 2, 1, 3)
        kv_s = sel[bidx, gidx, top_i]
        kpos = top_i[..., None] * SEL_BLK + jnp.arange(SEL_BLK, dtype=jnp.int32)
        dist = qpos[None, None, :, None, None] - kpos
        ok_s = (top_ok[..., None] & (dist >= 0))[:, :, :, None]
        bias_s = jnp.moveaxis(tab_g[gidx[..., None], t5_bucket(dist)], -1, 3)
        s_s = jnp.einsum('bqghd,bgqkld->bgqhkl', qb, kv_s[..., 0, :]).astype(jnp.float32) * scale + bias_s
        p_s = masked_softmax(s_s, ok_s, (-2, -1))
        o_s = jnp.einsum('bgqhkl,bgqkld->bqghd', p_s.astype(dt), kv_s[..., 1, :])
        kv_w = lax.dynamic_slice_in_dim(win_keys, win_hist_len + start, WIN + tq, axis=1)
        kwpos = q_pos0 + start - WIN + jnp.arange(WIN + tq, dtype=jnp.int32)
        dist_w = qpos[:, None] - kwpos[None, :]
        ok_w = ((dist_w >= 0) & (dist_w <= WIN) & (kwpos[None, :] >= 0))[None, :, None, None, :]
        bias_w = tab[t5_bucket(dist_w)].transpose(0, 2, 3, 1)
        s_w = jnp.einsum('bqghd,bkgd->bqghk', qb, kv_w[:, :, 0]).astype(jnp.float32) * scale + bias_w[None]
        p_w = masked_softmax(s_w, ok_w, (-1,))
        o_w = jnp.einsum('bqghk,bkgd->bqghd', p_w.astype(dt), kv_w[:, :, 1])
        return jnp.stack([o_c, o_s, o_w], axis=2)

    return over_query_blocks(block, q)


def even_layer(x, a_hists, conv_hist, p_len, rel_bias, norm_g, w_in, conv_w, conv_b, ln_g, ln_b, pw_w, pw_b, w_out):
    b, t = x.shape[:2]
    z = rms_norm(x, norm_g) @ w_in
    q, k, v, ga, glu, gb = split_cols(z, (A_QKV, A_QKV, A_QKV, A_WIDTH, 2 * B_WIDTH, B_WIDTH))
    shp = (b, t, A_NG, A_HPG, HEAD_DIM)
    q, k, v = q.reshape(shp), k.reshape(shp), v.reshape(shp)
    kv_full, new_hists = [], []
    for g, (win, _) in enumerate(A_GROUPS):
        full = jnp.concatenate([a_hists[g].astype(k.dtype), jnp.stack([k[:, :, g], v[:, :, g]], axis=2)], axis=1)
        kv_full.append(full)
        keep = min(win, p_len + t)
        new_hists.append(full[:, full.shape[1] - keep:])
    o_a = dilated_attention(q, kv_full, [h.shape[1] for h in a_hists], rel_bias).reshape(b, t, A_WIDTH)
    o_b, new_conv = conv_module(glu, conv_hist, conv_w, conv_b, ln_g, ln_b, pw_w, pw_b)
    mix = jnp.concatenate([o_a * jax.nn.silu(ga), o_b * jax.nn.silu(gb)], axis=-1)
    return x + mix @ w_out, new_hists, new_conv


def odd_layer(x, c_past, win_hist, rel_bias, norm_g, w_in, cmp_pos, cmp_w1, cmp_b1, cmp_w2, w_out):
    b, t = x.shape[:2]
    p_len = c_past.shape[1]
    z = rms_norm(x, norm_g) @ w_in
    q, kc, vc, ks, vs, kw, vw, gl, gp = split_cols(z, (C_WIDTH, C_KV, C_KV, C_KV, C_KV, C_KV, C_KV, 3 * C_HEADS, C_WIDTH))
    kvh = lambda a: a.reshape(b, t, C_KV_HEADS, HEAD_DIM)
    rows = jnp.stack([kvh(kc), kvh(vc), kvh(ks), kvh(vs)], axis=2)
    cs_full = jnp.concatenate([c_past.astype(rows.dtype), rows], axis=1)
    win_full = jnp.concatenate([win_hist.astype(rows.dtype), jnp.stack([kvh(kw), kvh(vw)], axis=2)], axis=1)
    o = nsa_attention(q.reshape(b, t, C_KV_HEADS, C_HPG, HEAD_DIM), cs_full, win_full, win_hist.shape[1],
                      rel_bias, cmp_pos, cmp_w1, cmp_b1, cmp_w2)
    gates = jax.nn.sigmoid(gl.astype(jnp.float32)).reshape(b, t, 3, C_KV_HEADS, C_HPG, 1).astype(o.dtype)
    o = jnp.sum(gates * o, axis=2).reshape(b, t, C_WIDTH)
    keep = min(WIN, p_len + t)
    return x + (o * jax.nn.silu(gp)) @ w_out, rows, win_full[:, win_full.shape[1] - keep:]


def run_trunk(x, p_len, a_hists, conv_hist, get_c_past, c_win, rel_bias, norm_even, w_in_even, conv_w, conv_b,
              conv_ln_g, conv_ln_b, conv_pw_w, conv_pw_b, w_out_even, norm_odd, w_in_odd, cmp_pos, cmp_w1,
              cmp_b1, cmp_w2, w_out_odd, final_norm):
    new_a = [[] for _ in A_GROUPS]
    new_conv, new_rows, new_win = [], [], []
    for layer in range(DEPTH):
        i = layer // 2
        if layer % 2 == 0:
            x, hists, conv = even_layer(x, [h[i] for h in a_hists], conv_hist[i], p_len, rel_bias, norm_even[i],
                                        w_in_even[i], conv_w[i], conv_b[i], conv_ln_g[i], conv_ln_b[i],
                                        conv_pw_w[i], conv_pw_b[i], w_out_even[i])
            for g in range(A_NG):
                new_a[g].append(hists[g])
            new_conv.append(conv)
        else:
            x, rows, win = odd_layer(x, get_c_past(i), c_win[i], rel_bias, norm_odd[i], w_in_odd[i],
                                     cmp_pos[i], cmp_w1[i], cmp_b1[i], cmp_w2[i], w_out_odd[i])
            new_rows.append(rows)
            new_win.append(win)
    y = rms_norm(x, final_norm)
    return y, [jnp.stack(a) for a in new_a], jnp.stack(new_conv), jnp.stack(new_rows), jnp.stack(new_win)


def setup_inputs(seed: int = 0) -> dict:
    key = jax.random.key(seed)
    k = jax.random.split(key, 27)
    nrm = lambda kk, shape, scale: jax.random.normal(kk, shape, jnp.float32) * scale
    n_pages = PAST_LEN // PAGE_SIZE
    n_pool = (5 * DEC_BATCH * n_pages + 3) // 4
    a_caches = [nrm(k[2 + g], (N_EVEN, DEC_BATCH, min(w, PAST_LEN), 2, A_HPG, HEAD_DIM), 1.0)
                for g, (w, _) in enumerate(A_GROUPS)]
    page_table = jax.random.permutation(k[8], n_pool)[:DEC_BATCH * n_pages].reshape(DEC_BATCH, n_pages).astype(jnp.int32)
    return {
        'x_prompt': nrm(k[0], (BATCH, SEQ, D_MODEL), 1.0),
        'x_sample': nrm(k[1], (DEC_BATCH, DEC_SEQ, D_MODEL), 1.0),
        'cache_a_kv0': a_caches[0],
        'cache_a_kv1': a_caches[1],
        'cache_a_kv2': a_caches[2],
        'state_b_conv': nrm(k[5], (N_EVEN, DEC_BATCH, CONV_W - 1, B_WIDTH), 0.5),
        'cache_c_kv': nrm(k[6], (N_ODD, n_pool, PAGE_SIZE, 4, C_KV_HEADS, HEAD_DIM), 1.0),
        'cache_c_win': nrm(k[7], (N_ODD, DEC_BATCH, min(WIN, PAST_LEN), 2, C_KV_HEADS, HEAD_DIM), 1.0),
        'page_table': page_table,
        'rel_bias': nrm(k[9], (NUM_BUCKETS, BIAS_HEADS), 0.5),
        'norm_even': 1.0 + nrm(k[10], (N_EVEN, D_MODEL), 0.02),
        'w_in_even': nrm(k[11], (N_EVEN, D_MODEL, EVEN_IN), D_MODEL ** -0.5),
        'conv_w': nrm(k[12], (N_EVEN, CONV_W, B_WIDTH), CONV_W ** -0.5),
        'conv_b': nrm(k[13], (N_EVEN, B_WIDTH), 0.02),
        'conv_ln_g': 1.0 + nrm(k[14], (N_EVEN, B_WIDTH), 0.02),
        'conv_ln_b': nrm(k[15], (N_EVEN, B_WIDTH), 0.02),
        'conv_pw_w': nrm(k[16], (N_EVEN, B_WIDTH, B_WIDTH), B_WIDTH ** -0.5),
        'conv_pw_b': nrm(k[17], (N_EVEN, B_WIDTH), 0.02),
        'w_out_even': nrm(k[18], (N_EVEN, EVEN_MIX, D_MODEL), EVEN_MIX ** -0.5),
        'norm_odd': 1.0 + nrm(k[19], (N_ODD, D_MODEL), 0.02),
        'w_in_odd': nrm(k[20], (N_ODD, D_MODEL, ODD_IN), D_MODEL ** -0.5),
        'cmp_pos': nrm(k[21], (N_ODD, 2, CMP_BLK, HEAD_DIM), 0.1),
        'cmp_w1': nrm(k[22], (N_ODD, 2, CMP_BLK, HEAD_DIM, CMP_HID), (CMP_BLK * HEAD_DIM) ** -0.5),
        'cmp_b1': nrm(k[23], (N_ODD, 2, CMP_HID), 0.02),
        'cmp_w2': nrm(k[24], (N_ODD, 2, CMP_HID, HEAD_DIM), CMP_HID ** -0.5),
        'w_out_odd': nrm(k[25], (N_ODD, C_WIDTH, D_MODEL), C_WIDTH ** -0.5),
        'final_norm': 1.0 + nrm(k[26], (D_MODEL,), 0.02),
    }


def reference(x_prompt, x_sample, cache_a_kv0, cache_a_kv1, cache_a_kv2, state_b_conv, cache_c_kv, cache_c_win,
              page_table, rel_bias, norm_even, w_in_even, conv_w, conv_b, conv_ln_g, conv_ln_b, conv_pw_w,
              conv_pw_b, w_out_even, norm_odd, w_in_odd, cmp_pos, cmp_w1, cmp_b1, cmp_w2, w_out_odd, final_norm):
    weights = (rel_bias, norm_even, w_in_even, conv_w, conv_b, conv_ln_g, conv_ln_b, conv_pw_w, conv_pw_b,
               w_out_even, norm_odd, w_in_odd, cmp_pos, cmp_w1, cmp_b1, cmp_w2, w_out_odd, final_norm)
    b = x_prompt.shape[0]
    db = x_sample.shape[0]
    dt = x_prompt.dtype
    empty_a = [jnp.zeros((N_EVEN, b, 0, 2, A_HPG, HEAD_DIM), dt) for _ in A_GROUPS]
    zero_conv = jnp.zeros((N_EVEN, b, CONV_W - 1, B_WIDTH), dt)
    empty_c = jnp.zeros((b, 0, 4, C_KV_HEADS, HEAD_DIM), dt)
    empty_win = jnp.zeros((N_ODD, b, 0, 2, C_KV_HEADS, HEAD_DIM), dt)
    y_p, a_p, conv_p, rows_p, win_p = run_trunk(x_prompt, 0, empty_a, zero_conv, lambda i: empty_c, empty_win,
                                                *weights)
    n_pages = page_table.shape[1]
    past = n_pages * cache_c_kv.shape[2]

    def paged_past(i):
        return cache_c_kv[i][page_table].reshape(db, past, 4, C_KV_HEADS, HEAD_DIM)

    y_s, a_s, conv_s, rows_s, win_s = run_trunk(x_sample, past, [cache_a_kv0, cache_a_kv1, cache_a_kv2],
                                                state_b_conv, paged_past, cache_c_win, *weights)
    return (y_p, y_s, a_p[0], a_p[1], a_p[2], conv_p, rows_p, win_p, a_s[0], a_s[1], a_s[2], conv_s, rows_s, win_s)
```

```python
import functools
import math

import numpy as np
import jax
import jax.numpy as jnp
from jax import lax
from jax.experimental import pallas as pl
from jax.experimental.pallas import tpu as pltpu

F32 = jnp.float32
BF16 = jnp.bfloat16

D_MODEL = 2048
HEAD_DIM = 128
LANES = 128
SUBLANES = 8
A_GROUPS = ((128, 1), (512, 4), (2048, 16))
A_NG = 3
A_HPG = 8
A_QKV = A_NG * A_HPG * HEAD_DIM
A_WIDTH = A_HPG * HEAD_DIM
B_WIDTH = 1024
CONV_W = 31
CONV_HALO = 32
EVEN_IN = 3 * A_QKV + A_WIDTH + 3 * B_WIDTH
C_HEADS = 16
C_KV_HEADS = 2
C_HPG = C_HEADS // C_KV_HEADS
C_WIDTH = C_HEADS * HEAD_DIM
C_KV = C_KV_HEADS * HEAD_DIM
CMP_BLK = 32
CMP_STRIDE = 16
CMP_HID = 128
SEL_BLK = 64
SEL_N = 16
WIN = 512
NUM_BUCKETS = 32
MAX_DIST = 2048
EPS = 1e-6
NEG = -1e30
SCALE = HEAD_DIM ** -0.5
VMEM_LIMIT = 56 * 1024 * 1024


def _cparams(*sem):
    return pltpu.CompilerParams(dimension_semantics=sem, vmem_limit_bytes=VMEM_LIMIT)


def _t5_bucket_np(dist):
    max_exact = NUM_BUCKETS // 2
    d = np.maximum(np.asarray(dist, np.int64), 0)
    ratio = np.log(np.maximum(d, 1).astype(np.float64) / max_exact) / math.log(MAX_DIST / max_exact)
    large = np.minimum(max_exact + (ratio * (NUM_BUCKETS - max_exact)).astype(np.int64), NUM_BUCKETS - 1)
    return np.where(d < max_exact, d, large).astype(np.int32)


def _silu(x):
    return x * jax.nn.sigmoid(x)


def _norm_matmul_kernel(x_ref, g_ref, w_ref, o_ref, xn_ref):
    @pl.when(pl.program_id(1) == 0)
    def _():
        x = x_ref[...]
        ms = jnp.mean(x * x, axis=-1, keepdims=True)
        xn_ref[...] = (x * lax.rsqrt(ms + EPS) * g_ref[...]).astype(BF16)

    o_ref[...] = jnp.dot(xn_ref[...], w_ref[...].astype(BF16), preferred_element_type=F32)


def _norm_matmul(x, g, w, *, tm, tn):
    m, k = x.shape
    n = w.shape[1]
    return pl.pallas_call(
        _norm_matmul_kernel,
        grid=(m // tm, n // tn),
        in_specs=[pl.BlockSpec((tm, k), lambda i, j: (i, 0)),
                  pl.BlockSpec((1, k), lambda i, j: (0, 0)),
                  pl.BlockSpec((k, tn), lambda i, j: (0, j))],
        out_specs=pl.BlockSpec((tm, tn), lambda i, j: (i, j)),
        out_shape=jax.ShapeDtypeStruct((m, n), F32),
        scratch_shapes=[pltpu.VMEM((tm, k), BF16)],
        compiler_params=_cparams("parallel", "arbitrary"),
        name="norm_matmul",
    )(x, g.reshape(1, k), w)


def _matmul_res_kernel(a_ref, w_ref, r_ref, o_ref):
    o_ref[...] = r_ref[...] + jnp.dot(a_ref[...], w_ref[...].astype(BF16), preferred_element_type=F32)


def _matmul_res(a, w, res, *, tm, tn):
    m, k = a.shape
    n = w.shape[1]
    return pl.pallas_call(
        _matmul_res_kernel,
        grid=(m // tm, n // tn),
        in_specs=[pl.BlockSpec((tm, k), lambda i, j: (i, 0)),
                  pl.BlockSpec((k, tn), lambda i, j: (0, j)),
                  pl.BlockSpec((tm, tn), lambda i, j: (i, j))],
        out_specs=pl.BlockSpec((tm, tn), lambda i, j: (i, j)),
        out_shape=jax.ShapeDtypeStruct((m, n), F32),
        compiler_params=_cparams("parallel", "arbitrary"),
        name="matmul_res",
    )(a, w, res)


def _rms_kernel(x_ref, g_ref, o_ref):
    x = x_ref[...]
    ms = jnp.mean(x * x, axis=-1, keepdims=True)
    o_ref[...] = x * lax.rsqrt(ms + EPS) * g_ref[...]


def _rms_norm(x, g, *, tm):
    m, k = x.shape
    return pl.pallas_call(
        _rms_kernel,
        grid=(m // tm,),
        in_specs=[pl.BlockSpec((tm, k), lambda i: (i, 0)), pl.BlockSpec((1, k), lambda i: (0, 0))],
        out_specs=pl.BlockSpec((tm, k), lambda i: (i, 0)),
        out_shape=jax.ShapeDtypeStruct((m, k), F32),
        compiler_params=_cparams("parallel"),
        name="rms_norm",
    )(x, g.reshape(1, k))


def _tile_attn_kernel(*refs, nh, rep, n_tiles, k_col, v_col, shared_kv, with_lse, lead_axis):
    q_ref = refs[0]
    if shared_kv:
        k_refs = v_refs = refs[1:1 + n_tiles]
        nxt = 1 + n_tiles
    else:
        k_refs = refs[1:1 + n_tiles]
        v_refs = refs[1 + n_tiles:1 + 2 * n_tiles]
        nxt = 1 + 2 * n_tiles
    b_ref = refs[nxt]
    o_ref = refs[nxt + 1]
    lse_ref = refs[nxt + 2] if with_lse else None
    tq = q_ref.shape[0]
    tk = k_refs[0].shape[0]
    lk = n_tiles * tk
    if n_tiles > 1:
        qi = pl.program_id(lead_axis)
        col = lax.broadcasted_iota(jnp.int32, (tq, lk), 1)
        pad_mask = jnp.where(col < (n_tiles - 1 - qi) * tk, NEG, 0.0).astype(F32)
    else:
        pad_mask = None
    if with_lse:
        lane = lax.broadcasted_iota(jnp.int32, (tq, LANES), 1)
        lse_t = jnp.zeros((tq, LANES), F32)
    for h in range(nh):
        kc = k_col + (h // rep) * HEAD_DIM
        vc = v_col + (h // rep) * HEAD_DIM
        q = (q_ref[:, h * HEAD_DIM:(h + 1) * HEAD_DIM] * SCALE).astype(BF16)
        if n_tiles > 1:
            k = jnp.concatenate([r[:, kc:kc + HEAD_DIM] for r in k_refs], axis=0).astype(BF16)
            v = jnp.concatenate([r[:, vc:vc + HEAD_DIM] for r in v_refs], axis=0).astype(BF16)
        else:
            k = k_refs[0][:, kc:kc + HEAD_DIM].astype(BF16)
            v = v_refs[0][:, vc:vc + HEAD_DIM].astype(BF16)
        s = lax.dot_general(q, k, (((1,), (1,)), ((), ())), preferred_element_type=F32)
        s = s + b_ref[h]
        if pad_mask is not None:
            s = s + pad_mask
        m = jnp.max(s, axis=-1, keepdims=True)
        e = jnp.exp(s - m)
        den = jnp.sum(e, axis=-1, keepdims=True)
        o = jnp.dot(e.astype(BF16), v, preferred_element_type=F32) / den
        o_ref[:, h * HEAD_DIM:(h + 1) * HEAD_DIM] = o
        if with_lse:
            lse_t = jnp.where(lane == h, m + jnp.log(den), lse_t)
    if with_lse:
        lse_ref[...] = lse_t


def _a_bias_prompt(rel_bias, g, tu):
    win, dil = A_GROUPS[g]
    i = np.arange(tu)[:, None]
    c = np.arange(2 * tu)[None, :]
    j = i + tu - c
    valid = (j >= 0) & (j <= win // dil)
    bucket = _t5_bucket_np(np.clip(j, 0, None) * dil)
    tab = rel_bias[:, g * A_HPG:(g + 1) * A_HPG]
    bias = jnp.transpose(tab[bucket], (2, 0, 1))
    return jnp.where(jnp.asarray(valid)[None], bias, NEG).astype(F32)


def _a_attn_prompt(z, rel_bias, g):
    b, t, n = z.shape
    win, dil = A_GROUPS[g]
    tu = win // dil
    assert tu == 128 and t % (dil * tu) == 0 and n % A_WIDTH == 0
    ncb = n // A_WIDTH
    nu = t // (dil * tu)
    zr = z.reshape(b, t // dil, dil * n)
    bias = _a_bias_prompt(rel_bias, g, tu)
    blk = (None, tu, A_WIDTH)
    kern = functools.partial(_tile_attn_kernel, nh=A_HPG, rep=1, n_tiles=2, k_col=0, v_col=0,
                             shared_kv=False, with_lse=True, lead_axis=2)
    o, lse = pl.pallas_call(
        kern,
        grid=(b, dil, nu),
        in_specs=[
            pl.BlockSpec(blk, lambda bi, r, u: (bi, u, r * ncb + g)),
            pl.BlockSpec(blk, lambda bi, r, u: (bi, jnp.maximum(u - 1, 0), r * ncb + 3 + g)),
            pl.BlockSpec(blk, lambda bi, r, u: (bi, u, r * ncb + 3 + g)),
            pl.BlockSpec(blk, lambda bi, r, u: (bi, jnp.maximum(u - 1, 0), r * ncb + 6 + g)),
            pl.BlockSpec(blk, lambda bi, r, u: (bi, u, r * ncb + 6 + g)),
            pl.BlockSpec((A_HPG, tu, 2 * tu), lambda bi, r, u: (0, 0, 0)),
        ],
        out_specs=[pl.BlockSpec(blk, lambda bi, r, u: (bi, u, r)),
                   pl.BlockSpec((None, tu, LANES), lambda bi, r, u: (bi, u, r))],
        out_shape=[jax.ShapeDtypeStruct((b, t // dil, dil * A_WIDTH), F32),
                   jax.ShapeDtypeStruct((b, t // dil, dil * LANES), F32)],
        compiler_params=_cparams("parallel", "parallel", "arbitrary"),
        name=f"a_attn_prompt_g{g}",
    )(zr, zr, zr, zr, zr, bias)
    return o.reshape(b, t, A_WIDTH), lse.reshape(b, t, LANES)


def _a_bias_sample(rel_bias, g, hist_len, tq):
    win, dil = A_GROUPS[g]
    lk = hist_len + tq
    qidx = hist_len + np.arange(tq)[:, None]
    kidx = np.arange(lk)[None, :]
    dist = qidx - kidx
    valid = (dist >= 0) & (dist <= win) & (dist % dil == 0)
    bucket = _t5_bucket_np(np.clip(dist, 0, None))
    tab = rel_bias[:, g * A_HPG:(g + 1) * A_HPG]
    bias = jnp.transpose(tab[bucket], (2, 0, 1))
    return jnp.where(jnp.asarray(valid)[None], bias, NEG).astype(F32)


def _a_attn_sample(z, kv_full, rel_bias, g):
    b, tq, n = z.shape
    lk = kv_full.shape[1]
    bias = _a_bias_sample(rel_bias, g, lk - tq, tq)
    kern = functools.partial(_tile_attn_kernel, nh=A_HPG, rep=1, n_tiles=1, k_col=0, v_col=A_WIDTH,
                             shared_kv=True, with_lse=True, lead_axis=0)
    o, lse = pl.pallas_call(
        kern,
        grid=(b,),
        in_specs=[pl.BlockSpec((None, tq, A_WIDTH), lambda bi: (bi, 0, g)),
                  pl.BlockSpec((None, lk, 2 * A_WIDTH), lambda bi: (bi, 0, 0)),
                  pl.BlockSpec((A_HPG, tq, lk), lambda bi: (0, 0, 0))],
        out_specs=[pl.BlockSpec((None, tq, A_WIDTH), lambda bi: (bi, 0, 0)),
                   pl.BlockSpec((None, tq, LANES), lambda bi: (bi, 0, 0))],
        out_shape=[jax.ShapeDtypeStruct((b, tq, A_WIDTH), F32),
                   jax.ShapeDtypeStruct((b, tq, LANES), F32)],
        compiler_params=_cparams("parallel"),
        name=f"a_attn_sample_g{g}",
    )(z, kv_full, bias)
    return o, lse


def _conv_kernel(a_ref, gt_ref, gb_ref, hist_ref, cw_ref, cb_ref, lg_ref, lb_ref, pw_ref, pb_ref,
                 o_ref, nc_ref, ubuf, ybuf, pwb):
    ti = pl.program_id(1)
    nt = pl.num_programs(1)
    tt = a_ref.shape[0]
    nhist = CONV_W - 1
    pad = CONV_HALO - nhist

    @pl.when(ti == 0)
    def _():
        pwb[...] = pw_ref[...].astype(BF16)
        ubuf[0:pad, :] = jnp.zeros((pad, B_WIDTH), F32)
        ubuf[pad:CONV_HALO, :] = hist_ref[...]

    ubuf[CONV_HALO:CONV_HALO + tt, :] = a_ref[...] * jax.nn.sigmoid(gt_ref[...])
    for c in range(B_WIDTH // LANES):
        cs = slice(c * LANES, (c + 1) * LANES)
        acc = jnp.zeros((tt, LANES), F32) + cb_ref[:, cs]
        for k in range(CONV_W):
            acc = acc + ubuf[pad + k:pad + k + tt, cs] * cw_ref[k:k + 1, cs]
        ybuf[:, cs] = acc
    y = ybuf[...]
    mu = jnp.mean(y, axis=-1, keepdims=True)
    yc = y - mu
    var = jnp.mean(yc * yc, axis=-1, keepdims=True)
    yn = yc * lax.rsqrt(var + EPS) * lg_ref[...] + lb_ref[...]
    act = _silu(yn).astype(BF16)
    ob = jnp.dot(act, pwb[...], preferred_element_type=F32) + pb_ref[...]
    o_ref[...] = ob * _silu(gb_ref[...])

    @pl.when(ti == nt - 1)
    def _():
        nc_ref[...] = ubuf[tt + pad:tt + CONV_HALO, :]

    @pl.when(ti < nt - 1)
    def _():
        ubuf[0:CONV_HALO, :] = ubuf[tt:tt + CONV_HALO, :]


def _conv_module(z, hist, conv_w, conv_b, ln_g, ln_b, pw_w, pw_b, *, tt):
    b, t, n = z.shape
    assert t % tt == 0 and (tt >= CONV_HALO or t == tt)
    glu0 = (3 * A_QKV + A_WIDTH) // B_WIDTH
    row = lambda a: a.reshape(1, B_WIDTH)
    blk = (None, tt, B_WIDTH)
    cst = lambda shape: pl.BlockSpec(shape, lambda bi, ti: (0,) * len(shape))
    return pl.pallas_call(
        _conv_kernel,
        grid=(b, t // tt),
        in_specs=[pl.BlockSpec(blk, lambda bi, ti: (bi, ti, glu0)),
                  pl.BlockSpec(blk, lambda bi, ti: (bi, ti, glu0 + 1)),
                  pl.BlockSpec(blk, lambda bi, ti: (bi, ti, glu0 + 2)),
                  pl.BlockSpec((None, CONV_W - 1, B_WIDTH), lambda bi, ti: (bi, 0, 0)),
                  cst((CONV_W, B_WIDTH)), cst((1, B_WIDTH)), cst((1, B_WIDTH)), cst((1, B_WIDTH)),
                  cst((B_WIDTH, B_WIDTH)), cst((1, B_WIDTH))],
        out_specs=[pl.BlockSpec(blk, lambda bi, ti: (bi, ti, 0)),
                   pl.BlockSpec((None, CONV_W - 1, B_WIDTH), lambda bi, ti: (bi, 0, 0))],
        out_shape=[jax.ShapeDtypeStruct((b, t, B_WIDTH), F32),
                   jax.ShapeDtypeStruct((b, CONV_W - 1, B_WIDTH), F32)],
        scratch_shapes=[pltpu.VMEM((CONV_HALO + tt, B_WIDTH), F32),
                        pltpu.VMEM((tt, B_WIDTH), F32),
                        pltpu.VMEM((B_WIDTH, B_WIDTH), BF16)],
        compiler_params=_cparams("parallel", "arbitrary"),
        name="conv_module",
    )(z, z, z, hist, conv_w, row(conv_b), row(ln_g), row(ln_b), pw_w, row(pw_b))


def _even_mix_kernel(o0_ref, o1_ref, o2_ref, l0_ref, l1_ref, l2_ref, ga_ref, ob_ref, mix_ref):
    ls = [l0_ref[...], l1_ref[...], l2_ref[...]]
    m = jnp.maximum(jnp.maximum(ls[0], ls[1]), ls[2])
    es = [jnp.exp(l - m) for l in ls]
    inv = 1.0 / (es[0] + es[1] + es[2])
    ws = [e * inv for e in es]
    o_refs = (o0_ref, o1_ref, o2_ref)
    for h in range(A_HPG):
        cs = slice(h * HEAD_DIM, (h + 1) * HEAD_DIM)
        acc = ws[0][:, h:h + 1] * o_refs[0][:, cs]
        for g in range(1, A_NG):
            acc = acc + ws[g][:, h:h + 1] * o_refs[g][:, cs]
        mix_ref[:, cs] = (acc * _silu(ga_ref[:, cs])).astype(BF16)
    mix_ref[:, A_WIDTH:] = ob_ref[...].astype(BF16)


def _even_mix(oas, lses, z2, ob, *, tm):
    m = z2.shape[0]
    ga_blk = 3 * A_QKV // A_WIDTH
    wide = lambda c: pl.BlockSpec((tm, A_WIDTH), lambda i: (i, c))
    narrow = pl.BlockSpec((tm, LANES), lambda i: (i, 0))
    return pl.pallas_call(
        _even_mix_kernel,
        grid=(m // tm,),
        in_specs=[wide(0), wide(0), wide(0), narrow, narrow, narrow, wide(ga_blk), wide(0)],
        out_specs=pl.BlockSpec((tm, A_WIDTH + B_WIDTH), lambda i: (i, 0)),
        out_shape=jax.ShapeDtypeStruct((m, A_WIDTH + B_WIDTH), BF16),
        compiler_params=_cparams("parallel"),
        name="even_mix",
    )(*oas, *lses, z2, ob)


def _even_layer(x, a_hists, conv_hist, rel_bias, norm_g, w_in, conv_w, conv_b, ln_g, ln_b, pw_w, pw_b, w_out,
                *, tm, tn_in, tn_out, tt):
    b, t, d = x.shape
    m = b * t
    x2 = x.reshape(m, d)
    z2 = _norm_matmul(x2, norm_g, w_in, tm=tm, tn=tn_in)
    z = z2.reshape(b, t, EVEN_IN)
    oas, lses, new_hists = [], [], []
    for g, (win, _) in enumerate(A_GROUPS):
        kcol = A_QKV + g * A_WIDTH
        vcol = 2 * A_QKV + g * A_WIDTH
        new_kv = jnp.stack([z[:, :, kcol:kcol + A_WIDTH], z[:, :, vcol:vcol + A_WIDTH]], axis=2)
        if a_hists is None:
            o, lse = _a_attn_prompt(z, rel_bias, g)
            new_hists.append(new_kv[:, t - min(win, t):].reshape(b, min(win, t), 2, A_HPG, HEAD_DIM))
        else:
            hist = a_hists[g].reshape(b, -1, 2, A_WIDTH)
            full = jnp.concatenate([hist, new_kv], axis=1)
            lk = full.shape[1]
            o, lse = _a_attn_sample(z, full.reshape(b, lk, 2 * A_WIDTH), rel_bias, g)
            keep = min(win, lk)
            new_hists.append(full[:, lk - keep:].reshape(b, keep, 2, A_HPG, HEAD_DIM))
        oas.append(o.reshape(m, A_WIDTH))
        lses.append(lse.reshape(m, LANES))
    ob, new_conv = _conv_module(z, conv_hist, conv_w, conv_b, ln_g, ln_b, pw_w, pw_b, tt=tt)
    mix = _even_mix(oas, lses, z2, ob.reshape(m, B_WIDTH), tm=min(tm, 256))
    y = _matmul_res(mix, w_out, x2, tm=tm, tn=tn_out)
    return y.reshape(b, t, d), new_hists, new_conv


ODD_KV0 = C_WIDTH
ODD_GL0 = C_WIDTH + 6 * C_KV
ODD_GL_PAD = 512
ODD_GP0 = ODD_GL0 + ODD_GL_PAD
ODD_Z = ODD_GP0 + C_WIDTH
CMP_HALF = CMP_BLK // 2
CMP_K = CMP_HALF * HEAD_DIM
PAGES_PER_STEP = 16


def _repack_w_in_odd(w):
    k = w.shape[0]
    n_gl = 3 * C_HEADS
    return jnp.concatenate([w[:, :ODD_GL0 + n_gl], jnp.zeros((k, ODD_GL_PAD - n_gl), w.dtype),
                            w[:, ODD_GL0 + n_gl:]], axis=1)


def _cmp_partials_kernel(*refs, n_src, n_prefetch):
    refs = refs[n_prefetch:]
    src = refs[:n_src]
    pos_ref, w1_ref, a_ref, b_ref, rows_ref = refs[n_src:n_src + 5]
    n = rows_ref.shape[1] // CMP_HALF
    r0 = 0
    for r in src:
        for c in range(2 * C_KV_HEADS):
            rows_ref[c, r0:r0 + r.shape[0], :] = r[:, c * HEAD_DIM:(c + 1) * HEAD_DIM]
        r0 += r.shape[0]
    for kv in range(2):
        x = jnp.concatenate(
            [jnp.concatenate([rows_ref[kv * C_KV_HEADS + g, pl.ds(l, n, stride=CMP_HALF), :]
                              for l in range(CMP_HALF)], axis=1) for g in range(C_KV_HEADS)], axis=0)
        for half, out in ((0, a_ref), (1, b_ref)):
            y = jnp.dot((x + pos_ref[kv, half]).astype(BF16), w1_ref[kv, half].astype(BF16),
                        preferred_element_type=F32)
            for g in range(C_KV_HEADS):
                out[kv * C_KV_HEADS + g] = y[g * n:(g + 1) * n]


def _cmp_weights(cmp_pos, cmp_w1):
    pos = cmp_pos.reshape(2, 2, 1, CMP_K)
    w1 = cmp_w1.reshape(2, 2, CMP_K, CMP_HID)
    return pos, w1


def _cmp_partials_prompt(z, cmp_pos, cmp_w1):
    b, t, _ = z.shape
    nch = t // CMP_STRIDE
    pos, w1 = _cmp_weights(cmp_pos, cmp_w1)
    kern = functools.partial(_cmp_partials_kernel, n_src=1, n_prefetch=0)
    out = jax.ShapeDtypeStruct((b, 4, nch, CMP_HID), F32)
    ospec = pl.BlockSpec((None, 4, nch, CMP_HID), lambda bi: (bi, 0, 0, 0))
    return pl.pallas_call(
        kern,
        grid=(b,),
        in_specs=[pl.BlockSpec((None, t, 2 * C_KV), lambda bi: (bi, 0, ODD_KV0 // (2 * C_KV))),
                  pl.BlockSpec(pos.shape, lambda bi: (0, 0, 0, 0)),
                  pl.BlockSpec(w1.shape, lambda bi: (0, 0, 0, 0))],
        out_specs=[ospec, ospec],
        out_shape=[out, out],
        scratch_shapes=[pltpu.VMEM((2 * C_KV_HEADS, t, HEAD_DIM), F32)],
        compiler_params=_cparams("parallel"),
        name="cmp_partials_prompt",
    )(z, pos, w1)


def _cmp_partials_paged(cache, layer, page_table, cmp_pos, cmp_w1):
    b, n_pages = page_table.shape
    page = cache.shape[2]
    pps = PAGES_PER_STEP
    assert n_pages % pps == 0 and page % CMP_STRIDE == 0
    nch_step = pps * page // CMP_STRIDE
    nch = n_pages * page // CMP_STRIDE
    pos, w1 = _cmp_weights(cmp_pos, cmp_w1)
    kern = functools.partial(_cmp_partials_kernel, n_src=pps, n_prefetch=1)
    out = jax.ShapeDtypeStruct((b, 4, nch, CMP_HID), F32)
    ospec = pl.BlockSpec((None, 4, nch_step, CMP_HID), lambda bi, pg, pt: (bi, 0, pg, 0))

    def page_spec(j):
        return pl.BlockSpec((None, None, page, 2 * C_KV), lambda bi, pg, pt: (layer, pt[bi, pg * pps + j], 0, 0))

    grid_spec = pltpu.PrefetchScalarGridSpec(
        num_scalar_prefetch=1,
        grid=(b, n_pages // pps),
        in_specs=[page_spec(j) for j in range(pps)] + [
            pl.BlockSpec(pos.shape, lambda bi, pg, pt: (0, 0, 0, 0)),
            pl.BlockSpec(w1.shape, lambda bi, pg, pt: (0, 0, 0, 0))],
        out_specs=[ospec, ospec],
        scratch_shapes=[pltpu.VMEM((2 * C_KV_HEADS, pps * page, HEAD_DIM), F32)])
    return pl.pallas_call(
        kern, grid_spec=grid_spec, out_shape=[out, out],
        compiler_params=_cparams("parallel", "arbitrary"),
        name="cmp_partials_paged",
    )(page_table, *([cache] * pps), pos, w1)


def _overlap_np(n_ch, n_sel, ns_pad):
    ci = np.arange(n_ch)[:, None]
    si = np.arange(ns_pad)[None, :]
    ov = (ci * CMP_STRIDE < (si + 1) * SEL_BLK) & (ci * CMP_STRIDE + CMP_BLK > si * SEL_BLK)
    ov &= (ci < n_ch - 1) & (si < n_sel)
    return ov.astype(np.float32)


def _cmp_select_kernel(q_ref, ak_ref, bk_ref, av_ref, bv_ref, b1_ref, w2_ref, ov_ref, oc_ref, sel_ref,
                       *, qpos0, n_sel):
    qi = pl.program_id(2)
    tq = q_ref.shape[0]
    nch = ak_ref.shape[0]
    ns_pad = ov_ref.shape[1]

    def finish(a_ref, b_ref, kv):
        hid = a_ref[...] + pltpu.roll(b_ref[...], nch - 1, 0) + b1_ref[kv:kv + 1, :]
        return jnp.dot(_silu(hid).astype(BF16), w2_ref[kv].astype(BF16),
                       preferred_element_type=F32).astype(BF16)

    k_cmp = finish(ak_ref, bk_ref, 0)
    v_cmp = finish(av_ref, bv_ref, 1)
    qpos = qpos0 + qi * tq + lax.broadcasted_iota(jnp.int32, (tq, nch), 0)
    cmp_end = lax.broadcasted_iota(jnp.int32, (tq, nch), 1) * CMP_STRIDE + (CMP_BLK - 1)
    ok = (cmp_end <= qpos) & (cmp_end < (nch - 1) * CMP_STRIDE + CMP_BLK - 1)
    psum = jnp.zeros((tq, nch), F32)
    for h in range(C_HPG):
        cs = slice(h * HEAD_DIM, (h + 1) * HEAD_DIM)
        q = (q_ref[:, cs] * SCALE).astype(BF16)
        s = lax.dot_general(q, k_cmp, (((1,), (1,)), ((), ())), preferred_element_type=F32)
        s = jnp.where(ok, s, NEG)
        m = jnp.max(s, axis=-1, keepdims=True)
        e = jnp.where(ok, jnp.exp(s - m), 0.0)
        den = jnp.sum(e, axis=-1, keepdims=True)
        p = e / jnp.where(den > 0.0, den, 1.0)
        psum = psum + p
        oc_ref[:, cs] = jnp.dot(p.astype(BF16), v_cmp, preferred_element_type=F32)
    imp = jnp.dot(psum, ov_ref[...], preferred_element_type=F32, precision=lax.Precision.HIGHEST)
    blk = lax.broadcasted_iota(jnp.int32, (tq, ns_pad), 1)
    qblk = (qpos0 + qi * tq + lax.broadcasted_iota(jnp.int32, (tq, ns_pad), 0)) // SEL_BLK
    forced = (blk == 0) | (blk == qblk) | (blk == qblk - 1)
    allowed = (blk <= qblk) & (blk < n_sel)
    score = jnp.where(allowed, jnp.where(forced, -NEG, imp), -1.0)
    rank = jnp.zeros((tq, ns_pad), jnp.int32)
    for j in range(n_sel):
        cj = score[:, j:j + 1]
        before = (cj > score) | ((cj == score) & (j < blk))
        rank = rank + before.astype(jnp.int32)
    sel_ref[...] = (allowed & (rank < SEL_N)).astype(F32)


def _cmp_select(z, a, bm, cmp_b1, cmp_w2, *, tq, qpos0, n_sel):
    b, t, _ = z.shape
    nch = a.shape[2]
    ns_pad = -(-n_sel // LANES) * LANES
    ov = jnp.asarray(_overlap_np(nch, n_sel, ns_pad))
    kern = functools.partial(_cmp_select_kernel, qpos0=qpos0, n_sel=n_sel)
    part = lambda kv: pl.BlockSpec((None, None, nch, CMP_HID), lambda bi, g, qi: (bi, kv * C_KV_HEADS + g, 0, 0))
    return pl.pallas_call(
        kern,
        grid=(b, C_KV_HEADS, t // tq),
        in_specs=[pl.BlockSpec((None, tq, C_HPG * HEAD_DIM), lambda bi, g, qi: (bi, qi, g)),
                  part(0), part(0), part(1), part(1),
                  pl.BlockSpec((2, CMP_HID), lambda bi, g, qi: (0, 0)),
                  pl.BlockSpec((2, CMP_HID, HEAD_DIM), lambda bi, g, qi: (0, 0, 0)),
                  pl.BlockSpec((nch, ns_pad), lambda bi, g, qi: (0, 0))],
        out_specs=[pl.BlockSpec((None, tq, C_HPG * HEAD_DIM), lambda bi, g, qi: (bi, qi, g)),
                   pl.BlockSpec((None, None, tq, ns_pad), lambda bi, g, qi: (bi, g, qi, 0))],
        out_shape=[jax.ShapeDtypeStruct((b, t, C_WIDTH), F32),
                   jax.ShapeDtypeStruct((b, C_KV_HEADS, t, ns_pad), F32)],
        compiler_params=_cparams("parallel", "parallel", "arbitrary"),
        name="cmp_select",
    )(z, a, bm, a, bm, cmp_b1, cmp_w2, ov)


def _c_tab(rel_bias):
    return rel_bias[:, :C_HEADS]


def _c_bias_tiles(rel_bias, n_delta, tq, max_dist):
    delta = np.arange(n_delta)[:, None, None]
    i = np.arange(tq)[None, :, None]
    c = np.arange(tq)[None, None, :]
    dist = delta * tq + i - c
    valid = (dist >= 0) & (dist <= max_dist)
    bias = jnp.transpose(_c_tab(rel_bias)[_t5_bucket_np(np.clip(dist, 0, None))], (3, 0, 1, 2))
    return jnp.where(jnp.asarray(valid)[None], bias, NEG).astype(F32)


def _sel_prompt_kernel(q_ref, k_ref, v_ref, b_ref, sel_ref, o_ref, msk_ref):
    qi = pl.program_id(2)
    tq = q_ref.shape[0]
    n_tiles = msk_ref.shape[0]
    ns_pad = sel_ref.shape[1]
    sel = sel_ref[...].astype(BF16)
    srow = lax.broadcasted_iota(jnp.int32, (ns_pad, tq), 0)
    scol = lax.broadcasted_iota(jnp.int32, (ns_pad, tq), 1) // SEL_BLK
    per_tile = tq // SEL_BLK
    for t in range(n_tiles):
        expand = (srow == scol + t * per_tile).astype(BF16)
        hit = jnp.dot(sel, expand, preferred_element_type=F32)
        msk_ref[t] = jnp.where(hit > 0.5, 0.0, NEG)

    for h in range(C_HPG):
        cs = slice(h * HEAD_DIM, (h + 1) * HEAD_DIM)
        q = (q_ref[:, cs] * SCALE).astype(BF16)

        def body(ki, carry):
            m, l, acc = carry
            rows = pl.ds(pl.multiple_of(ki * tq, tq), tq)
            k = k_ref[rows, :].astype(BF16)
            v = v_ref[rows, :].astype(BF16)
            s = lax.dot_general(q, k, (((1,), (1,)), ((), ())), preferred_element_type=F32)
            s = s + b_ref[h, qi - ki] + msk_ref[ki]
            m_new = jnp.maximum(m, jnp.max(s, axis=-1, keepdims=True))
            alpha = jnp.exp(m - m_new)
            e = jnp.exp(s - m_new)
            l = alpha * l + jnp.sum(e, axis=-1, keepdims=True)
            acc = alpha * acc + jnp.dot(e.astype(BF16), v, preferred_element_type=F32)
            return m_new, l, acc

        init = (jnp.full((tq, 1), NEG, F32), jnp.zeros((tq, 1), F32), jnp.zeros((tq, HEAD_DIM), F32))
        m, l, acc = lax.fori_loop(0, qi + 1, body, init)
        o_ref[:, cs] = acc / l


def _sel_attn_prompt(z, sel, rel_bias, *, tq):
    b, t, _ = z.shape
    nt = t // tq
    ns_pad = sel.shape[3]
    bias = _c_bias_tiles(rel_bias, nt, tq, t)
    kcol = (ODD_KV0 + 2 * C_KV) // HEAD_DIM
    vcol = (ODD_KV0 + 3 * C_KV) // HEAD_DIM
    return pl.pallas_call(
        _sel_prompt_kernel,
        grid=(b, C_KV_HEADS, nt),
        in_specs=[pl.BlockSpec((None, tq, C_HPG * HEAD_DIM), lambda bi, g, qi: (bi, qi, g)),
                  pl.BlockSpec((None, t, HEAD_DIM), lambda bi, g, qi: (bi, 0, kcol + g)),
                  pl.BlockSpec((None, t, HEAD_DIM), lambda bi, g, qi: (bi, 0, vcol + g)),
                  pl.BlockSpec((C_HPG, nt, tq, tq), lambda bi, g, qi: (g, 0, 0, 0)),
                  pl.BlockSpec((None, None, tq, ns_pad), lambda bi, g, qi: (bi, g, qi, 0))],
        out_specs=pl.BlockSpec((None, tq, C_HPG * HEAD_DIM), lambda bi, g, qi: (bi, qi, g)),
        out_shape=jax.ShapeDtypeStruct((b, t, C_WIDTH), F32),
        scratch_shapes=[pltpu.VMEM((nt, tq, tq), F32)],
        compiler_params=_cparams("parallel", "parallel", "arbitrary"),
        name="sel_attn_prompt",
    )(z, z, z, bias, sel)


def _sel_paged_kernel(pt_ref, *refs, n_src, past):
    del pt_ref
    pages = refs[:n_src]
    q_ref, new_ref, sel_ref, bias_ref, bnew_ref, o_ref, m_ref, l_ref, acc_ref = refs[n_src:]
    pg = pl.program_id(1)
    tq = q_ref.shape[0]
    rows = C_HPG * tq
    page = pages[0].shape[0]
    keys = n_src * page
    ns_pad = sel_ref.shape[2]

    def q_rows(g):
        return jnp.concatenate(
            [(q_ref[:, (g * C_HPG + h) * HEAD_DIM:(g * C_HPG + h + 1) * HEAD_DIM] * SCALE).astype(BF16)
             for h in range(C_HPG)], axis=0)

    @pl.when(pg == 0)
    def _():
        for g in range(C_KV_HEADS):
            k = new_ref[:, g * HEAD_DIM:(g + 1) * HEAD_DIM].astype(BF16)
            v = new_ref[:, C_KV + g * HEAD_DIM:C_KV + (g + 1) * HEAD_DIM].astype(BF16)
            s = lax.dot_general(q_rows(g), k, (((1,), (1,)), ((), ())), preferred_element_type=F32)
            s = s + bnew_ref[g * C_HPG:(g + 1) * C_HPG].reshape(rows, tq)
            m = jnp.max(s, axis=-1, keepdims=True)
            e = jnp.exp(s - m)
            m_ref[g] = m
            l_ref[g] = jnp.sum(e, axis=-1, keepdims=True)
            acc_ref[g] = jnp.dot(e.astype(BF16), v, preferred_element_type=F32)

    srow = lax.broadcasted_iota(jnp.int32, (ns_pad, keys), 0)
    scol = lax.broadcasted_iota(jnp.int32, (ns_pad, keys), 1) // SEL_BLK + pg * (keys // SEL_BLK)
    expand = (srow == scol).astype(BF16)
    for g in range(C_KV_HEADS):
        k = jnp.concatenate([r[:, g * HEAD_DIM:(g + 1) * HEAD_DIM] for r in pages], axis=0).astype(BF16)
        v = jnp.concatenate([r[:, C_KV + g * HEAD_DIM:C_KV + (g + 1) * HEAD_DIM] for r in pages],
                            axis=0).astype(BF16)
        hit = jnp.dot(sel_ref[g].astype(BF16), expand, preferred_element_type=F32)
        mask = jnp.where(hit > 0.5, 0.0, NEG)
        s = lax.dot_general(q_rows(g), k, (((1,), (1,)), ((), ())), preferred_element_type=F32)
        s = (s.reshape(C_HPG, tq, keys) + bias_ref[g * C_HPG:(g + 1) * C_HPG] + mask[None]).reshape(rows, keys)
        m_old = m_ref[g]
        m_new = jnp.maximum(m_old, jnp.max(s, axis=-1, keepdims=True))
        alpha = jnp.exp(m_old - m_new)
        e = jnp.exp(s - m_new)
        m_ref[g] = m_new
        l_ref[g] = alpha * l_ref[g] + jnp.sum(e, axis=-1, keepdims=True)
        acc_ref[g] = alpha * acc_ref[g] + jnp.dot(e.astype(BF16), v, preferred_element_type=F32)

    @pl.when(pg == pl.num_programs(1) - 1)
    def _():
        for g in range(C_KV_HEADS):
            o = acc_ref[g] / l_ref[g]
            for h in range(C_HPG):
                o_ref[:, (g * C_HPG + h) * HEAD_DIM:(g * C_HPG + h + 1) * HEAD_DIM] = o[h * tq:(h + 1) * tq]


def _sel_attn_paged(z, sel, cache, layer, page_table, rel_bias):
    b, tq, _ = z.shape
    n_pages = page_table.shape[1]
    page = cache.shape[2]
    past = n_pages * page
    pps = PAGES_PER_STEP
    keys = pps * page
    ns_pad = sel.shape[3]
    tab = _c_tab(rel_bias)
    dist = past + np.arange(tq)[:, None] - np.arange(past)[None, :]
    bias = jnp.transpose(tab[_t5_bucket_np(dist)], (2, 0, 1)).astype(F32)
    dnew = np.arange(tq)[:, None] - np.arange(tq)[None, :]
    bnew = jnp.where(jnp.asarray(dnew >= 0)[None],
                     jnp.transpose(tab[_t5_bucket_np(np.clip(dnew, 0, None))], (2, 0, 1)), NEG).astype(F32)
    kern = functools.partial(_sel_paged_kernel, n_src=pps, past=past)

    def page_spec(j):
        return pl.BlockSpec((None, None, page, 2 * C_KV), lambda bi, pg, pt: (layer, pt[bi, pg * pps + j], 0, 1))

    rows = C_HPG * tq
    grid_spec = pltpu.PrefetchScalarGridSpec(
        num_scalar_prefetch=1,
        grid=(b, n_pages // pps),
        in_specs=[page_spec(j) for j in range(pps)] + [
            pl.BlockSpec((None, tq, C_WIDTH), lambda bi, pg, pt: (bi, 0, 0)),
            pl.BlockSpec((None, tq, 2 * C_KV), lambda bi, pg, pt: (bi, 0, (ODD_KV0 + 2 * C_KV) // (2 * C_KV))),
            pl.BlockSpec((None, C_KV_HEADS, tq, ns_pad), lambda bi, pg, pt: (bi, 0, 0, 0)),
            pl.BlockSpec((C_HEADS, tq, keys), lambda bi, pg, pt: (0, 0, pg)),
            pl.BlockSpec((C_HEADS, tq, tq), lambda bi, pg, pt: (0, 0, 0))],
        out_specs=pl.BlockSpec((None, tq, C_WIDTH), lambda bi, pg, pt: (bi, 0, 0)),
        scratch_shapes=[pltpu.VMEM((C_KV_HEADS, rows, 1), F32), pltpu.VMEM((C_KV_HEADS, rows, 1), F32),
                        pltpu.VMEM((C_KV_HEADS, rows, HEAD_DIM), F32)])
    return pl.pallas_call(
        kern, grid_spec=grid_spec, out_shape=jax.ShapeDtypeStruct((b, tq, C_WIDTH), F32),
        compiler_params=_cparams("parallel", "arbitrary"),
        name="sel_attn_paged",
    )(page_table, *([cache] * pps), z, z, sel, bias, bnew)


def _win_attn_prompt(z, rel_bias, *, tq):
    b, t, _ = z.shape
    n_tiles = WIN // tq + 1
    tiles = _c_bias_tiles(rel_bias, n_tiles, tq, WIN)
    bias = jnp.concatenate([tiles[:, n_tiles - 1 - p] for p in range(n_tiles)], axis=-1)
    kv_blk = (ODD_KV0 + 4 * C_KV) // (2 * C_KV)
    kern = functools.partial(_tile_attn_kernel, nh=C_HEADS, rep=C_HPG, n_tiles=n_tiles, k_col=0, v_col=C_KV,
                             shared_kv=True, with_lse=False, lead_axis=1)

    def kv_spec(p):
        return pl.BlockSpec((None, tq, 2 * C_KV),
                            lambda bi, qi: (bi, jnp.maximum(qi - (n_tiles - 1 - p), 0), kv_blk))

    return pl.pallas_call(
        kern,
        grid=(b, t // tq),
        in_specs=[pl.BlockSpec((None, tq, C_WIDTH), lambda bi, qi: (bi, qi, 0))]
                 + [kv_spec(p) for p in range(n_tiles)]
                 + [pl.BlockSpec((C_HEADS, tq, n_tiles * tq), lambda bi, qi: (0, 0, 0))],
        out_specs=pl.BlockSpec((None, tq, C_WIDTH), lambda bi, qi: (bi, qi, 0)),
        out_shape=jax.ShapeDtypeStruct((b, t, C_WIDTH), F32),
        compiler_params=_cparams("parallel", "arbitrary"),
        name="win_attn_prompt",
    )(z, *([z] * n_tiles), bias)


def _win_attn_sample(z, win_full, rel_bias):
    b, tq, _ = z.shape
    lk = win_full.shape[1]
    dist = (lk - tq) + np.arange(tq)[:, None] - np.arange(lk)[None, :]
    valid = (dist >= 0) & (dist <= WIN)
    bias = jnp.transpose(_c_tab(rel_bias)[_t5_bucket_np(np.clip(dist, 0, None))], (2, 0, 1))
    bias = jnp.where(jnp.asarray(valid)[None], bias, NEG).astype(F32)
    kern = functools.partial(_tile_attn_kernel, nh=C_HEADS, rep=C_HPG, n_tiles=1, k_col=0, v_col=C_KV,
                             shared_kv=True, with_lse=False, lead_axis=0)
    return pl.pallas_call(
        kern,
        grid=(b,),
        in_specs=[pl.BlockSpec((None, tq, C_WIDTH), lambda bi: (bi, 0, 0)),
                  pl.BlockSpec((None, lk, 2 * C_KV), lambda bi: (bi, 0, 0)),
                  pl.BlockSpec((C_HEADS, tq, lk), lambda bi: (0, 0, 0))],
        out_specs=pl.BlockSpec((None, tq, C_WIDTH), lambda bi: (bi, 0, 0)),
        out_shape=jax.ShapeDtypeStruct((b, tq, C_WIDTH), F32),
        compiler_params=_cparams("parallel"),
        name="win_attn_sample",
    )(z, win_full, bias)


def _odd_mix_kernel(oc_ref, os_ref, ow_ref, gl_ref, gp_ref, mix_ref):
    gates = jax.nn.sigmoid(gl_ref[...])
    branches = (oc_ref, os_ref, ow_ref)
    for h in range(C_HEADS):
        cs = slice(h * HEAD_DIM, (h + 1) * HEAD_DIM)
        acc = gates[:, h:h + 1] * branches[0][:, cs]
        for br in range(1, 3):
            lane = br * C_HEADS + h
            acc = acc + gates[:, lane:lane + 1] * branches[br][:, cs]
        mix_ref[:, cs] = (acc * _silu(gp_ref[:, cs])).astype(BF16)


def _odd_mix(oc, osel, ow, z2, *, tm):
    m = z2.shape[0]
    wide = lambda c: pl.BlockSpec((tm, C_WIDTH), lambda i: (i, c))
    return pl.pallas_call(
        _odd_mix_kernel,
        grid=(m // tm,),
        in_specs=[wide(0), wide(0), wide(0),
                  pl.BlockSpec((tm, LANES), lambda i: (i, ODD_GL0 // LANES)),
                  wide(ODD_GP0 // C_WIDTH)],
        out_specs=wide(0),
        out_shape=jax.ShapeDtypeStruct((m, C_WIDTH), BF16),
        compiler_params=_cparams("parallel"),
        name="odd_mix",
    )(oc, osel, ow, z2, z2)


def _odd_layer(x, past, win_hist, rel_bias, norm_g, w_in_packed, cmp_pos, cmp_w1, cmp_b1, cmp_w2, w_out,
               *, tm, tn_in, tn_out, tq):
    b, t, d = x.shape
    m = b * t
    x2 = x.reshape(m, d)
    z2 = _norm_matmul(x2, norm_g, w_in_packed, tm=tm, tn=tn_in)
    z = z2.reshape(b, t, ODD_Z)
    rows = z[:, :, ODD_KV0:ODD_KV0 + 4 * C_KV].reshape(b, t, 4, C_KV_HEADS, HEAD_DIM)
    win_new = z[:, :, ODD_KV0 + 4 * C_KV:ODD_KV0 + 6 * C_KV]
    if past is None:
        length = t
        a, bm = _cmp_partials_prompt(z, cmp_pos, cmp_w1)
        qpos0 = 0
    else:
        cache, layer, page_table = past
        p_len = page_table.shape[1] * cache.shape[2]
        length = p_len + t
        assert length // CMP_STRIDE == p_len // CMP_STRIDE and p_len % SEL_BLK == 0
        a, bm = _cmp_partials_paged(cache, layer, page_table, cmp_pos, cmp_w1)
        qpos0 = p_len
    n_sel = -(-length // SEL_BLK)
    oc, sel = _cmp_select(z, a, bm, cmp_b1, cmp_w2, tq=tq, qpos0=qpos0, n_sel=n_sel)
    if past is None:
        osel = _sel_attn_prompt(z, sel, rel_bias, tq=tq)
        ow = _win_attn_prompt(z, rel_bias, tq=tq)
        new_win = win_new[:, t - min(WIN, t):].reshape(b, min(WIN, t), 2, C_KV_HEADS, HEAD_DIM)
    else:
        osel = _sel_attn_paged(z, sel, cache, layer, page_table, rel_bias)
        win_full = jnp.concatenate([win_hist.reshape(b, -1, 2 * C_KV), win_new], axis=1)
        ow = _win_attn_sample(z, win_full, rel_bias)
        lk = win_full.shape[1]
        keep = min(WIN, length)
        new_win = win_full[:, lk - keep:].reshape(b, keep, 2, C_KV_HEADS, HEAD_DIM)
    mix = _odd_mix(oc.reshape(m, C_WIDTH), osel.reshape(m, C_WIDTH), ow.reshape(m, C_WIDTH), z2, tm=min(tm, 256))
    y = _matmul_res(mix, w_out, x2, tm=tm, tn=tn_out)
    return y.reshape(b, t, d), rows, new_win


def _run_trunk(x, a_caches, conv_state, c_cache, c_win, page_table, rel_bias, norm_even, w_in_even, conv_w,
               conv_b, conv_ln_g, conv_ln_b, conv_pw_w, conv_pw_b, w_out_even, norm_odd, w_in_odd_packed,
               cmp_pos, cmp_w1, cmp_b1, cmp_w2, w_out_odd, final_norm, *, tm, tt, tq):
    b, t, d = x.shape
    prompt = a_caches is None
    depth = norm_even.shape[0] + norm_odd.shape[0]
    new_a = [[] for _ in A_GROUPS]
    new_conv, new_rows, new_win = [], [], []
    for layer in range(depth):
        i = layer // 2
        if layer % 2 == 0:
            hists = None if prompt else [c[i] for c in a_caches]
            conv_hist = jnp.zeros((b, CONV_W - 1, B_WIDTH), F32) if prompt else conv_state[i]
            x, hists, conv = _even_layer(x, hists, conv_hist, rel_bias, norm_even[i], w_in_even[i], conv_w[i],
                                         conv_b[i], conv_ln_g[i], conv_ln_b[i], conv_pw_w[i], conv_pw_b[i],
                                         w_out_even[i], tm=tm, tn_in=512, tn_out=512, tt=tt)
            for g in range(A_NG):
                new_a[g].append(hists[g])
            new_conv.append(conv)
        else:
            past = None if prompt else (c_cache, i, page_table)
            x, rows, win = _odd_layer(x, past, None if prompt else c_win[i], rel_bias, norm_odd[i],
                                      w_in_odd_packed[i], cmp_pos[i], cmp_w1[i], cmp_b1[i], cmp_w2[i],
                                      w_out_odd[i], tm=tm, tn_in=512, tn_out=512, tq=tq)
            new_rows.append(rows)
            new_win.append(win)
    y = _rms_norm(x.reshape(b * t, d), final_norm, tm=min(tm, 256)).reshape(b, t, d)
    return (y, [jnp.stack(a) for a in new_a], jnp.stack(new_conv), jnp.stack(new_rows), jnp.stack(new_win))


def kernel(x_prompt, x_sample, cache_a_kv0, cache_a_kv1, cache_a_kv2, state_b_conv, cache_c_kv, cache_c_win, page_table, rel_bias, norm_even, w_in_even, conv_w, conv_b, conv_ln_g, conv_ln_b, conv_pw_w, conv_pw_b, w_out_even, norm_odd, w_in_odd, cmp_pos, cmp_w1, cmp_b1, cmp_w2, w_out_odd, final_norm):
    w_in_odd_packed = jnp.stack([_repack_w_in_odd(w_in_odd[i]) for i in range(w_in_odd.shape[0])])
    weights = (rel_bias, norm_even, w_in_even, conv_w, conv_b, conv_ln_g, conv_ln_b, conv_pw_w, conv_pw_b,
               w_out_even, norm_odd, w_in_odd_packed, cmp_pos, cmp_w1, cmp_b1, cmp_w2, w_out_odd, final_norm)
    y_p, a_p, conv_p, rows_p, win_p = _run_trunk(x_prompt, None, None, None, None, None, *weights,
                                                 tm=1024, tt=256, tq=128)
    n_odd, n_pool, page = cache_c_kv.shape[:3]
    c_cache = cache_c_kv.reshape(n_odd, n_pool, page, 4 * C_KV)
    db, dt = x_sample.shape[:2]
    y_s, a_s, conv_s, rows_s, win_s = _run_trunk(x_sample, (cache_a_kv0, cache_a_kv1, cache_a_kv2), state_b_conv,
                                                 c_cache, cache_c_win, page_table, *weights,
                                                 tm=db * dt, tt=dt, tq=dt)
    return (y_p, y_s, a_p[0], a_p[1], a_p[2], conv_p, rows_p, win_p, a_s[0], a_s[1], a_s[2], conv_s, rows_s, win_s)
```

```python
import functools
import math

import numpy as np
import jax
import jax.numpy as jnp
from jax import lax
from jax.experimental import pallas as pl
from jax.experimental.pallas import tpu as pltpu

F32 = jnp.float32
BF16 = jnp.bfloat16

D_MODEL = 2048
HEAD_DIM = 128
LANES = 128
SUBLANES = 8
A_GROUPS = ((128, 1), (512, 4), (2048, 16))
A_NG = 3
A_HPG = 8
A_QKV = A_NG * A_HPG * HEAD_DIM
A_WIDTH = A_HPG * HEAD_DIM
B_WIDTH = 1024
CONV_W = 31
CONV_HALO = 32
EVEN_IN = 3 * A_QKV + A_WIDTH + 3 * B_WIDTH
C_HEADS = 16
C_KV_HEADS = 2
C_HPG = C_HEADS // C_KV_HEADS
C_WIDTH = C_HEADS * HEAD_DIM
C_KV = C_KV_HEADS * HEAD_DIM
CMP_BLK = 32
CMP_STRIDE = 16
CMP_HID = 128
SEL_BLK = 64
SEL_N = 16
WIN = 512
NUM_BUCKETS = 32
MAX_DIST = 2048
EPS = 1e-6
NEG = -1e30
SCALE = HEAD_DIM ** -0.5
VMEM_LIMIT = 56 * 1024 * 1024


def _cparams(*sem):
    return pltpu.CompilerParams(dimension_semantics=sem, vmem_limit_bytes=VMEM_LIMIT)


def _t5_bucket_np(dist):
    max_exact = NUM_BUCKETS // 2
    d = np.maximum(np.asarray(dist, np.int64), 0)
    ratio = np.log(np.maximum(d, 1).astype(np.float64) / max_exact) / math.log(MAX_DIST / max_exact)
    large = np.minimum(max_exact + (ratio * (NUM_BUCKETS - max_exact)).astype(np.int64), NUM_BUCKETS - 1)
    return np.where(d < max_exact, d, large).astype(np.int32)


def _silu(x):
    return x * jax.nn.sigmoid(x)


def _norm_matmul_kernel(x_ref, g_ref, w_ref, o_ref, xn_ref):
    @pl.when(pl.program_id(1) == 0)
    def _():
        x = x_ref[...]
        ms = jnp.mean(x * x, axis=-1, keepdims=True)
        xn_ref[...] = (x * lax.rsqrt(ms + EPS) * g_ref[...]).astype(BF16)

    o_ref[...] = jnp.dot(xn_ref[...], w_ref[...].astype(BF16), preferred_element_type=F32)


def _norm_matmul(x, g, w, layer, *, tm, tn):
    m, k = x.shape
    n = w.shape[2]
    return pl.pallas_call(
        _norm_matmul_kernel,
        grid=(m // tm, n // tn),
        in_specs=[pl.BlockSpec((tm, k), lambda i, j: (i, 0)),
                  pl.BlockSpec((1, k), lambda i, j: (0, 0)),
                  pl.BlockSpec((None, k, tn), lambda i, j: (layer, 0, j))],
        out_specs=pl.BlockSpec((tm, tn), lambda i, j: (i, j)),
        out_shape=jax.ShapeDtypeStruct((m, n), F32),
        scratch_shapes=[pltpu.VMEM((tm, k), BF16)],
        compiler_params=_cparams("parallel", "arbitrary"),
        name="norm_matmul",
    )(x, g.reshape(1, k), w)


def _matmul_res_kernel(*refs, n_parts):
    a_refs = refs[:n_parts]
    w_ref, r_ref, o_ref = refs[n_parts:]
    acc = r_ref[...]
    k0 = 0
    for a_ref in a_refs:
        kp = a_ref.shape[1]
        acc = acc + jnp.dot(a_ref[...], w_ref[k0:k0 + kp, :].astype(BF16), preferred_element_type=F32)
        k0 += kp
    o_ref[...] = acc


def _matmul_res(parts, w, layer, res, *, tm, tn):
    m = res.shape[0]
    k, n = w.shape[1:]
    assert sum(p.shape[1] for p in parts) == k
    return pl.pallas_call(
        functools.partial(_matmul_res_kernel, n_parts=len(parts)),
        grid=(m // tm, n // tn),
        in_specs=[pl.BlockSpec((tm, p.shape[1]), lambda i, j: (i, 0)) for p in parts]
                 + [pl.BlockSpec((None, k, tn), lambda i, j: (layer, 0, j)),
                    pl.BlockSpec((tm, tn), lambda i, j: (i, j))],
        out_specs=pl.BlockSpec((tm, tn), lambda i, j: (i, j)),
        out_shape=jax.ShapeDtypeStruct((m, n), F32),
        compiler_params=_cparams("parallel", "arbitrary"),
        name="matmul_res",
    )(*parts, w, res)


def _rms_kernel(x_ref, g_ref, o_ref):
    x = x_ref[...]
    ms = jnp.mean(x * x, axis=-1, keepdims=True)
    o_ref[...] = x * lax.rsqrt(ms + EPS) * g_ref[...]


def _rms_norm(x, g, *, tm):
    m, k = x.shape
    return pl.pallas_call(
        _rms_kernel,
        grid=(m // tm,),
        in_specs=[pl.BlockSpec((tm, k), lambda i: (i, 0)), pl.BlockSpec((1, k), lambda i: (0, 0))],
        out_specs=pl.BlockSpec((tm, k), lambda i: (i, 0)),
        out_shape=jax.ShapeDtypeStruct((m, k), F32),
        compiler_params=_cparams("parallel"),
        name="rms_norm",
    )(x, g.reshape(1, k))


def _c_tab(rel_bias):
    return rel_bias[:, :C_HEADS]


def _c_rev_table(rel_bias, length, tq, max_dist):
    dist = length - np.arange(length + tq)
    valid = (dist >= 0) & (dist <= max_dist)
    tab = _c_tab(rel_bias)[_t5_bucket_np(np.clip(dist, 0, None))]
    tab = jnp.where(jnp.asarray(valid)[:, None], tab, NEG).astype(F32)
    return tab.T.reshape(C_HEADS, 1, length + tq)


def _toeplitz(vec, tq):
    w = vec.shape[1]
    return pltpu.roll(jnp.broadcast_to(vec, (tq, w)), 0, 1, stride=1, stride_axis=0)[:, tq:]


def _tile_attn_kernel(*refs, nh, rep, n_tiles, k_col, v_col, shared_kv, with_lse, lead_axis, toeplitz=False):
    if toeplitz:
        rev_ref, bias_scr = refs[-1 - int(with_lse) - 2], refs[-1]
        tq_ = refs[0].shape[0]

        @pl.when(pl.program_id(lead_axis) == 0)
        def _():
            for h in range(nh):
                bias_scr[h] = _toeplitz(rev_ref[h], tq_)

        refs = list(refs[:-1])
        refs[-1 - int(with_lse) - 1] = bias_scr
    q_ref = refs[0]
    if shared_kv:
        k_refs = v_refs = refs[1:1 + n_tiles]
        nxt = 1 + n_tiles
    else:
        k_refs = refs[1:1 + n_tiles]
        v_refs = refs[1 + n_tiles:1 + 2 * n_tiles]
        nxt = 1 + 2 * n_tiles
    b_ref = refs[nxt]
    o_ref = refs[nxt + 1]
    lse_ref = refs[nxt + 2] if with_lse else None
    tq = q_ref.shape[0]
    tk = k_refs[0].shape[0]
    lk = n_tiles * tk
    if n_tiles > 1:
        qi = pl.program_id(lead_axis)
        col = lax.broadcasted_iota(jnp.int32, (tq, lk), 1)
        pad_mask = jnp.where(col < (n_tiles - 1 - qi) * tk, NEG, 0.0).astype(F32)
    else:
        pad_mask = None
    if with_lse:
        lane = lax.broadcasted_iota(jnp.int32, (tq, LANES), 1)
        lse_t = jnp.zeros((tq, LANES), F32)
    for h in range(nh):
        kc = k_col + (h // rep) * HEAD_DIM
        vc = v_col + (h // rep) * HEAD_DIM
        q = (q_ref[:, h * HEAD_DIM:(h + 1) * HEAD_DIM] * SCALE).astype(BF16)
        if n_tiles > 1:
            k = jnp.concatenate([r[:, kc:kc + HEAD_DIM] for r in k_refs], axis=0).astype(BF16)
            v = jnp.concatenate([r[:, vc:vc + HEAD_DIM] for r in v_refs], axis=0).astype(BF16)
        else:
            k = k_refs[0][:, kc:kc + HEAD_DIM].astype(BF16)
            v = v_refs[0][:, vc:vc + HEAD_DIM].astype(BF16)
        s = lax.dot_general(q, k, (((1,), (1,)), ((), ())), preferred_element_type=F32)
        s = s + b_ref[h]
        if pad_mask is not None:
            s = s + pad_mask
        m = jnp.max(s, axis=-1, keepdims=True)
        e = jnp.exp(s - m)
        den = jnp.sum(e, axis=-1, keepdims=True)
        o = jnp.dot(e.astype(BF16), v, preferred_element_type=F32) / den
        o_ref[:, h * HEAD_DIM:(h + 1) * HEAD_DIM] = o
        if with_lse:
            lse_t = jnp.where(lane == h, m + jnp.log(den), lse_t)
    if with_lse:
        lse_ref[...] = lse_t


A_TU = 128


def _a_rev_table(rel_bias):
    j = 2 * A_TU - np.arange(3 * A_TU)
    rows = []
    for g, (win, dil) in enumerate(A_GROUPS):
        valid = (j >= 0) & (j <= win // dil)
        tab = rel_bias[:, g * A_HPG:(g + 1) * A_HPG][_t5_bucket_np(np.clip(j, 0, None) * dil)]
        rows.append(jnp.where(jnp.asarray(valid)[:, None], tab, NEG).T)
    return jnp.stack(rows, axis=1).astype(F32)


def _a_prompt_kernel(q0_ref, q1_ref, q2_ref, k0_ref, k1_ref, k2_ref, v0_ref, v1_ref, v2_ref, ga_ref, rev_ref,
                     mix_ref, o_scr, l_scr, bias_scr):
    t = q0_ref.shape[0]
    tu = A_TU
    q_refs, k_refs, v_refs = (q0_ref, q1_ref, q2_ref), (k0_ref, k1_ref, k2_ref), (v0_ref, v1_ref, v2_ref)
    for g in range(A_NG):
        bias_scr[g] = _toeplitz(rev_ref[g:g + 1, :], tu)
    col = lax.broadcasted_iota(jnp.int32, (tu, 2 * tu), 1)
    for g, (win, dil) in enumerate(A_GROUPS):
        n_u = t // (dil * tu)
        q_ref, k_ref, v_ref = q_refs[g], k_refs[g], v_refs[g]

        def rows(r, u, dil=dil):
            start = r + u * (tu * dil)
            if dil == 1:
                return pl.ds(pl.multiple_of(start, tu), tu)
            return pl.ds(start, tu, stride=dil)

        def body(it, carry, g=g, n_u=n_u, q_ref=q_ref, k_ref=k_ref, v_ref=v_ref, rows=rows):
            r = it // n_u
            u = it % n_u
            cur = rows(r, u)
            q = (q_ref[cur, :] * SCALE).astype(BF16)
            if n_u > 1:
                prev = rows(r, jnp.maximum(u - 1, 0))
                k = jnp.concatenate([k_ref[prev, :], k_ref[cur, :]], axis=0).astype(BF16)
                v = jnp.concatenate([v_ref[prev, :], v_ref[cur, :]], axis=0).astype(BF16)
                bias = bias_scr[g] + jnp.where((col < tu) & (u == 0), NEG, 0.0)
            else:
                k = k_ref[cur, :].astype(BF16)
                v = v_ref[cur, :].astype(BF16)
                bias = bias_scr[g][:, tu:]
            s = lax.dot_general(q, k, (((1,), (1,)), ((), ())), preferred_element_type=F32) + bias
            m = jnp.max(s, axis=-1, keepdims=True)
            e = jnp.exp(s - m)
            den = jnp.sum(e, axis=-1, keepdims=True)
            o_scr[g, cur, :] = jnp.dot(e.astype(BF16), v, preferred_element_type=F32) / den
            l_scr[g, cur, :] = jnp.broadcast_to(m + jnp.log(den), (tu, HEAD_DIM))
            return carry

        lax.fori_loop(0, t // tu, body, 0)

    def combine(c, carry):
        cur = pl.ds(pl.multiple_of(c * tu, tu), tu)
        ls = [l_scr[g, cur, :] for g in range(A_NG)]
        m = jnp.maximum(jnp.maximum(ls[0], ls[1]), ls[2])
        es = [jnp.exp(l - m) for l in ls]
        acc = es[0] * o_scr[0, cur, :] + es[1] * o_scr[1, cur, :] + es[2] * o_scr[2, cur, :]
        out = acc / (es[0] + es[1] + es[2])
        mix_ref[cur, :] = (out * _silu(ga_ref[cur, :])).astype(BF16)
        return carry

    lax.fori_loop(0, t // tu, combine, 0)


def _a_attn_prompt(z, rel_bias):
    b, t, n = z.shape
    for win, dil in A_GROUPS:
        assert win // dil == A_TU and t % (dil * A_TU) == 0
    rev = _a_rev_table(rel_bias)
    nq = A_QKV // HEAD_DIM
    col = lambda base: pl.BlockSpec((None, t, HEAD_DIM), lambda bi, h: (bi, 0, base + h))
    return pl.pallas_call(
        _a_prompt_kernel,
        grid=(b, A_HPG),
        in_specs=[col(g * A_HPG) for g in range(A_NG)]
                 + [col(nq + g * A_HPG) for g in range(A_NG)]
                 + [col(2 * nq + g * A_HPG) for g in range(A_NG)]
                 + [col(3 * nq), pl.BlockSpec((None, A_NG, 3 * A_TU), lambda bi, h: (h, 0, 0))],
        out_specs=pl.BlockSpec((None, t, HEAD_DIM), lambda bi, h: (bi, 0, h)),
        out_shape=jax.ShapeDtypeStruct((b, t, A_WIDTH), BF16),
        scratch_shapes=[pltpu.VMEM((A_NG, t, HEAD_DIM), F32), pltpu.VMEM((A_NG, t, HEAD_DIM), F32),
                        pltpu.VMEM((A_NG, A_TU, 2 * A_TU), F32)],
        compiler_params=_cparams("parallel", "parallel"),
        name="a_attn_prompt",
    )(*([z] * 10), rev)


def _a_bias_sample(rel_bias, g, hist_len, tq):
    win, dil = A_GROUPS[g]
    lk = hist_len + tq
    qidx = hist_len + np.arange(tq)[:, None]
    kidx = np.arange(lk)[None, :]
    dist = qidx - kidx
    valid = (dist >= 0) & (dist <= win) & (dist % dil == 0)
    bucket = _t5_bucket_np(np.clip(dist, 0, None))
    tab = rel_bias[:, g * A_HPG:(g + 1) * A_HPG]
    bias = jnp.transpose(tab[bucket], (2, 0, 1))
    return jnp.where(jnp.asarray(valid)[None], bias, NEG).astype(F32)


def _a_attn_sample(z, kv_full, rel_bias, g):
    b, tq, n = z.shape
    lk = kv_full.shape[1]
    bias = _a_bias_sample(rel_bias, g, lk - tq, tq)
    kern = functools.partial(_tile_attn_kernel, nh=A_HPG, rep=1, n_tiles=1, k_col=0, v_col=A_WIDTH,
                             shared_kv=True, with_lse=True, lead_axis=0)
    o, lse = pl.pallas_call(
        kern,
        grid=(b,),
        in_specs=[pl.BlockSpec((None, tq, A_WIDTH), lambda bi: (bi, 0, g)),
                  pl.BlockSpec((None, lk, 2 * A_WIDTH), lambda bi: (bi, 0, 0)),
                  pl.BlockSpec((A_HPG, tq, lk), lambda bi: (0, 0, 0))],
        out_specs=[pl.BlockSpec((None, tq, A_WIDTH), lambda bi: (bi, 0, 0)),
                   pl.BlockSpec((None, tq, LANES), lambda bi: (bi, 0, 0))],
        out_shape=[jax.ShapeDtypeStruct((b, tq, A_WIDTH), F32),
                   jax.ShapeDtypeStruct((b, tq, LANES), F32)],
        compiler_params=_cparams("parallel"),
        name=f"a_attn_sample_g{g}",
    )(z, kv_full, bias)
    return o, lse


def _conv_kernel(a_ref, gt_ref, gb_ref, hist_ref, cw_ref, cb_ref, lg_ref, lb_ref, pw_ref, pb_ref,
                 o_ref, nc_ref, ubuf, ybuf, pwb):
    ti = pl.program_id(1)
    nt = pl.num_programs(1)
    tt = a_ref.shape[0]
    nhist = CONV_W - 1
    pad = CONV_HALO - nhist

    @pl.when(ti == 0)
    def _():
        pwb[...] = pw_ref[...].astype(BF16)
        ubuf[0:pad, :] = jnp.zeros((pad, B_WIDTH), F32)
        ubuf[pad:CONV_HALO, :] = hist_ref[...]

    ubuf[CONV_HALO:CONV_HALO + tt, :] = a_ref[...] * jax.nn.sigmoid(gt_ref[...])
    for c in range(B_WIDTH // LANES):
        cs = slice(c * LANES, (c + 1) * LANES)
        acc = jnp.zeros((tt, LANES), F32) + cb_ref[:, cs]
        for k in range(CONV_W):
            acc = acc + ubuf[pad + k:pad + k + tt, cs] * cw_ref[k:k + 1, cs]
        ybuf[:, cs] = acc
    y = ybuf[...]
    mu = jnp.mean(y, axis=-1, keepdims=True)
    yc = y - mu
    var = jnp.mean(yc * yc, axis=-1, keepdims=True)
    yn = yc * lax.rsqrt(var + EPS) * lg_ref[...] + lb_ref[...]
    act = _silu(yn).astype(BF16)
    ob = jnp.dot(act, pwb[...], preferred_element_type=F32) + pb_ref[...]
    o_ref[...] = (ob * _silu(gb_ref[...])).astype(BF16)

    @pl.when(ti == nt - 1)
    def _():
        nc_ref[...] = ubuf[tt + pad:tt + CONV_HALO, :]

    @pl.when(ti < nt - 1)
    def _():
        ubuf[0:CONV_HALO, :] = ubuf[tt:tt + CONV_HALO, :]


def _conv_module(z, hist, layer, conv_w, conv_b, ln_g, ln_b, pw_w, pw_b, *, tt):
    b, t, n = z.shape
    assert t % tt == 0 and (tt >= CONV_HALO or t == tt)
    glu0 = (3 * A_QKV + A_WIDTH) // B_WIDTH
    row = lambda a: a.reshape(a.shape[0], 1, B_WIDTH)
    blk = (None, tt, B_WIDTH)
    lyr = lambda shape: pl.BlockSpec((None,) + shape, lambda bi, ti: (layer,) + (0,) * len(shape))
    return pl.pallas_call(
        _conv_kernel,
        grid=(b, t // tt),
        in_specs=[pl.BlockSpec(blk, lambda bi, ti: (bi, ti, glu0)),
                  pl.BlockSpec(blk, lambda bi, ti: (bi, ti, glu0 + 1)),
                  pl.BlockSpec(blk, lambda bi, ti: (bi, ti, glu0 + 2)),
                  pl.BlockSpec((None, CONV_W - 1, B_WIDTH), lambda bi, ti: (bi, 0, 0)),
                  lyr((CONV_W, B_WIDTH)), lyr((1, B_WIDTH)), lyr((1, B_WIDTH)), lyr((1, B_WIDTH)),
                  lyr((B_WIDTH, B_WIDTH)), lyr((1, B_WIDTH))],
        out_specs=[pl.BlockSpec(blk, lambda bi, ti: (bi, ti, 0)),
                   pl.BlockSpec((None, CONV_W - 1, B_WIDTH), lambda bi, ti: (bi, 0, 0))],
        out_shape=[jax.ShapeDtypeStruct((b, t, B_WIDTH), BF16),
                   jax.ShapeDtypeStruct((b, CONV_W - 1, B_WIDTH), F32)],
        scratch_shapes=[pltpu.VMEM((CONV_HALO + tt, B_WIDTH), F32),
                        pltpu.VMEM((tt, B_WIDTH), F32),
                        pltpu.VMEM((B_WIDTH, B_WIDTH), BF16)],
        compiler_params=_cparams("parallel", "arbitrary"),
        name="conv_module",
    )(z, z, z, hist, conv_w, row(conv_b), row(ln_g), row(ln_b), pw_w, row(pw_b))


def _even_mix_kernel(o0_ref, o1_ref, o2_ref, l0_ref, l1_ref, l2_ref, ga_ref, mix_ref):
    ls = [l0_ref[...], l1_ref[...], l2_ref[...]]
    m = jnp.maximum(jnp.maximum(ls[0], ls[1]), ls[2])
    es = [jnp.exp(l - m) for l in ls]
    inv = 1.0 / (es[0] + es[1] + es[2])
    ws = [e * inv for e in es]
    o_refs = (o0_ref, o1_ref, o2_ref)
    for h in range(A_HPG):
        cs = slice(h * HEAD_DIM, (h + 1) * HEAD_DIM)
        acc = ws[0][:, h:h + 1] * o_refs[0][:, cs]
        for g in range(1, A_NG):
            acc = acc + ws[g][:, h:h + 1] * o_refs[g][:, cs]
        mix_ref[:, cs] = (acc * _silu(ga_ref[:, cs])).astype(BF16)


def _even_mix(oas, lses, z2, *, tm):
    m = z2.shape[0]
    ga_blk = 3 * A_QKV // A_WIDTH
    wide = lambda c: pl.BlockSpec((tm, A_WIDTH), lambda i: (i, c))
    narrow = pl.BlockSpec((tm, LANES), lambda i: (i, 0))
    return pl.pallas_call(
        _even_mix_kernel,
        grid=(m // tm,),
        in_specs=[wide(0), wide(0), wide(0), narrow, narrow, narrow, wide(ga_blk)],
        out_specs=wide(0),
        out_shape=jax.ShapeDtypeStruct((m, A_WIDTH), BF16),
        compiler_params=_cparams("parallel"),
        name="even_mix",
    )(*oas, *lses, z2)


def _even_layer(x, a_hists, conv_hist, rel_bias, layer, norm_g, w_in, conv_w, conv_b, ln_g, ln_b, pw_w, pw_b,
                w_out, *, tm, tn_in, tn_out, tt):
    b, t, d = x.shape
    m = b * t
    x2 = x.reshape(m, d)
    z2 = _norm_matmul(x2, norm_g[layer], w_in, layer, tm=tm, tn=tn_in)
    z = z2.reshape(b, t, EVEN_IN)
    oas, lses, new_hists = [], [], []
    for g, (win, _) in enumerate(A_GROUPS):
        kcol = A_QKV + g * A_WIDTH
        vcol = 2 * A_QKV + g * A_WIDTH
        new_kv = jnp.stack([z[:, :, kcol:kcol + A_WIDTH], z[:, :, vcol:vcol + A_WIDTH]], axis=2)
        if a_hists is None:
            new_hists.append(new_kv[:, t - min(win, t):].reshape(b, min(win, t), 2, A_HPG, HEAD_DIM))
        else:
            hist = a_hists[g].reshape(b, -1, 2, A_WIDTH)
            full = jnp.concatenate([hist, new_kv], axis=1)
            lk = full.shape[1]
            o, lse = _a_attn_sample(z, full.reshape(b, lk, 2 * A_WIDTH), rel_bias, g)
            keep = min(win, lk)
            new_hists.append(full[:, lk - keep:].reshape(b, keep, 2, A_HPG, HEAD_DIM))
            oas.append(o.reshape(m, A_WIDTH))
            lses.append(lse.reshape(m, LANES))
    if a_hists is None:
        mix_a = _a_attn_prompt(z, rel_bias).reshape(m, A_WIDTH)
    else:
        mix_a = _even_mix(oas, lses, z2, tm=min(tm, 256))
    mix_b, new_conv = _conv_module(z, conv_hist, layer, conv_w, conv_b, ln_g, ln_b, pw_w, pw_b, tt=tt)
    y = _matmul_res([mix_a, mix_b.reshape(m, B_WIDTH)], w_out, layer, x2, tm=tm, tn=tn_out)
    return y.reshape(b, t, d), new_hists, new_conv


ODD_KV0 = C_WIDTH
ODD_GL0 = C_WIDTH + 6 * C_KV
ODD_GL_PAD = 512
ODD_GP0 = ODD_GL0 + ODD_GL_PAD
ODD_Z = ODD_GP0 + C_WIDTH
CMP_HALF = CMP_BLK // 2
CMP_K = CMP_HALF * HEAD_DIM
PAGES_PER_STEP = 16
CACHE_ROW_KINDS = 4 * C_KV_HEADS


def _repack_w_in_odd(w):
    n_gl = 3 * C_HEADS
    pad = jnp.zeros(w.shape[:2] + (ODD_GL_PAD - n_gl,), w.dtype)
    return jnp.concatenate([w[..., :ODD_GL0 + n_gl], pad, w[..., ODD_GL0 + n_gl:]], axis=-1)


def _cmp_partials_kernel(*refs, n_src, n_prefetch, interleaved):
    refs = refs[n_prefetch:]
    src = refs[:n_src]
    pos_ref, w1_ref, a_ref, b_ref, rows_ref = refs[n_src:n_src + 5]
    n = rows_ref.shape[1] // CMP_HALF
    r0 = 0
    for r in src:
        nr = r.shape[0] // CACHE_ROW_KINDS if interleaved else r.shape[0]
        for c in range(2 * C_KV_HEADS):
            if interleaved:
                rows_ref[c, r0:r0 + nr, :] = r[pl.ds(c, nr, stride=CACHE_ROW_KINDS), :]
            else:
                rows_ref[c, r0:r0 + nr, :] = r[:, c * HEAD_DIM:(c + 1) * HEAD_DIM]
        r0 += nr
    for kv in range(2):
        x = jnp.concatenate(
            [jnp.concatenate([rows_ref[kv * C_KV_HEADS + g, pl.ds(l, n, stride=CMP_HALF), :]
                              for l in range(CMP_HALF)], axis=1) for g in range(C_KV_HEADS)], axis=0)
        for half, out in ((0, a_ref), (1, b_ref)):
            y = jnp.dot((x + pos_ref[kv, half]).astype(BF16), w1_ref[kv, half].astype(BF16),
                        preferred_element_type=F32)
            for g in range(C_KV_HEADS):
                out[kv * C_KV_HEADS + g] = y[g * n:(g + 1) * n]


def _cmp_weights(cmp_pos, cmp_w1):
    n = cmp_pos.shape[0]
    return cmp_pos.reshape(n, 2, 2, 1, CMP_K), cmp_w1.reshape(n, 2, 2, CMP_K, CMP_HID)


def _cmp_partials_prompt(z, layer, cmp_pos, cmp_w1):
    b, t, _ = z.shape
    nch = t // CMP_STRIDE
    pos, w1 = _cmp_weights(cmp_pos, cmp_w1)
    kern = functools.partial(_cmp_partials_kernel, n_src=1, n_prefetch=0, interleaved=False)
    out = jax.ShapeDtypeStruct((b, 4, nch, CMP_HID), F32)
    ospec = pl.BlockSpec((None, 4, nch, CMP_HID), lambda bi: (bi, 0, 0, 0))
    return pl.pallas_call(
        kern,
        grid=(b,),
        in_specs=[pl.BlockSpec((None, t, 2 * C_KV), lambda bi: (bi, 0, ODD_KV0 // (2 * C_KV))),
                  pl.BlockSpec((None,) + pos.shape[1:], lambda bi: (layer, 0, 0, 0, 0)),
                  pl.BlockSpec((None,) + w1.shape[1:], lambda bi: (layer, 0, 0, 0, 0))],
        out_specs=[ospec, ospec],
        out_shape=[out, out],
        scratch_shapes=[pltpu.VMEM((2 * C_KV_HEADS, t, HEAD_DIM), F32)],
        compiler_params=_cparams("parallel"),
        name="cmp_partials_prompt",
    )(z, pos, w1)


def _cmp_partials_paged(cache, layer, page_table, cmp_pos, cmp_w1):
    b, n_pages = page_table.shape
    page = cache.shape[2] // CACHE_ROW_KINDS
    pps = PAGES_PER_STEP
    assert n_pages % pps == 0 and page % CMP_STRIDE == 0
    nch_step = pps * page // CMP_STRIDE
    nch = n_pages * page // CMP_STRIDE
    pos, w1 = _cmp_weights(cmp_pos, cmp_w1)
    kern = functools.partial(_cmp_partials_kernel, n_src=pps, n_prefetch=1, interleaved=True)
    out = jax.ShapeDtypeStruct((b, 4, nch, CMP_HID), F32)
    ospec = pl.BlockSpec((None, 4, nch_step, CMP_HID), lambda bi, pg, pt: (bi, 0, pg, 0))

    def page_spec(j):
        return pl.BlockSpec((None, None, page * CACHE_ROW_KINDS, HEAD_DIM),
                            lambda bi, pg, pt: (layer, pt[bi, pg * pps + j], 0, 0))

    grid_spec = pltpu.PrefetchScalarGridSpec(
        num_scalar_prefetch=1,
        grid=(b, n_pages // pps),
        in_specs=[page_spec(j) for j in range(pps)] + [
            pl.BlockSpec((None,) + pos.shape[1:], lambda bi, pg, pt: (layer, 0, 0, 0, 0)),
            pl.BlockSpec((None,) + w1.shape[1:], lambda bi, pg, pt: (layer, 0, 0, 0, 0))],
        out_specs=[ospec, ospec],
        scratch_shapes=[pltpu.VMEM((2 * C_KV_HEADS, pps * page, HEAD_DIM), F32)])
    return pl.pallas_call(
        kern, grid_spec=grid_spec, out_shape=[out, out],
        compiler_params=_cparams("parallel", "arbitrary"),
        name="cmp_partials_paged",
    )(page_table, *([cache] * pps), pos, w1)


def _overlap_np(n_ch, n_sel, ns_pad):
    ci = np.arange(n_ch)[:, None]
    si = np.arange(ns_pad)[None, :]
    ov = (ci * CMP_STRIDE < (si + 1) * SEL_BLK) & (ci * CMP_STRIDE + CMP_BLK > si * SEL_BLK)
    ov &= (ci < n_ch - 1) & (si < n_sel)
    return ov.astype(np.float32)


def _cmp_select_kernel(q_ref, ak_ref, bk_ref, av_ref, bv_ref, b1_ref, w2_ref, ov_ref, oc_ref, sel_ref,
                       *, qpos0, n_sel):
    qi = pl.program_id(2)
    tq = q_ref.shape[0]
    nch = ak_ref.shape[0]
    ns_pad = ov_ref.shape[1]

    def finish(a_ref, b_ref, kv):
        hid = a_ref[...] + pltpu.roll(b_ref[...], nch - 1, 0) + b1_ref[kv:kv + 1, :]
        return jnp.dot(_silu(hid).astype(BF16), w2_ref[kv].astype(BF16),
                       preferred_element_type=F32).astype(BF16)

    k_cmp = finish(ak_ref, bk_ref, 0)
    v_cmp = finish(av_ref, bv_ref, 1)
    qpos = qpos0 + qi * tq + lax.broadcasted_iota(jnp.int32, (tq, nch), 0)
    cmp_end = lax.broadcasted_iota(jnp.int32, (tq, nch), 1) * CMP_STRIDE + (CMP_BLK - 1)
    ok = (cmp_end <= qpos) & (cmp_end < (nch - 1) * CMP_STRIDE + CMP_BLK - 1)
    psum = jnp.zeros((tq, nch), F32)
    for h in range(C_HPG):
        cs = slice(h * HEAD_DIM, (h + 1) * HEAD_DIM)
        q = (q_ref[:, cs] * SCALE).astype(BF16)
        s = lax.dot_general(q, k_cmp, (((1,), (1,)), ((), ())), preferred_element_type=F32)
        s = jnp.where(ok, s, NEG)
        m = jnp.max(s, axis=-1, keepdims=True)
        e = jnp.where(ok, jnp.exp(s - m), 0.0)
        den = jnp.sum(e, axis=-1, keepdims=True)
        p = e / jnp.where(den > 0.0, den, 1.0)
        psum = psum + p
        oc_ref[:, cs] = jnp.dot(p.astype(BF16), v_cmp, preferred_element_type=F32)
    imp = jnp.dot(psum, ov_ref[...], preferred_element_type=F32, precision=lax.Precision.HIGHEST)
    blk = lax.broadcasted_iota(jnp.int32, (tq, ns_pad), 1)
    qblk = (qpos0 + qi * tq + lax.broadcasted_iota(jnp.int32, (tq, ns_pad), 0)) // SEL_BLK
    forced = (blk == 0) | (blk == qblk) | (blk == qblk - 1)
    allowed = (blk <= qblk) & (blk < n_sel)
    score = jnp.where(allowed, jnp.where(forced, -NEG, imp), -1.0)
    rank = jnp.zeros((tq, ns_pad), jnp.int32)
    for j in range(n_sel):
        cj = score[:, j:j + 1]
        before = (cj > score) | ((cj == score) & (j < blk))
        rank = rank + before.astype(jnp.int32)
    sel_ref[...] = (allowed & (rank < SEL_N)).astype(F32)


def _cmp_select(z, a, bm, layer, cmp_b1, cmp_w2, *, tq, qpos0, n_sel):
    b, t, _ = z.shape
    nch = a.shape[2]
    ns_pad = -(-n_sel // LANES) * LANES
    ov = jnp.asarray(_overlap_np(nch, n_sel, ns_pad))
    kern = functools.partial(_cmp_select_kernel, qpos0=qpos0, n_sel=n_sel)
    part = lambda kv: pl.BlockSpec((None, None, nch, CMP_HID), lambda bi, g, qi: (bi, kv * C_KV_HEADS + g, 0, 0))
    return pl.pallas_call(
        kern,
        grid=(b, C_KV_HEADS, t // tq),
        in_specs=[pl.BlockSpec((None, tq, C_HPG * HEAD_DIM), lambda bi, g, qi: (bi, qi, g)),
                  part(0), part(0), part(1), part(1),
                  pl.BlockSpec((None, 2, CMP_HID), lambda bi, g, qi: (layer, 0, 0)),
                  pl.BlockSpec((None, 2, CMP_HID, HEAD_DIM), lambda bi, g, qi: (layer, 0, 0, 0)),
                  pl.BlockSpec((nch, ns_pad), lambda bi, g, qi: (0, 0))],
        out_specs=[pl.BlockSpec((None, tq, C_HPG * HEAD_DIM), lambda bi, g, qi: (bi, qi, g)),
                   pl.BlockSpec((None, None, tq, ns_pad), lambda bi, g, qi: (bi, g, qi, 0))],
        out_shape=[jax.ShapeDtypeStruct((b, t, C_WIDTH), F32),
                   jax.ShapeDtypeStruct((b, C_KV_HEADS, t, ns_pad), F32)],
        compiler_params=_cparams("parallel", "parallel", "arbitrary"),
        name="cmp_select",
    )(z, a, bm, a, bm, cmp_b1, cmp_w2, ov)


def _sel_prompt_kernel(q_ref, k_ref, v_ref, rev_ref, sel_ref, o_ref,
                       bias_ref, msk_ref, qb_ref, m_ref, l_ref, acc_ref):
    qi = pl.program_id(2)
    tq = q_ref.shape[0]
    nt = bias_ref.shape[1] - 1
    tk = 2 * tq
    ns_pad = sel_ref.shape[1]

    @pl.when(qi == 0)
    def _():
        for h in range(C_HPG):
            bias_ref[h, 0] = jnp.full((tq, tq), NEG, F32)
            for delta in range(nt):
                off = (nt - 1 - delta) * tq
                bias_ref[h, delta + 1] = _toeplitz(rev_ref[h, :, off:off + 2 * tq], tq)

    sel = sel_ref[...].astype(BF16)
    srow = lax.broadcasted_iota(jnp.int32, (ns_pad, tk), 0)
    scol = lax.broadcasted_iota(jnp.int32, (ns_pad, tk), 1) // SEL_BLK
    for t in range(nt // 2):
        expand = (srow == scol + t * (tk // SEL_BLK)).astype(BF16)
        hit = jnp.dot(sel, expand, preferred_element_type=F32)
        msk_ref[t] = jnp.where(hit > 0.5, 0.0, NEG)
    for h in range(C_HPG):
        qb_ref[h] = (q_ref[:, h * HEAD_DIM:(h + 1) * HEAD_DIM] * SCALE).astype(BF16)
        m_ref[h] = jnp.full((tq, 1), NEG, F32)
        l_ref[h] = jnp.zeros((tq, 1), F32)
        acc_ref[h] = jnp.zeros((tq, HEAD_DIM), F32)

    def body(kj, carry):
        rows = pl.ds(pl.multiple_of(kj * tk, tk), tk)
        k = k_ref[rows, :].astype(BF16)
        v = v_ref[rows, :].astype(BF16)
        msk = msk_ref[kj]
        d0 = qi - 2 * kj
        for h in range(C_HPG):
            bias = jnp.concatenate([bias_ref[h, d0 + 1], bias_ref[h, d0]], axis=1)
            s = lax.dot_general(qb_ref[h], k, (((1,), (1,)), ((), ())), preferred_element_type=F32)
            s = s + bias + msk
            m_old = m_ref[h]
            m_new = jnp.maximum(m_old, jnp.max(s, axis=-1, keepdims=True))
            alpha = jnp.exp(m_old - m_new)
            e = jnp.exp(s - m_new)
            m_ref[h] = m_new
            l_ref[h] = alpha * l_ref[h] + jnp.sum(e, axis=-1, keepdims=True)
            acc_ref[h] = alpha * acc_ref[h] + jnp.dot(e.astype(BF16), v, preferred_element_type=F32)
        return carry

    lax.fori_loop(0, (qi + 2) // 2, body, 0)
    for h in range(C_HPG):
        o_ref[:, h * HEAD_DIM:(h + 1) * HEAD_DIM] = acc_ref[h] / l_ref[h]


def _sel_attn_prompt(z, sel, rel_bias, *, tq):
    b, t, _ = z.shape
    nt = t // tq
    assert nt % 2 == 0
    ns_pad = sel.shape[3]
    rev = _c_rev_table(rel_bias, t, tq, t)
    kcol = (ODD_KV0 + 2 * C_KV) // HEAD_DIM
    vcol = (ODD_KV0 + 3 * C_KV) // HEAD_DIM
    return pl.pallas_call(
        _sel_prompt_kernel,
        grid=(b, C_KV_HEADS, nt),
        in_specs=[pl.BlockSpec((None, tq, C_HPG * HEAD_DIM), lambda bi, g, qi: (bi, qi, g)),
                  pl.BlockSpec((None, t, HEAD_DIM), lambda bi, g, qi: (bi, 0, kcol + g)),
                  pl.BlockSpec((None, t, HEAD_DIM), lambda bi, g, qi: (bi, 0, vcol + g)),
                  pl.BlockSpec((C_HPG, 1, t + tq), lambda bi, g, qi: (g, 0, 0)),
                  pl.BlockSpec((None, None, tq, ns_pad), lambda bi, g, qi: (bi, g, qi, 0))],
        out_specs=pl.BlockSpec((None, tq, C_HPG * HEAD_DIM), lambda bi, g, qi: (bi, qi, g)),
        out_shape=jax.ShapeDtypeStruct((b, t, C_WIDTH), F32),
        scratch_shapes=[pltpu.VMEM((C_HPG, nt + 1, tq, tq), F32),
                        pltpu.VMEM((nt // 2, tq, 2 * tq), F32),
                        pltpu.VMEM((C_HPG, tq, HEAD_DIM), BF16),
                        pltpu.VMEM((C_HPG, tq, 1), F32),
                        pltpu.VMEM((C_HPG, tq, 1), F32),
                        pltpu.VMEM((C_HPG, tq, HEAD_DIM), F32)],
        compiler_params=_cparams("parallel", "parallel", "arbitrary"),
        name="sel_attn_prompt",
    )(z, z, z, rev, sel)


def _sel_paged_kernel(pt_ref, *refs, n_src, past):
    del pt_ref
    pages = refs[:n_src]
    q_ref, new_ref, sel_ref, bias_ref, bnew_ref, o_ref, m_ref, l_ref, acc_ref = refs[n_src:]
    pg = pl.program_id(1)
    tq = q_ref.shape[0]
    rows = C_HPG * tq
    page = pages[0].shape[0] // CACHE_ROW_KINDS
    keys = n_src * page
    ns_pad = sel_ref.shape[2]

    def page_rows(r, kind):
        return r[pl.ds(kind, page, stride=CACHE_ROW_KINDS), :]

    def q_rows(g):
        return jnp.concatenate(
            [(q_ref[:, (g * C_HPG + h) * HEAD_DIM:(g * C_HPG + h + 1) * HEAD_DIM] * SCALE).astype(BF16)
             for h in range(C_HPG)], axis=0)

    @pl.when(pg == 0)
    def _():
        for g in range(C_KV_HEADS):
            k = new_ref[:, g * HEAD_DIM:(g + 1) * HEAD_DIM].astype(BF16)
            v = new_ref[:, C_KV + g * HEAD_DIM:C_KV + (g + 1) * HEAD_DIM].astype(BF16)
            s = lax.dot_general(q_rows(g), k, (((1,), (1,)), ((), ())), preferred_element_type=F32)
            s = s + bnew_ref[g * C_HPG:(g + 1) * C_HPG].reshape(rows, tq)
            m = jnp.max(s, axis=-1, keepdims=True)
            e = jnp.exp(s - m)
            m_ref[g] = m
            l_ref[g] = jnp.sum(e, axis=-1, keepdims=True)
            acc_ref[g] = jnp.dot(e.astype(BF16), v, preferred_element_type=F32)

    srow = lax.broadcasted_iota(jnp.int32, (ns_pad, keys), 0)
    scol = lax.broadcasted_iota(jnp.int32, (ns_pad, keys), 1) // SEL_BLK + pg * (keys // SEL_BLK)
    expand = (srow == scol).astype(BF16)
    for g in range(C_KV_HEADS):
        k = jnp.concatenate([page_rows(r, 2 * C_KV_HEADS + g) for r in pages], axis=0).astype(BF16)
        v = jnp.concatenate([page_rows(r, 3 * C_KV_HEADS + g) for r in pages], axis=0).astype(BF16)
        hit = jnp.dot(sel_ref[g].astype(BF16), expand, preferred_element_type=F32)
        mask = jnp.where(hit > 0.5, 0.0, NEG)
        s = lax.dot_general(q_rows(g), k, (((1,), (1,)), ((), ())), preferred_element_type=F32)
        s = (s.reshape(C_HPG, tq, keys) + bias_ref[g * C_HPG:(g + 1) * C_HPG] + mask[None]).reshape(rows, keys)
        m_old = m_ref[g]
        m_new = jnp.maximum(m_old, jnp.max(s, axis=-1, keepdims=True))
        alpha = jnp.exp(m_old - m_new)
        e = jnp.exp(s - m_new)
        m_ref[g] = m_new
        l_ref[g] = alpha * l_ref[g] + jnp.sum(e, axis=-1, keepdims=True)
        acc_ref[g] = alpha * acc_ref[g] + jnp.dot(e.astype(BF16), v, preferred_element_type=F32)

    @pl.when(pg == pl.num_programs(1) - 1)
    def _():
        for g in range(C_KV_HEADS):
            o = acc_ref[g] / l_ref[g]
            for h in range(C_HPG):
                o_ref[:, (g * C_HPG + h) * HEAD_DIM:(g * C_HPG + h + 1) * HEAD_DIM] = o[h * tq:(h + 1) * tq]


def _sel_attn_paged(z, sel, cache, layer, page_table, rel_bias):
    b, tq, _ = z.shape
    n_pages = page_table.shape[1]
    page = cache.shape[2] // CACHE_ROW_KINDS
    past = n_pages * page
    pps = PAGES_PER_STEP
    keys = pps * page
    ns_pad = sel.shape[3]
    tab = _c_tab(rel_bias)
    rev = tab[_t5_bucket_np(past + tq - 1 - np.arange(past + tq - 1))].T.astype(F32)
    bias = jnp.stack([rev[:, tq - 1 - t:tq - 1 - t + past] for t in range(tq)], axis=1)
    dnew = np.arange(tq)[:, None] - np.arange(tq)[None, :]
    bnew = jnp.where(jnp.asarray(dnew >= 0)[None],
                     jnp.transpose(tab[_t5_bucket_np(np.clip(dnew, 0, None))], (2, 0, 1)), NEG).astype(F32)
    kern = functools.partial(_sel_paged_kernel, n_src=pps, past=past)

    def page_spec(j):
        return pl.BlockSpec((None, None, page * CACHE_ROW_KINDS, HEAD_DIM),
                            lambda bi, pg, pt: (layer, pt[bi, pg * pps + j], 0, 0))

    rows = C_HPG * tq
    grid_spec = pltpu.PrefetchScalarGridSpec(
        num_scalar_prefetch=1,
        grid=(b, n_pages // pps),
        in_specs=[page_spec(j) for j in range(pps)] + [
            pl.BlockSpec((None, tq, C_WIDTH), lambda bi, pg, pt: (bi, 0, 0)),
            pl.BlockSpec((None, tq, 2 * C_KV), lambda bi, pg, pt: (bi, 0, (ODD_KV0 + 2 * C_KV) // (2 * C_KV))),
            pl.BlockSpec((None, C_KV_HEADS, tq, ns_pad), lambda bi, pg, pt: (bi, 0, 0, 0)),
            pl.BlockSpec((C_HEADS, tq, keys), lambda bi, pg, pt: (0, 0, pg)),
            pl.BlockSpec((C_HEADS, tq, tq), lambda bi, pg, pt: (0, 0, 0))],
        out_specs=pl.BlockSpec((None, tq, C_WIDTH), lambda bi, pg, pt: (bi, 0, 0)),
        scratch_shapes=[pltpu.VMEM((C_KV_HEADS, rows, 1), F32), pltpu.VMEM((C_KV_HEADS, rows, 1), F32),
                        pltpu.VMEM((C_KV_HEADS, rows, HEAD_DIM), F32)])
    return pl.pallas_call(
        kern, grid_spec=grid_spec, out_shape=jax.ShapeDtypeStruct((b, tq, C_WIDTH), F32),
        compiler_params=_cparams("parallel", "arbitrary"),
        name="sel_attn_paged",
    )(page_table, *([cache] * pps), z, z, sel, bias, bnew)


def _win_attn_prompt(z, rel_bias, *, tq):
    b, t, _ = z.shape
    n_tiles = WIN // tq + 1
    rev = _c_rev_table(rel_bias, n_tiles * tq, tq, WIN)
    kv_blk = (ODD_KV0 + 4 * C_KV) // (2 * C_KV)
    kern = functools.partial(_tile_attn_kernel, nh=C_HEADS, rep=C_HPG, n_tiles=n_tiles, k_col=0, v_col=C_KV,
                             shared_kv=True, with_lse=False, lead_axis=1, toeplitz=True)

    def kv_spec(p):
        return pl.BlockSpec((None, tq, 2 * C_KV),
                            lambda bi, qi: (bi, jnp.maximum(qi - (n_tiles - 1 - p), 0), kv_blk))

    return pl.pallas_call(
        kern,
        grid=(b, t // tq),
        in_specs=[pl.BlockSpec((None, tq, C_WIDTH), lambda bi, qi: (bi, qi, 0))]
                 + [kv_spec(p) for p in range(n_tiles)]
                 + [pl.BlockSpec((C_HEADS, 1, (n_tiles + 1) * tq), lambda bi, qi: (0, 0, 0))],
        out_specs=pl.BlockSpec((None, tq, C_WIDTH), lambda bi, qi: (bi, qi, 0)),
        out_shape=jax.ShapeDtypeStruct((b, t, C_WIDTH), F32),
        scratch_shapes=[pltpu.VMEM((C_HEADS, tq, n_tiles * tq), F32)],
        compiler_params=_cparams("parallel", "arbitrary"),
        name="win_attn_prompt",
    )(z, *([z] * n_tiles), rev)


def _win_attn_sample(z, win_full, rel_bias):
    b, tq, _ = z.shape
    lk = win_full.shape[1]
    dist = (lk - tq) + np.arange(tq)[:, None] - np.arange(lk)[None, :]
    valid = (dist >= 0) & (dist <= WIN)
    bias = jnp.transpose(_c_tab(rel_bias)[_t5_bucket_np(np.clip(dist, 0, None))], (2, 0, 1))
    bias = jnp.where(jnp.asarray(valid)[None], bias, NEG).astype(F32)
    kern = functools.partial(_tile_attn_kernel, nh=C_HEADS, rep=C_HPG, n_tiles=1, k_col=0, v_col=C_KV,
                             shared_kv=True, with_lse=False, lead_axis=0)
    return pl.pallas_call(
        kern,
        grid=(b,),
        in_specs=[pl.BlockSpec((None, tq, C_WIDTH), lambda bi: (bi, 0, 0)),
                  pl.BlockSpec((None, lk, 2 * C_KV), lambda bi: (bi, 0, 0)),
                  pl.BlockSpec((C_HEADS, tq, lk), lambda bi: (0, 0, 0))],
        out_specs=pl.BlockSpec((None, tq, C_WIDTH), lambda bi: (bi, 0, 0)),
        out_shape=jax.ShapeDtypeStruct((b, tq, C_WIDTH), F32),
        compiler_params=_cparams("parallel"),
        name="win_attn_sample",
    )(z, win_full, bias)


def _odd_mix_kernel(oc_ref, os_ref, ow_ref, gl_ref, gp_ref, mix_ref):
    gates = jax.nn.sigmoid(gl_ref[...])
    branches = (oc_ref, os_ref, ow_ref)
    for h in range(C_HEADS):
        cs = slice(h * HEAD_DIM, (h + 1) * HEAD_DIM)
        acc = gates[:, h:h + 1] * branches[0][:, cs]
        for br in range(1, 3):
            lane = br * C_HEADS + h
            acc = acc + gates[:, lane:lane + 1] * branches[br][:, cs]
        mix_ref[:, cs] = (acc * _silu(gp_ref[:, cs])).astype(BF16)


def _odd_mix(oc, osel, ow, z2, *, tm):
    m = z2.shape[0]
    wide = lambda c: pl.BlockSpec((tm, C_WIDTH), lambda i: (i, c))
    return pl.pallas_call(
        _odd_mix_kernel,
        grid=(m // tm,),
        in_specs=[wide(0), wide(0), wide(0),
                  pl.BlockSpec((tm, LANES), lambda i: (i, ODD_GL0 // LANES)),
                  wide(ODD_GP0 // C_WIDTH)],
        out_specs=wide(0),
        out_shape=jax.ShapeDtypeStruct((m, C_WIDTH), BF16),
        compiler_params=_cparams("parallel"),
        name="odd_mix",
    )(oc, osel, ow, z2, z2)


def _odd_layer(x, past, win_hist, rel_bias, layer, norm_g, w_in_packed, cmp_pos, cmp_w1, cmp_b1, cmp_w2, w_out,
               *, tm, tn_in, tn_out, tq):
    b, t, d = x.shape
    m = b * t
    x2 = x.reshape(m, d)
    z2 = _norm_matmul(x2, norm_g[layer], w_in_packed, layer, tm=tm, tn=tn_in)
    z = z2.reshape(b, t, ODD_Z)
    rows = z[:, :, ODD_KV0:ODD_KV0 + 4 * C_KV].reshape(b, t, 4, C_KV_HEADS, HEAD_DIM)
    win_new = z[:, :, ODD_KV0 + 4 * C_KV:ODD_KV0 + 6 * C_KV]
    if past is None:
        length = t
        a, bm = _cmp_partials_prompt(z, layer, cmp_pos, cmp_w1)
        qpos0 = 0
    else:
        cache, page_table = past
        p_len = page_table.shape[1] * (cache.shape[2] // CACHE_ROW_KINDS)
        length = p_len + t
        assert length // CMP_STRIDE == p_len // CMP_STRIDE and p_len % SEL_BLK == 0
        a, bm = _cmp_partials_paged(cache, layer, page_table, cmp_pos, cmp_w1)
        qpos0 = p_len
    n_sel = -(-length // SEL_BLK)
    oc, sel = _cmp_select(z, a, bm, layer, cmp_b1, cmp_w2, tq=tq, qpos0=qpos0, n_sel=n_sel)
    if past is None:
        osel = _sel_attn_prompt(z, sel, rel_bias, tq=tq)
        ow = _win_attn_prompt(z, rel_bias, tq=tq)
        new_win = win_new[:, t - min(WIN, t):].reshape(b, min(WIN, t), 2, C_KV_HEADS, HEAD_DIM)
    else:
        osel = _sel_attn_paged(z, sel, cache, layer, page_table, rel_bias)
        win_full = jnp.concatenate([win_hist.reshape(b, -1, 2 * C_KV), win_new], axis=1)
        ow = _win_attn_sample(z, win_full, rel_bias)
        lk = win_full.shape[1]
        keep = min(WIN, length)
        new_win = win_full[:, lk - keep:].reshape(b, keep, 2, C_KV_HEADS, HEAD_DIM)
    mix = _odd_mix(oc.reshape(m, C_WIDTH), osel.reshape(m, C_WIDTH), ow.reshape(m, C_WIDTH), z2, tm=min(tm, 256))
    y = _matmul_res([mix], w_out, layer, x2, tm=tm, tn=tn_out)
    return y.reshape(b, t, d), rows, new_win


def _run_trunk(x, a_caches, conv_state, c_cache, c_win, page_table, rel_bias, norm_even, w_in_even, conv_w,
               conv_b, conv_ln_g, conv_ln_b, conv_pw_w, conv_pw_b, w_out_even, norm_odd, w_in_odd_packed,
               cmp_pos, cmp_w1, cmp_b1, cmp_w2, w_out_odd, final_norm, *, tm, tt, tq):
    b, t, d = x.shape
    prompt = a_caches is None
    depth = norm_even.shape[0] + norm_odd.shape[0]
    new_a = [[] for _ in A_GROUPS]
    new_conv, new_rows, new_win = [], [], []
    for depth_i in range(depth):
        i = depth_i // 2
        if depth_i % 2 == 0:
            hists = None if prompt else [c[i] for c in a_caches]
            conv_hist = jnp.zeros((b, CONV_W - 1, B_WIDTH), F32) if prompt else conv_state[i]
            x, hists, conv = _even_layer(x, hists, conv_hist, rel_bias, i, norm_even, w_in_even, conv_w,
                                         conv_b, conv_ln_g, conv_ln_b, conv_pw_w, conv_pw_b,
                                         w_out_even, tm=tm, tn_in=512, tn_out=512, tt=tt)
            for g in range(A_NG):
                new_a[g].append(hists[g])
            new_conv.append(conv)
        else:
            past = None if prompt else (c_cache, page_table)
            x, rows, win = _odd_layer(x, past, None if prompt else c_win[i], rel_bias, i, norm_odd,
                                      w_in_odd_packed, cmp_pos, cmp_w1, cmp_b1, cmp_w2,
                                      w_out_odd, tm=tm, tn_in=512, tn_out=512, tq=tq)
            new_rows.append(rows)
            new_win.append(win)
    y = _rms_norm(x.reshape(b * t, d), final_norm, tm=min(tm, 256)).reshape(b, t, d)
    return (y, [jnp.stack(a) for a in new_a], jnp.stack(new_conv), jnp.stack(new_rows), jnp.stack(new_win))


def kernel(x_prompt, x_sample, cache_a_kv0, cache_a_kv1, cache_a_kv2, state_b_conv, cache_c_kv, cache_c_win, page_table, rel_bias, norm_even, w_in_even, conv_w, conv_b, conv_ln_g, conv_ln_b, conv_pw_w, conv_pw_b, w_out_even, norm_odd, w_in_odd, cmp_pos, cmp_w1, cmp_b1, cmp_w2, w_out_odd, final_norm):
    w_in_odd_packed = _repack_w_in_odd(w_in_odd)
    weights = (rel_bias, norm_even, w_in_even, conv_w, conv_b, conv_ln_g, conv_ln_b, conv_pw_w, conv_pw_b,
               w_out_even, norm_odd, w_in_odd_packed, cmp_pos, cmp_w1, cmp_b1, cmp_w2, w_out_odd, final_norm)
    y_p, a_p, conv_p, rows_p, win_p = _run_trunk(x_prompt, None, None, None, None, None, *weights,
                                                 tm=1024, tt=256, tq=128)
    n_odd, n_pool, page = cache_c_kv.shape[:3]
    c_cache = cache_c_kv.reshape(n_odd, n_pool, page * CACHE_ROW_KINDS, HEAD_DIM)
    db, dt = x_sample.shape[:2]
    y_s, a_s, conv_s, rows_s, win_s = _run_trunk(x_sample, (cache_a_kv0, cache_a_kv1, cache_a_kv2), state_b_conv,
                                                 c_cache, cache_c_win, page_table, *weights,
                                                 tm=db * dt, tt=dt, tq=dt)
    return (y_p, y_s, a_p[0], a_p[1], a_p[2], conv_p, rows_p, win_p, a_s[0], a_s[1], a_s[2], conv_s, rows_s, win_s)
```

```python
import functools
import math

import numpy as np
import jax
import jax.numpy as jnp
from jax import lax
from jax.experimental import pallas as pl
from jax.experimental.pallas import tpu as pltpu

F32 = jnp.float32
BF16 = jnp.bfloat16

D_MODEL = 2048
HEAD_DIM = 128
LANES = 128
SUBLANES = 8
A_GROUPS = ((128, 1), (512, 4), (2048, 16))
A_NG = 3
A_HPG = 8
A_QKV = A_NG * A_HPG * HEAD_DIM
A_WIDTH = A_HPG * HEAD_DIM
B_WIDTH = 1024
CONV_W = 31
CONV_HALO = 32
EVEN_IN = 3 * A_QKV + A_WIDTH + 3 * B_WIDTH
C_HEADS = 16
C_KV_HEADS = 2
C_HPG = C_HEADS // C_KV_HEADS
C_WIDTH = C_HEADS * HEAD_DIM
C_KV = C_KV_HEADS * HEAD_DIM
CMP_BLK = 32
CMP_STRIDE = 16
CMP_HID = 128
SEL_BLK = 64
SEL_N = 16
WIN = 512
NUM_BUCKETS = 32
MAX_DIST = 2048
EPS = 1e-6
NEG = -1e30
SCALE = HEAD_DIM ** -0.5
VMEM_LIMIT = 56 * 1024 * 1024


def _cparams(*sem):
    return pltpu.CompilerParams(dimension_semantics=sem, vmem_limit_bytes=VMEM_LIMIT)


def _t5_bucket_np(dist):
    max_exact = NUM_BUCKETS // 2
    d = np.maximum(np.asarray(dist, np.int64), 0)
    ratio = np.log(np.maximum(d, 1).astype(np.float64) / max_exact) / math.log(MAX_DIST / max_exact)
    large = np.minimum(max_exact + (ratio * (NUM_BUCKETS - max_exact)).astype(np.int64), NUM_BUCKETS - 1)
    return np.where(d < max_exact, d, large).astype(np.int32)


def _silu(x):
    return x * jax.nn.sigmoid(x)


def _norm_matmul_kernel(x_ref, g_ref, w_ref, o_ref, xn_ref):
    @pl.when(pl.program_id(1) == 0)
    def _():
        x = x_ref[...]
        ms = jnp.mean(x * x, axis=-1, keepdims=True)
        xn_ref[...] = (x * lax.rsqrt(ms + EPS) * g_ref[...]).astype(BF16)

    o_ref[...] = jnp.dot(xn_ref[...], w_ref[...].astype(BF16), preferred_element_type=F32)


def _norm_matmul(x, g, w, layer, *, tm, tn):
    m, k = x.shape
    n = w.shape[2]
    return pl.pallas_call(
        _norm_matmul_kernel,
        grid=(m // tm, n // tn),
        in_specs=[pl.BlockSpec((tm, k), lambda i, j: (i, 0)),
                  pl.BlockSpec((1, k), lambda i, j: (0, 0)),
                  pl.BlockSpec((None, k, tn), lambda i, j: (layer, 0, j))],
        out_specs=pl.BlockSpec((tm, tn), lambda i, j: (i, j)),
        out_shape=jax.ShapeDtypeStruct((m, n), F32),
        scratch_shapes=[pltpu.VMEM((tm, k), BF16)],
        compiler_params=_cparams("parallel", "arbitrary"),
        name="norm_matmul",
    )(x, g.reshape(1, k), w)


def _matmul_res_kernel(*refs, n_parts):
    a_refs = refs[:n_parts]
    w_ref, r_ref, o_ref = refs[n_parts:]
    acc = r_ref[...]
    k0 = 0
    for a_ref in a_refs:
        kp = a_ref.shape[1]
        acc = acc + jnp.dot(a_ref[...], w_ref[k0:k0 + kp, :].astype(BF16), preferred_element_type=F32)
        k0 += kp
    o_ref[...] = acc


def _matmul_res(parts, w, layer, res, *, tm, tn):
    m = res.shape[0]
    k, n = w.shape[1:]
    assert sum(p.shape[1] for p in parts) == k
    return pl.pallas_call(
        functools.partial(_matmul_res_kernel, n_parts=len(parts)),
        grid=(m // tm, n // tn),
        in_specs=[pl.BlockSpec((tm, p.shape[1]), lambda i, j: (i, 0)) for p in parts]
                 + [pl.BlockSpec((None, k, tn), lambda i, j: (layer, 0, j)),
                    pl.BlockSpec((tm, tn), lambda i, j: (i, j))],
        out_specs=pl.BlockSpec((tm, tn), lambda i, j: (i, j)),
        out_shape=jax.ShapeDtypeStruct((m, n), F32),
        compiler_params=_cparams("parallel", "arbitrary"),
        name="matmul_res",
    )(*parts, w, res)


def _rms_kernel(x_ref, g_ref, o_ref):
    x = x_ref[...]
    ms = jnp.mean(x * x, axis=-1, keepdims=True)
    o_ref[...] = x * lax.rsqrt(ms + EPS) * g_ref[...]


def _rms_norm(x, g, *, tm):
    m, k = x.shape
    return pl.pallas_call(
        _rms_kernel,
        grid=(m // tm,),
        in_specs=[pl.BlockSpec((tm, k), lambda i: (i, 0)), pl.BlockSpec((1, k), lambda i: (0, 0))],
        out_specs=pl.BlockSpec((tm, k), lambda i: (i, 0)),
        out_shape=jax.ShapeDtypeStruct((m, k), F32),
        compiler_params=_cparams("parallel"),
        name="rms_norm",
    )(x, g.reshape(1, k))


def _c_tab(rel_bias):
    return rel_bias[:, :C_HEADS]


def _c_rev_table(rel_bias, length, tq, max_dist):
    dist = length - np.arange(length + tq)
    valid = (dist >= 0) & (dist <= max_dist)
    tab = _c_tab(rel_bias)[_t5_bucket_np(np.clip(dist, 0, None))]
    tab = jnp.where(jnp.asarray(valid)[:, None], tab, NEG).astype(F32)
    return tab.T.reshape(C_HEADS, 1, length + tq)


def _toeplitz(vec, tq):
    w = vec.shape[1]
    return pltpu.roll(jnp.broadcast_to(vec, (tq, w)), 0, 1, stride=1, stride_axis=0)[:, tq:]


def _tile_attn_kernel(*refs, nh, rep, n_tiles, k_col, v_col, shared_kv, with_lse, lead_axis, toeplitz=False):
    if toeplitz:
        rev_ref, bias_scr = refs[-1 - int(with_lse) - 2], refs[-1]
        tq_ = refs[0].shape[0]

        @pl.when(pl.program_id(lead_axis) == 0)
        def _():
            for h in range(nh):
                bias_scr[h] = _toeplitz(rev_ref[h], tq_)

        refs = list(refs[:-1])
        refs[-1 - int(with_lse) - 1] = bias_scr
    q_ref = refs[0]
    if shared_kv:
        k_refs = v_refs = refs[1:1 + n_tiles]
        nxt = 1 + n_tiles
    else:
        k_refs = refs[1:1 + n_tiles]
        v_refs = refs[1 + n_tiles:1 + 2 * n_tiles]
        nxt = 1 + 2 * n_tiles
    b_ref = refs[nxt]
    o_ref = refs[nxt + 1]
    lse_ref = refs[nxt + 2] if with_lse else None
    tq = q_ref.shape[0]
    tk = k_refs[0].shape[0]
    lk = n_tiles * tk
    if n_tiles > 1:
        qi = pl.program_id(lead_axis)
        col = lax.broadcasted_iota(jnp.int32, (tq, lk), 1)
        pad_mask = jnp.where(col < (n_tiles - 1 - qi) * tk, NEG, 0.0).astype(F32)
    else:
        pad_mask = None
    if with_lse:
        lane = lax.broadcasted_iota(jnp.int32, (tq, LANES), 1)
        lse_t = jnp.zeros((tq, LANES), F32)
    for j in range(nh // rep):
        kc = k_col + j * HEAD_DIM
        vc = v_col + j * HEAD_DIM
        heads = range(j * rep, (j + 1) * rep)
        q = jnp.concatenate([(q_ref[:, h * HEAD_DIM:(h + 1) * HEAD_DIM] * SCALE).astype(BF16) for h in heads], axis=0)
        if n_tiles > 1:
            k = jnp.concatenate([r[:, kc:kc + HEAD_DIM] for r in k_refs], axis=0).astype(BF16)
            v = jnp.concatenate([r[:, vc:vc + HEAD_DIM] for r in v_refs], axis=0).astype(BF16)
        else:
            k = k_refs[0][:, kc:kc + HEAD_DIM].astype(BF16)
            v = v_refs[0][:, vc:vc + HEAD_DIM].astype(BF16)
        s = lax.dot_general(q, k, (((1,), (1,)), ((), ())), preferred_element_type=F32)
        s = s.reshape(rep, tq, lk) + b_ref[j * rep:(j + 1) * rep]
        if pad_mask is not None:
            s = s + pad_mask[None]
        s = s.reshape(rep * tq, lk)
        m = jnp.max(s, axis=-1, keepdims=True)
        e = jnp.exp(s - m)
        den = jnp.sum(e, axis=-1, keepdims=True)
        o = jnp.dot(e.astype(BF16), v, preferred_element_type=F32) / den
        for r, h in enumerate(heads):
            o_ref[:, h * HEAD_DIM:(h + 1) * HEAD_DIM] = o[r * tq:(r + 1) * tq]
        if with_lse:
            lse = m + jnp.log(den)
            for r, h in enumerate(heads):
                lse_t = jnp.where(lane == h, lse[r * tq:(r + 1) * tq], lse_t)
    if with_lse:
        lse_ref[...] = lse_t


A_TU = 128


def _a_rev_table(rel_bias):
    j = 2 * A_TU - np.arange(3 * A_TU)
    rows = []
    for g, (win, dil) in enumerate(A_GROUPS):
        valid = (j >= 0) & (j <= win // dil)
        tab = rel_bias[:, g * A_HPG:(g + 1) * A_HPG][_t5_bucket_np(np.clip(j, 0, None) * dil)]
        rows.append(jnp.where(jnp.asarray(valid)[:, None], tab, NEG).T)
    return jnp.stack(rows, axis=1).astype(F32)


def _a_prompt_kernel(q0_ref, q1_ref, q2_ref, k0_ref, k1_ref, k2_ref, v0_ref, v1_ref, v2_ref, ga_ref, rev_ref,
                     mix_ref, o_scr, l_scr, bias_scr):
    t = q0_ref.shape[0]
    tu = A_TU
    q_refs, k_refs, v_refs = (q0_ref, q1_ref, q2_ref), (k0_ref, k1_ref, k2_ref), (v0_ref, v1_ref, v2_ref)
    for g in range(A_NG):
        bias_scr[g] = _toeplitz(rev_ref[g:g + 1, :], tu)
    col = lax.broadcasted_iota(jnp.int32, (tu, 2 * tu), 1)
    for g, (win, dil) in enumerate(A_GROUPS):
        n_u = t // (dil * tu)
        q_ref, k_ref, v_ref = q_refs[g], k_refs[g], v_refs[g]

        def rows(r, u, dil=dil):
            start = r + u * (tu * dil)
            if dil == 1:
                return pl.ds(pl.multiple_of(start, tu), tu)
            return pl.ds(start, tu, stride=dil)

        def body(it, carry, g=g, n_u=n_u, q_ref=q_ref, k_ref=k_ref, v_ref=v_ref, rows=rows):
            r = it // n_u
            u = it % n_u
            cur = rows(r, u)
            q = (q_ref[cur, :] * SCALE).astype(BF16)
            if n_u > 1:
                prev = rows(r, jnp.maximum(u - 1, 0))
                k = jnp.concatenate([k_ref[prev, :], k_ref[cur, :]], axis=0).astype(BF16)
                v = jnp.concatenate([v_ref[prev, :], v_ref[cur, :]], axis=0).astype(BF16)
                bias = bias_scr[g] + jnp.where((col < tu) & (u == 0), NEG, 0.0)
            else:
                k = k_ref[cur, :].astype(BF16)
                v = v_ref[cur, :].astype(BF16)
                bias = bias_scr[g][:, tu:]
            s = lax.dot_general(q, k, (((1,), (1,)), ((), ())), preferred_element_type=F32) + bias
            m = jnp.max(s, axis=-1, keepdims=True)
            e = jnp.exp(s - m)
            den = jnp.sum(e, axis=-1, keepdims=True)
            o_scr[g, cur, :] = jnp.dot(e.astype(BF16), v, preferred_element_type=F32) / den
            l_scr[g, cur, :] = jnp.broadcast_to(m + jnp.log(den), (tu, HEAD_DIM))
            return carry

        lax.fori_loop(0, t // tu, body, 0, unroll=4)

    def combine(c, carry):
        cur = pl.ds(pl.multiple_of(c * tu, tu), tu)
        ls = [l_scr[g, cur, :] for g in range(A_NG)]
        m = jnp.maximum(jnp.maximum(ls[0], ls[1]), ls[2])
        es = [jnp.exp(l - m) for l in ls]
        acc = es[0] * o_scr[0, cur, :] + es[1] * o_scr[1, cur, :] + es[2] * o_scr[2, cur, :]
        out = acc / (es[0] + es[1] + es[2])
        mix_ref[cur, :] = (out * _silu(ga_ref[cur, :])).astype(BF16)
        return carry

    lax.fori_loop(0, t // tu, combine, 0)


def _a_attn_prompt(z, rel_bias):
    b, t, n = z.shape
    for win, dil in A_GROUPS:
        assert win // dil == A_TU and t % (dil * A_TU) == 0
    rev = _a_rev_table(rel_bias)
    nq = A_QKV // HEAD_DIM
    col = lambda base: pl.BlockSpec((None, t, HEAD_DIM), lambda bi, h: (bi, 0, base + h))
    return pl.pallas_call(
        _a_prompt_kernel,
        grid=(b, A_HPG),
        in_specs=[col(g * A_HPG) for g in range(A_NG)]
                 + [col(nq + g * A_HPG) for g in range(A_NG)]
                 + [col(2 * nq + g * A_HPG) for g in range(A_NG)]
                 + [col(3 * nq), pl.BlockSpec((None, A_NG, 3 * A_TU), lambda bi, h: (h, 0, 0))],
        out_specs=pl.BlockSpec((None, t, HEAD_DIM), lambda bi, h: (bi, 0, h)),
        out_shape=jax.ShapeDtypeStruct((b, t, A_WIDTH), BF16),
        scratch_shapes=[pltpu.VMEM((A_NG, t, HEAD_DIM), F32), pltpu.VMEM((A_NG, t, HEAD_DIM), F32),
                        pltpu.VMEM((A_NG, A_TU, 2 * A_TU), F32)],
        compiler_params=_cparams("parallel", "parallel"),
        name="a_attn_prompt",
    )(*([z] * 10), rev)


def _a_bias_sample(rel_bias, g, hist_len, tq):
    win, dil = A_GROUPS[g]
    lk = hist_len + tq
    qidx = hist_len + np.arange(tq)[:, None]
    kidx = np.arange(lk)[None, :]
    dist = qidx - kidx
    valid = (dist >= 0) & (dist <= win) & (dist % dil == 0)
    bucket = _t5_bucket_np(np.clip(dist, 0, None))
    tab = rel_bias[:, g * A_HPG:(g + 1) * A_HPG]
    bias = jnp.transpose(tab[bucket], (2, 0, 1))
    return jnp.where(jnp.asarray(valid)[None], bias, NEG).astype(F32)


A_ROWS = 2 * A_HPG
A_CHUNK = 512


def _a_sample_kernel(*refs, aliased):
    q_ref, kn_ref, vn_ref, old_ref, head_ref, bias_ref, bnew_ref = refs[:7]
    o_ref, lse_ref, new_ref, m_ref, l_ref, acc_ref = refs[7 + int(aliased):]
    c = pl.program_id(1)
    last = pl.num_programs(1) - 1
    tq = q_ref.shape[0]
    n_tok = old_ref.shape[0] // A_ROWS
    shift = tq * A_ROWS

    @pl.when(c == 0)
    def _():
        m_ref[...] = jnp.full(m_ref.shape, NEG, F32)
        l_ref[...] = jnp.zeros(l_ref.shape, F32)
        acc_ref[...] = jnp.zeros(acc_ref.shape, F32)

    def update(h, q, k, v, bias):
        s = lax.dot_general(q, k, (((1,), (1,)), ((), ())), preferred_element_type=F32) + bias
        m_old = m_ref[h]
        m_new = jnp.maximum(m_old, jnp.max(s, axis=-1, keepdims=True))
        alpha = jnp.exp(m_old - m_new)
        e = jnp.exp(s - m_new)
        m_ref[h] = m_new
        l_ref[h] = alpha * l_ref[h] + jnp.sum(e, axis=-1, keepdims=True)
        acc_ref[h] = alpha * acc_ref[h] + jnp.dot(e.astype(BF16), v, preferred_element_type=F32)

    qs = [(q_ref[:, h * HEAD_DIM:(h + 1) * HEAD_DIM] * SCALE).astype(BF16) for h in range(A_HPG)]
    for h in range(A_HPG):
        k = old_ref[pl.ds(h, n_tok, stride=A_ROWS), :].astype(BF16)
        v = old_ref[pl.ds(A_HPG + h, n_tok, stride=A_ROWS), :].astype(BF16)
        update(h, qs[h], k, v, bias_ref[h])

    new_ref[0:n_tok * A_ROWS - shift, :] = old_ref[shift:, :]

    @pl.when(c < last)
    def _():
        new_ref[n_tok * A_ROWS - shift:, :] = head_ref[...]

    @pl.when(c == last)
    def _():
        lane = lax.broadcasted_iota(jnp.int32, (tq, LANES), 1)
        lse_t = jnp.zeros((tq, LANES), F32)
        for h in range(A_HPG):
            cs = slice(h * HEAD_DIM, (h + 1) * HEAD_DIM)
            update(h, qs[h], kn_ref[:, cs].astype(BF16), vn_ref[:, cs].astype(BF16), bnew_ref[h])
            o_ref[:, cs] = acc_ref[h] / l_ref[h]
            lse_t = jnp.where(lane == h, m_ref[h] + jnp.log(l_ref[h]), lse_t)
            base = n_tok * A_ROWS - shift
            new_ref[pl.ds(base + h, tq, stride=A_ROWS), :] = kn_ref[:, cs]
            new_ref[pl.ds(base + A_HPG + h, tq, stride=A_ROWS), :] = vn_ref[:, cs]
        lse_ref[...] = lse_t


def _a_attn_sample(z, cache, layer, new_cache, rel_bias, g):
    b, tq, n = z.shape
    n_layers, _, w = cache.shape[:3]
    assert cache.shape[3:] == (2, A_HPG, HEAD_DIM) and w % LANES == 0 and tq * A_ROWS == LANES
    ch = min(w, A_CHUNK)
    n_ch = w // ch
    old = cache.reshape(n_layers, b, w * A_ROWS, HEAD_DIM)
    bias_full = _a_bias_sample(rel_bias, g, w, tq)
    bias = jnp.stack([bias_full[:, :, i * ch:(i + 1) * ch] for i in range(n_ch)])
    bnew = bias_full[:, :, w:]
    nq = A_QKV // A_WIDTH
    zcol = lambda cb: pl.BlockSpec((None, tq, A_WIDTH), lambda bi, c: (bi, 0, cb))
    heads_per_chunk = ch * A_ROWS // LANES
    in_specs = [zcol(g), zcol(nq + g), zcol(2 * nq + g),
                pl.BlockSpec((None, None, ch * A_ROWS, HEAD_DIM), lambda bi, c: (layer, bi, c, 0)),
                pl.BlockSpec((None, None, LANES, HEAD_DIM),
                             lambda bi, c: (layer, bi, jnp.minimum(c + 1, n_ch - 1) * heads_per_chunk, 0)),
                pl.BlockSpec((None, A_HPG, tq, ch), lambda bi, c: (c, 0, 0, 0)),
                pl.BlockSpec((A_HPG, tq, tq), lambda bi, c: (0, 0, 0))]
    args = [z, z, z, old, old, bias, bnew]
    aliases = {}
    if new_cache is not None:
        in_specs.append(pl.BlockSpec(memory_space=pl.ANY))
        args.append(new_cache)
        aliases = {len(args) - 1: 2}
    o, lse, new = pl.pallas_call(
        functools.partial(_a_sample_kernel, aliased=new_cache is not None),
        grid=(b, n_ch),
        in_specs=in_specs,
        out_specs=[pl.BlockSpec((None, tq, A_WIDTH), lambda bi, c: (bi, 0, 0)),
                   pl.BlockSpec((None, tq, LANES), lambda bi, c: (bi, 0, 0)),
                   pl.BlockSpec((None, None, ch * A_ROWS, HEAD_DIM), lambda bi, c: (layer, bi, c, 0))],
        out_shape=[jax.ShapeDtypeStruct((b, tq, A_WIDTH), F32),
                   jax.ShapeDtypeStruct((b, tq, LANES), F32),
                   jax.ShapeDtypeStruct(old.shape, F32)],
        scratch_shapes=[pltpu.VMEM((A_HPG, tq, 1), F32), pltpu.VMEM((A_HPG, tq, 1), F32),
                        pltpu.VMEM((A_HPG, tq, HEAD_DIM), F32)],
        input_output_aliases=aliases,
        compiler_params=_cparams("parallel", "arbitrary"),
        name=f"a_attn_sample_g{g}",
    )(*args)
    return o, lse, new


def _conv_kernel(a_ref, gt_ref, gb_ref, hist_ref, cw_ref, cb_ref, lg_ref, lb_ref, pw_ref, pb_ref,
                 o_ref, nc_ref, ubuf, ybuf, pwb):
    ti = pl.program_id(1)
    nt = pl.num_programs(1)
    tt = a_ref.shape[0]
    nhist = CONV_W - 1
    pad = CONV_HALO - nhist

    @pl.when(ti == 0)
    def _():
        pwb[...] = pw_ref[...].astype(BF16)
        ubuf[0:pad, :] = jnp.zeros((pad, B_WIDTH), F32)
        ubuf[pad:CONV_HALO, :] = hist_ref[...]

    ubuf[CONV_HALO:CONV_HALO + tt, :] = a_ref[...] * jax.nn.sigmoid(gt_ref[...])
    for c in range(B_WIDTH // LANES):
        cs = slice(c * LANES, (c + 1) * LANES)
        acc = jnp.zeros((tt, LANES), F32) + cb_ref[:, cs]
        for k in range(CONV_W):
            acc = acc + ubuf[pad + k:pad + k + tt, cs] * cw_ref[k:k + 1, cs]
        ybuf[:, cs] = acc
    y = ybuf[...]
    mu = jnp.mean(y, axis=-1, keepdims=True)
    yc = y - mu
    var = jnp.mean(yc * yc, axis=-1, keepdims=True)
    yn = yc * lax.rsqrt(var + EPS) * lg_ref[...] + lb_ref[...]
    act = _silu(yn).astype(BF16)
    ob = jnp.dot(act, pwb[...], preferred_element_type=F32) + pb_ref[...]
    o_ref[...] = (ob * _silu(gb_ref[...])).astype(BF16)

    @pl.when(ti == nt - 1)
    def _():
        nc_ref[...] = ubuf[tt + pad:tt + CONV_HALO, :]

    @pl.when(ti < nt - 1)
    def _():
        ubuf[0:CONV_HALO, :] = ubuf[tt:tt + CONV_HALO, :]


def _conv_module(z, hist, layer, conv_w, conv_b, ln_g, ln_b, pw_w, pw_b, *, tt):
    b, t, n = z.shape
    assert t % tt == 0 and (tt >= CONV_HALO or t == tt)
    glu0 = (3 * A_QKV + A_WIDTH) // B_WIDTH
    row = lambda a: a.reshape(a.shape[0], 1, B_WIDTH)
    blk = (None, tt, B_WIDTH)
    lyr = lambda shape: pl.BlockSpec((None,) + shape, lambda bi, ti: (layer,) + (0,) * len(shape))
    return pl.pallas_call(
        _conv_kernel,
        grid=(b, t // tt),
        in_specs=[pl.BlockSpec(blk, lambda bi, ti: (bi, ti, glu0)),
                  pl.BlockSpec(blk, lambda bi, ti: (bi, ti, glu0 + 1)),
                  pl.BlockSpec(blk, lambda bi, ti: (bi, ti, glu0 + 2)),
                  pl.BlockSpec((None, CONV_W - 1, B_WIDTH), lambda bi, ti: (bi, 0, 0)),
                  lyr((CONV_W, B_WIDTH)), lyr((1, B_WIDTH)), lyr((1, B_WIDTH)), lyr((1, B_WIDTH)),
                  lyr((B_WIDTH, B_WIDTH)), lyr((1, B_WIDTH))],
        out_specs=[pl.BlockSpec(blk, lambda bi, ti: (bi, ti, 0)),
                   pl.BlockSpec((None, CONV_W - 1, B_WIDTH), lambda bi, ti: (bi, 0, 0))],
        out_shape=[jax.ShapeDtypeStruct((b, t, B_WIDTH), BF16),
                   jax.ShapeDtypeStruct((b, CONV_W - 1, B_WIDTH), F32)],
        scratch_shapes=[pltpu.VMEM((CONV_HALO + tt, B_WIDTH), F32),
                        pltpu.VMEM((tt, B_WIDTH), F32),
                        pltpu.VMEM((B_WIDTH, B_WIDTH), BF16)],
        compiler_params=_cparams("parallel", "arbitrary"),
        name="conv_module",
    )(z, z, z, hist, conv_w, row(conv_b), row(ln_g), row(ln_b), pw_w, row(pw_b))


def _even_mix_kernel(o0_ref, o1_ref, o2_ref, l0_ref, l1_ref, l2_ref, ga_ref, mix_ref):
    ls = [l0_ref[...], l1_ref[...], l2_ref[...]]
    m = jnp.maximum(jnp.maximum(ls[0], ls[1]), ls[2])
    es = [jnp.exp(l - m) for l in ls]
    inv = 1.0 / (es[0] + es[1] + es[2])
    ws = [e * inv for e in es]
    o_refs = (o0_ref, o1_ref, o2_ref)
    for h in range(A_HPG):
        cs = slice(h * HEAD_DIM, (h + 1) * HEAD_DIM)
        acc = ws[0][:, h:h + 1] * o_refs[0][:, cs]
        for g in range(1, A_NG):
            acc = acc + ws[g][:, h:h + 1] * o_refs[g][:, cs]
        mix_ref[:, cs] = (acc * _silu(ga_ref[:, cs])).astype(BF16)


def _even_mix(oas, lses, z2, *, tm):
    m = z2.shape[0]
    ga_blk = 3 * A_QKV // A_WIDTH
    wide = lambda c: pl.BlockSpec((tm, A_WIDTH), lambda i: (i, c))
    narrow = pl.BlockSpec((tm, LANES), lambda i: (i, 0))
    return pl.pallas_call(
        _even_mix_kernel,
        grid=(m // tm,),
        in_specs=[wide(0), wide(0), wide(0), narrow, narrow, narrow, wide(ga_blk)],
        out_specs=wide(0),
        out_shape=jax.ShapeDtypeStruct((m, A_WIDTH), BF16),
        compiler_params=_cparams("parallel"),
        name="even_mix",
    )(*oas, *lses, z2)


def _even_layer(x, a_caches, a_new, conv_hist, rel_bias, layer, norm_g, w_in, conv_w, conv_b, ln_g, ln_b, pw_w,
                pw_b, w_out, *, tm, tn_in, tn_out, tt):
    b, t, d = x.shape
    m = b * t
    x2 = x.reshape(m, d)
    z2 = _norm_matmul(x2, norm_g[layer], w_in, layer, tm=tm, tn=tn_in)
    z = z2.reshape(b, t, EVEN_IN)
    oas, lses, new_hists = [], [], []
    for g, (win, _) in enumerate(A_GROUPS):
        if a_caches is None:
            kcol = A_QKV + g * A_WIDTH
            vcol = 2 * A_QKV + g * A_WIDTH
            new_kv = jnp.stack([z[:, :, kcol:kcol + A_WIDTH], z[:, :, vcol:vcol + A_WIDTH]], axis=2)
            new_hists.append(new_kv[:, t - min(win, t):].reshape(b, min(win, t), 2, A_HPG, HEAD_DIM))
        else:
            assert a_caches[g].shape[2] == win
            o, lse, new = _a_attn_sample(z, a_caches[g], layer, None if a_new is None else a_new[g], rel_bias, g)
            new_hists.append(new)
            oas.append(o.reshape(m, A_WIDTH))
            lses.append(lse.reshape(m, LANES))
    if a_caches is None:
        mix_a = _a_attn_prompt(z, rel_bias).reshape(m, A_WIDTH)
    else:
        mix_a = _even_mix(oas, lses, z2, tm=min(tm, 256))
    mix_b, new_conv = _conv_module(z, conv_hist, layer, conv_w, conv_b, ln_g, ln_b, pw_w, pw_b, tt=tt)
    y = _matmul_res([mix_a, mix_b.reshape(m, B_WIDTH)], w_out, layer, x2, tm=tm, tn=tn_out)
    return y.reshape(b, t, d), new_hists, new_conv


ODD_KV0 = C_WIDTH
ODD_GL0 = C_WIDTH + 6 * C_KV
ODD_GL_PAD = 512
ODD_GP0 = ODD_GL0 + ODD_GL_PAD
ODD_Z = ODD_GP0 + C_WIDTH
CMP_HALF = CMP_BLK // 2
CMP_K = CMP_HALF * HEAD_DIM
PAGES_PER_STEP = 16
CACHE_ROW_KINDS = 4 * C_KV_HEADS


def _repack_w_in_odd(w):
    n_gl = 3 * C_HEADS
    pad = jnp.zeros(w.shape[:2] + (ODD_GL_PAD - n_gl,), w.dtype)
    return jnp.concatenate([w[..., :ODD_GL0 + n_gl], pad, w[..., ODD_GL0 + n_gl:]], axis=-1)


def _cmp_partials_kernel(*refs, n_src, n_prefetch, interleaved):
    refs = refs[n_prefetch:]
    src = refs[:n_src]
    pos_ref, w1_ref, a_ref, b_ref, rows_ref = refs[n_src:n_src + 5]
    n = rows_ref.shape[1] // CMP_HALF
    r0 = 0
    for r in src:
        nr = r.shape[0] // CACHE_ROW_KINDS if interleaved else r.shape[0]
        for c in range(2 * C_KV_HEADS):
            if interleaved:
                rows_ref[c, r0:r0 + nr, :] = r[pl.ds(c, nr, stride=CACHE_ROW_KINDS), :]
            else:
                rows_ref[c, r0:r0 + nr, :] = r[:, c * HEAD_DIM:(c + 1) * HEAD_DIM]
        r0 += nr
    for kv in range(2):
        x = jnp.concatenate(
            [jnp.concatenate([rows_ref[kv * C_KV_HEADS + g, pl.ds(l, n, stride=CMP_HALF), :]
                              for l in range(CMP_HALF)], axis=1) for g in range(C_KV_HEADS)], axis=0)
        for half, out in ((0, a_ref), (1, b_ref)):
            y = jnp.dot((x + pos_ref[kv, half]).astype(BF16), w1_ref[kv, half].astype(BF16),
                        preferred_element_type=F32)
            for g in range(C_KV_HEADS):
                out[kv * C_KV_HEADS + g] = y[g * n:(g + 1) * n]


def _cmp_weights(cmp_pos, cmp_w1):
    n = cmp_pos.shape[0]
    return cmp_pos.reshape(n, 2, 2, 1, CMP_K), cmp_w1.reshape(n, 2, 2, CMP_K, CMP_HID)


def _cmp_partials_prompt(z, layer, cmp_pos, cmp_w1):
    b, t, _ = z.shape
    nch = t // CMP_STRIDE
    pos, w1 = _cmp_weights(cmp_pos, cmp_w1)
    kern = functools.partial(_cmp_partials_kernel, n_src=1, n_prefetch=0, interleaved=False)
    out = jax.ShapeDtypeStruct((b, 4, nch, CMP_HID), F32)
    ospec = pl.BlockSpec((None, 4, nch, CMP_HID), lambda bi: (bi, 0, 0, 0))
    return pl.pallas_call(
        kern,
        grid=(b,),
        in_specs=[pl.BlockSpec((None, t, 2 * C_KV), lambda bi: (bi, 0, ODD_KV0 // (2 * C_KV))),
                  pl.BlockSpec((None,) + pos.shape[1:], lambda bi: (layer, 0, 0, 0, 0)),
                  pl.BlockSpec((None,) + w1.shape[1:], lambda bi: (layer, 0, 0, 0, 0))],
        out_specs=[ospec, ospec],
        out_shape=[out, out],
        scratch_shapes=[pltpu.VMEM((2 * C_KV_HEADS, t, HEAD_DIM), F32)],
        compiler_params=_cparams("parallel"),
        name="cmp_partials_prompt",
    )(z, pos, w1)


def _cmp_partials_paged(cache, layer, page_table, cmp_pos, cmp_w1):
    b, n_pages = page_table.shape
    page = cache.shape[2] // CACHE_ROW_KINDS
    pps = PAGES_PER_STEP
    assert n_pages % pps == 0 and page % CMP_STRIDE == 0
    nch_step = pps * page // CMP_STRIDE
    nch = n_pages * page // CMP_STRIDE
    pos, w1 = _cmp_weights(cmp_pos, cmp_w1)
    kern = functools.partial(_cmp_partials_kernel, n_src=pps, n_prefetch=1, interleaved=True)
    out = jax.ShapeDtypeStruct((b, 4, nch, CMP_HID), F32)
    ospec = pl.BlockSpec((None, 4, nch_step, CMP_HID), lambda bi, pg, pt: (bi, 0, pg, 0))

    def page_spec(j):
        return pl.BlockSpec((None, None, page * CACHE_ROW_KINDS, HEAD_DIM),
                            lambda bi, pg, pt: (layer, pt[bi, pg * pps + j], 0, 0))

    grid_spec = pltpu.PrefetchScalarGridSpec(
        num_scalar_prefetch=1,
        grid=(b, n_pages // pps),
        in_specs=[page_spec(j) for j in range(pps)] + [
            pl.BlockSpec((None,) + pos.shape[1:], lambda bi, pg, pt: (layer, 0, 0, 0, 0)),
            pl.BlockSpec((None,) + w1.shape[1:], lambda bi, pg, pt: (layer, 0, 0, 0, 0))],
        out_specs=[ospec, ospec],
        scratch_shapes=[pltpu.VMEM((2 * C_KV_HEADS, pps * page, HEAD_DIM), F32)])
    return pl.pallas_call(
        kern, grid_spec=grid_spec, out_shape=[out, out],
        compiler_params=_cparams("parallel", "arbitrary"),
        name="cmp_partials_paged",
    )(page_table, *([cache] * pps), pos, w1)


def _overlap_np(n_ch, n_sel, ns_pad):
    ci = np.arange(n_ch)[:, None]
    si = np.arange(ns_pad)[None, :]
    ov = (ci * CMP_STRIDE < (si + 1) * SEL_BLK) & (ci * CMP_STRIDE + CMP_BLK > si * SEL_BLK)
    ov &= (ci < n_ch - 1) & (si < n_sel)
    return ov.astype(np.float32)


def _cmp_select_kernel(q_ref, ak_ref, bk_ref, av_ref, bv_ref, b1_ref, w2_ref, ov_ref, oc_ref, sel_ref,
                       *, qpos0, n_sel):
    qi = pl.program_id(2)
    tq = q_ref.shape[0]
    nch = ak_ref.shape[0]
    ns_pad = ov_ref.shape[1]

    def finish(a_ref, b_ref, kv):
        hid = a_ref[...] + pltpu.roll(b_ref[...], nch - 1, 0) + b1_ref[kv:kv + 1, :]
        return jnp.dot(_silu(hid).astype(BF16), w2_ref[kv].astype(BF16),
                       preferred_element_type=F32).astype(BF16)

    k_cmp = finish(ak_ref, bk_ref, 0)
    v_cmp = finish(av_ref, bv_ref, 1)
    qpos = qpos0 + qi * tq + lax.broadcasted_iota(jnp.int32, (tq, nch), 0)
    cmp_end = lax.broadcasted_iota(jnp.int32, (tq, nch), 1) * CMP_STRIDE + (CMP_BLK - 1)
    ok = (cmp_end <= qpos) & (cmp_end < (nch - 1) * CMP_STRIDE + CMP_BLK - 1)
    q = jnp.concatenate([(q_ref[:, h * HEAD_DIM:(h + 1) * HEAD_DIM] * SCALE).astype(BF16) for h in range(C_HPG)],
                        axis=0)
    s = lax.dot_general(q, k_cmp, (((1,), (1,)), ((), ())), preferred_element_type=F32)
    s = jnp.where(ok[None], s.reshape(C_HPG, tq, nch), NEG)
    m = jnp.max(s, axis=-1, keepdims=True)
    e = jnp.where(ok[None], jnp.exp(s - m), 0.0)
    den = jnp.sum(e, axis=-1, keepdims=True)
    p = e / jnp.where(den > 0.0, den, 1.0)
    psum = jnp.sum(p, axis=0)
    oc = jnp.dot(p.reshape(C_HPG * tq, nch).astype(BF16), v_cmp, preferred_element_type=F32)
    for h in range(C_HPG):
        oc_ref[:, h * HEAD_DIM:(h + 1) * HEAD_DIM] = oc[h * tq:(h + 1) * tq]
    imp = jnp.dot(psum, ov_ref[...], preferred_element_type=F32, precision=lax.Precision.HIGHEST)
    blk = lax.broadcasted_iota(jnp.int32, (tq, ns_pad), 1)
    qblk = (qpos0 + qi * tq + lax.broadcasted_iota(jnp.int32, (tq, ns_pad), 0)) // SEL_BLK
    forced = (blk == 0) | (blk == qblk) | (blk == qblk - 1)
    allowed = (blk <= qblk) & (blk < n_sel)
    score = jnp.where(allowed, jnp.where(forced, -NEG, imp), -1.0)
    rank = jnp.zeros((tq, ns_pad), jnp.int32)
    for j in range(n_sel):
        cj = score[:, j:j + 1]
        before = (cj > score) | ((cj == score) & (j < blk))
        rank = rank + before.astype(jnp.int32)
    sel_ref[...] = (allowed & (rank < SEL_N)).astype(F32)


def _cmp_select(z, a, bm, layer, cmp_b1, cmp_w2, *, tq, qpos0, n_sel):
    b, t, _ = z.shape
    nch = a.shape[2]
    ns_pad = -(-n_sel // LANES) * LANES
    ov = jnp.asarray(_overlap_np(nch, n_sel, ns_pad))
    kern = functools.partial(_cmp_select_kernel, qpos0=qpos0, n_sel=n_sel)
    part = lambda kv: pl.BlockSpec((None, None, nch, CMP_HID), lambda bi, g, qi: (bi, kv * C_KV_HEADS + g, 0, 0))
    return pl.pallas_call(
        kern,
        grid=(b, C_KV_HEADS, t // tq),
        in_specs=[pl.BlockSpec((None, tq, C_HPG * HEAD_DIM), lambda bi, g, qi: (bi, qi, g)),
                  part(0), part(0), part(1), part(1),
                  pl.BlockSpec((None, 2, CMP_HID), lambda bi, g, qi: (layer, 0, 0)),
                  pl.BlockSpec((None, 2, CMP_HID, HEAD_DIM), lambda bi, g, qi: (layer, 0, 0, 0)),
                  pl.BlockSpec((nch, ns_pad), lambda bi, g, qi: (0, 0))],
        out_specs=[pl.BlockSpec((None, tq, C_HPG * HEAD_DIM), lambda bi, g, qi: (bi, qi, g)),
                   pl.BlockSpec((None, None, tq, ns_pad), lambda bi, g, qi: (bi, g, qi, 0))],
        out_shape=[jax.ShapeDtypeStruct((b, t, C_WIDTH), F32),
                   jax.ShapeDtypeStruct((b, C_KV_HEADS, t, ns_pad), F32)],
        compiler_params=_cparams("parallel", "parallel", "arbitrary"),
        name="cmp_select",
    )(z, a, bm, a, bm, cmp_b1, cmp_w2, ov)


def _sel_prompt_kernel(q_ref, k_ref, v_ref, rev_ref, sel_ref, o_ref,
                       bias_ref, msk_ref, qb_ref, m_ref, l_ref, acc_ref):
    qi = pl.program_id(2)
    tq = q_ref.shape[0]
    nt = bias_ref.shape[0] - 1
    tk = 2 * tq
    ns_pad = sel_ref.shape[1]

    @pl.when(qi == 0)
    def _():
        bias_ref[0] = jnp.full((C_HPG, tq, tq), NEG, F32)
        for h in range(C_HPG):
            for delta in range(nt):
                off = (nt - 1 - delta) * tq
                bias_ref[delta + 1, h] = _toeplitz(rev_ref[h, :, off:off + 2 * tq], tq).T

    sel_t = sel_ref[...].T.astype(BF16)
    erow = lax.broadcasted_iota(jnp.int32, (tk, ns_pad), 0) // SEL_BLK
    ecol = lax.broadcasted_iota(jnp.int32, (tk, ns_pad), 1)
    for t in range(nt // 2):
        expand = (ecol == erow + t * (tk // SEL_BLK)).astype(BF16)
        hit = jnp.dot(expand, sel_t, preferred_element_type=F32)
        msk_ref[t] = jnp.where(hit > 0.5, 0.0, NEG)
    for h in range(C_HPG):
        qb_ref[h] = (q_ref[:, h * HEAD_DIM:(h + 1) * HEAD_DIM] * SCALE).astype(BF16)
    m_ref[...] = jnp.full(m_ref.shape, NEG, F32)
    l_ref[...] = jnp.zeros(l_ref.shape, F32)
    acc_ref[...] = jnp.zeros(acc_ref.shape, F32)

    def body(kj, carry):
        keys = pl.ds(pl.multiple_of(kj * tk, tk), tk)
        k = k_ref[keys, :].astype(BF16)
        v_t = v_ref[keys, :].T.astype(BF16)
        msk = msk_ref[kj]
        d0 = qi - 2 * kj
        for h in range(C_HPG):
            bias = jnp.concatenate([bias_ref[d0 + 1, h], bias_ref[d0, h]], axis=0)
            s = lax.dot_general(k, qb_ref[h], (((1,), (1,)), ((), ())), preferred_element_type=F32)
            s = s + bias + msk
            m_old = m_ref[h:h + 1, :]
            m_new = jnp.maximum(m_old, jnp.max(s, axis=0, keepdims=True))
            alpha = jnp.exp(m_old - m_new)
            e = jnp.exp(s - m_new)
            m_ref[h:h + 1, :] = m_new
            l_ref[h:h + 1, :] = alpha * l_ref[h:h + 1, :] + jnp.sum(e, axis=0, keepdims=True)
            acc_ref[h] = alpha * acc_ref[h] + jnp.dot(v_t, e.astype(BF16), preferred_element_type=F32)
        return carry

    lax.fori_loop(0, (qi + 2) // 2, body, 0)
    for h in range(C_HPG):
        o_ref[:, h * HEAD_DIM:(h + 1) * HEAD_DIM] = (acc_ref[h] / l_ref[h:h + 1, :]).T


def _sel_attn_prompt(z, sel, rel_bias, *, tq):
    b, t, _ = z.shape
    nt = t // tq
    assert nt % 2 == 0
    ns_pad = sel.shape[3]
    rev = _c_rev_table(rel_bias, t, tq, t)
    kcol = (ODD_KV0 + 2 * C_KV) // HEAD_DIM
    vcol = (ODD_KV0 + 3 * C_KV) // HEAD_DIM
    return pl.pallas_call(
        _sel_prompt_kernel,
        grid=(b, C_KV_HEADS, nt),
        in_specs=[pl.BlockSpec((None, tq, C_HPG * HEAD_DIM), lambda bi, g, qi: (bi, qi, g)),
                  pl.BlockSpec((None, t, HEAD_DIM), lambda bi, g, qi: (bi, 0, kcol + g)),
                  pl.BlockSpec((None, t, HEAD_DIM), lambda bi, g, qi: (bi, 0, vcol + g)),
                  pl.BlockSpec((C_HPG, 1, t + tq), lambda bi, g, qi: (g, 0, 0)),
                  pl.BlockSpec((None, None, tq, ns_pad), lambda bi, g, qi: (bi, g, qi, 0))],
        out_specs=pl.BlockSpec((None, tq, C_HPG * HEAD_DIM), lambda bi, g, qi: (bi, qi, g)),
        out_shape=jax.ShapeDtypeStruct((b, t, C_WIDTH), F32),
        scratch_shapes=[pltpu.VMEM((nt + 1, C_HPG, tq, tq), F32),
                        pltpu.VMEM((nt // 2, 2 * tq, tq), F32),
                        pltpu.VMEM((C_HPG, tq, HEAD_DIM), BF16),
                        pltpu.VMEM((C_HPG, tq), F32),
                        pltpu.VMEM((C_HPG, tq), F32),
                        pltpu.VMEM((C_HPG, HEAD_DIM, tq), F32)],
        compiler_params=_cparams("parallel", "parallel", "arbitrary"),
        name="sel_attn_prompt",
    )(z, z, z, rev, sel)


def _sel_paged_kernel(pt_ref, *refs, n_src, past):
    del pt_ref
    pages = refs[:n_src]
    q_ref, new_ref, sel_ref, bias_ref, bnew_ref, o_ref, m_ref, l_ref, acc_ref = refs[n_src:]
    pg = pl.program_id(1)
    tq = q_ref.shape[0]
    rows = C_HPG * tq
    page = pages[0].shape[0] // CACHE_ROW_KINDS
    keys = n_src * page
    ns_pad = sel_ref.shape[2]

    def page_rows(r, kind):
        return r[pl.ds(kind, page, stride=CACHE_ROW_KINDS), :]

    def q_rows(g):
        return jnp.concatenate(
            [(q_ref[:, (g * C_HPG + h) * HEAD_DIM:(g * C_HPG + h + 1) * HEAD_DIM] * SCALE).astype(BF16)
             for h in range(C_HPG)], axis=0)

    @pl.when(pg == 0)
    def _():
        for g in range(C_KV_HEADS):
            k = new_ref[:, g * HEAD_DIM:(g + 1) * HEAD_DIM].astype(BF16)
            v = new_ref[:, C_KV + g * HEAD_DIM:C_KV + (g + 1) * HEAD_DIM].astype(BF16)
            s = lax.dot_general(q_rows(g), k, (((1,), (1,)), ((), ())), preferred_element_type=F32)
            s = s + bnew_ref[g * C_HPG:(g + 1) * C_HPG].reshape(rows, tq)
            m = jnp.max(s, axis=-1, keepdims=True)
            e = jnp.exp(s - m)
            m_ref[g] = m
            l_ref[g] = jnp.sum(e, axis=-1, keepdims=True)
            acc_ref[g] = jnp.dot(e.astype(BF16), v, preferred_element_type=F32)

    srow = lax.broadcasted_iota(jnp.int32, (ns_pad, keys), 0)
    scol = lax.broadcasted_iota(jnp.int32, (ns_pad, keys), 1) // SEL_BLK + pg * (keys // SEL_BLK)
    expand = (srow == scol).astype(BF16)
    for g in range(C_KV_HEADS):
        k = jnp.concatenate([page_rows(r, 2 * C_KV_HEADS + g) for r in pages], axis=0).astype(BF16)
        v = jnp.concatenate([page_rows(r, 3 * C_KV_HEADS + g) for r in pages], axis=0).astype(BF16)
        hit = jnp.dot(sel_ref[g].astype(BF16), expand, preferred_element_type=F32)
        mask = jnp.where(hit > 0.5, 0.0, NEG)
        s = lax.dot_general(q_rows(g), k, (((1,), (1,)), ((), ())), preferred_element_type=F32)
        s = (s.reshape(C_HPG, tq, keys) + bias_ref[g * C_HPG:(g + 1) * C_HPG] + mask[None]).reshape(rows, keys)
        m_old = m_ref[g]
        m_new = jnp.maximum(m_old, jnp.max(s, axis=-1, keepdims=True))
        alpha = jnp.exp(m_old - m_new)
        e = jnp.exp(s - m_new)
        m_ref[g] = m_new
        l_ref[g] = alpha * l_ref[g] + jnp.sum(e, axis=-1, keepdims=True)
        acc_ref[g] = alpha * acc_ref[g] + jnp.dot(e.astype(BF16), v, preferred_element_type=F32)

    @pl.when(pg == pl.num_programs(1) - 1)
    def _():
        for g in range(C_KV_HEADS):
            o = acc_ref[g] / l_ref[g]
            for h in range(C_HPG):
                o_ref[:, (g * C_HPG + h) * HEAD_DIM:(g * C_HPG + h + 1) * HEAD_DIM] = o[h * tq:(h + 1) * tq]


def _sel_attn_paged(z, sel, cache, layer, page_table, rel_bias):
    b, tq, _ = z.shape
    n_pages = page_table.shape[1]
    page = cache.shape[2] // CACHE_ROW_KINDS
    past = n_pages * page
    pps = PAGES_PER_STEP
    keys = pps * page
    ns_pad = sel.shape[3]
    tab = _c_tab(rel_bias)
    rev = tab[_t5_bucket_np(past + tq - 1 - np.arange(past + tq - 1))].T.astype(F32)
    bias = jnp.stack([rev[:, tq - 1 - t:tq - 1 - t + past] for t in range(tq)], axis=1)
    dnew = np.arange(tq)[:, None] - np.arange(tq)[None, :]
    bnew = jnp.where(jnp.asarray(dnew >= 0)[None],
                     jnp.transpose(tab[_t5_bucket_np(np.clip(dnew, 0, None))], (2, 0, 1)), NEG).astype(F32)
    kern = functools.partial(_sel_paged_kernel, n_src=pps, past=past)

    def page_spec(j):
        return pl.BlockSpec((None, None, page * CACHE_ROW_KINDS, HEAD_DIM),
                            lambda bi, pg, pt: (layer, pt[bi, pg * pps + j], 0, 0))

    rows = C_HPG * tq
    grid_spec = pltpu.PrefetchScalarGridSpec(
        num_scalar_prefetch=1,
        grid=(b, n_pages // pps),
        in_specs=[page_spec(j) for j in range(pps)] + [
            pl.BlockSpec((None, tq, C_WIDTH), lambda bi, pg, pt: (bi, 0, 0)),
            pl.BlockSpec((None, tq, 2 * C_KV), lambda bi, pg, pt: (bi, 0, (ODD_KV0 + 2 * C_KV) // (2 * C_KV))),
            pl.BlockSpec((None, C_KV_HEADS, tq, ns_pad), lambda bi, pg, pt: (bi, 0, 0, 0)),
            pl.BlockSpec((C_HEADS, tq, keys), lambda bi, pg, pt: (0, 0, pg)),
            pl.BlockSpec((C_HEADS, tq, tq), lambda bi, pg, pt: (0, 0, 0))],
        out_specs=pl.BlockSpec((None, tq, C_WIDTH), lambda bi, pg, pt: (bi, 0, 0)),
        scratch_shapes=[pltpu.VMEM((C_KV_HEADS, rows, 1), F32), pltpu.VMEM((C_KV_HEADS, rows, 1), F32),
                        pltpu.VMEM((C_KV_HEADS, rows, HEAD_DIM), F32)])
    return pl.pallas_call(
        kern, grid_spec=grid_spec, out_shape=jax.ShapeDtypeStruct((b, tq, C_WIDTH), F32),
        compiler_params=_cparams("parallel", "arbitrary"),
        name="sel_attn_paged",
    )(page_table, *([cache] * pps), z, z, sel, bias, bnew)


def _win_attn_prompt(z, rel_bias, *, tq):
    b, t, _ = z.shape
    n_tiles = WIN // tq + 1
    rev = _c_rev_table(rel_bias, n_tiles * tq, tq, WIN)
    kv_blk = (ODD_KV0 + 4 * C_KV) // (2 * C_KV)
    kern = functools.partial(_tile_attn_kernel, nh=C_HEADS, rep=C_HPG, n_tiles=n_tiles, k_col=0, v_col=C_KV,
                             shared_kv=True, with_lse=False, lead_axis=1, toeplitz=True)

    def kv_spec(p):
        return pl.BlockSpec((None, tq, 2 * C_KV),
                            lambda bi, qi: (bi, jnp.maximum(qi - (n_tiles - 1 - p), 0), kv_blk))

    return pl.pallas_call(
        kern,
        grid=(b, t // tq),
        in_specs=[pl.BlockSpec((None, tq, C_WIDTH), lambda bi, qi: (bi, qi, 0))]
                 + [kv_spec(p) for p in range(n_tiles)]
                 + [pl.BlockSpec((C_HEADS, 1, (n_tiles + 1) * tq), lambda bi, qi: (0, 0, 0))],
        out_specs=pl.BlockSpec((None, tq, C_WIDTH), lambda bi, qi: (bi, qi, 0)),
        out_shape=jax.ShapeDtypeStruct((b, t, C_WIDTH), F32),
        scratch_shapes=[pltpu.VMEM((C_HEADS, tq, n_tiles * tq), F32)],
        compiler_params=_cparams("parallel", "arbitrary"),
        name="win_attn_prompt",
    )(z, *([z] * n_tiles), rev)


def _win_attn_sample(z, win_full, rel_bias):
    b, tq, _ = z.shape
    lk = win_full.shape[1]
    dist = (lk - tq) + np.arange(tq)[:, None] - np.arange(lk)[None, :]
    valid = (dist >= 0) & (dist <= WIN)
    bias = jnp.transpose(_c_tab(rel_bias)[_t5_bucket_np(np.clip(dist, 0, None))], (2, 0, 1))
    bias = jnp.where(jnp.asarray(valid)[None], bias, NEG).astype(F32)
    kern = functools.partial(_tile_attn_kernel, nh=C_HEADS, rep=C_HPG, n_tiles=1, k_col=0, v_col=C_KV,
                             shared_kv=True, with_lse=False, lead_axis=0)
    return pl.pallas_call(
        kern,
        grid=(b,),
        in_specs=[pl.BlockSpec((None, tq, C_WIDTH), lambda bi: (bi, 0, 0)),
                  pl.BlockSpec((None, lk, 2 * C_KV), lambda bi: (bi, 0, 0)),
                  pl.BlockSpec((C_HEADS, tq, lk), lambda bi: (0, 0, 0))],
        out_specs=pl.BlockSpec((None, tq, C_WIDTH), lambda bi: (bi, 0, 0)),
        out_shape=jax.ShapeDtypeStruct((b, tq, C_WIDTH), F32),
        compiler_params=_cparams("parallel"),
        name="win_attn_sample",
    )(z, win_full, bias)


def _odd_mix_kernel(oc_ref, os_ref, ow_ref, gl_ref, gp_ref, mix_ref):
    gates = jax.nn.sigmoid(gl_ref[...])
    branches = (oc_ref, os_ref, ow_ref)
    for h in range(C_HEADS):
        cs = slice(h * HEAD_DIM, (h + 1) * HEAD_DIM)
        acc = gates[:, h:h + 1] * branches[0][:, cs]
        for br in range(1, 3):
            lane = br * C_HEADS + h
            acc = acc + gates[:, lane:lane + 1] * branches[br][:, cs]
        mix_ref[:, cs] = (acc * _silu(gp_ref[:, cs])).astype(BF16)


def _odd_mix(oc, osel, ow, z2, *, tm):
    m = z2.shape[0]
    wide = lambda c: pl.BlockSpec((tm, C_WIDTH), lambda i: (i, c))
    return pl.pallas_call(
        _odd_mix_kernel,
        grid=(m // tm,),
        in_specs=[wide(0), wide(0), wide(0),
                  pl.BlockSpec((tm, LANES), lambda i: (i, ODD_GL0 // LANES)),
                  wide(ODD_GP0 // C_WIDTH)],
        out_specs=wide(0),
        out_shape=jax.ShapeDtypeStruct((m, C_WIDTH), BF16),
        compiler_params=_cparams("parallel"),
        name="odd_mix",
    )(oc, osel, ow, z2, z2)


def _odd_layer(x, past, win_hist, rel_bias, layer, norm_g, w_in_packed, cmp_pos, cmp_w1, cmp_b1, cmp_w2, w_out,
               *, tm, tn_in, tn_out, tq):
    b, t, d = x.shape
    m = b * t
    x2 = x.reshape(m, d)
    z2 = _norm_matmul(x2, norm_g[layer], w_in_packed, layer, tm=tm, tn=tn_in)
    z = z2.reshape(b, t, ODD_Z)
    rows = z[:, :, ODD_KV0:ODD_KV0 + 4 * C_KV].reshape(b, t, 4, C_KV_HEADS, HEAD_DIM)
    win_new = z[:, :, ODD_KV0 + 4 * C_KV:ODD_KV0 + 6 * C_KV]
    if past is None:
        length = t
        a, bm = _cmp_partials_prompt(z, layer, cmp_pos, cmp_w1)
        qpos0 = 0
    else:
        cache, page_table = past
        p_len = page_table.shape[1] * (cache.shape[2] // CACHE_ROW_KINDS)
        length = p_len + t
        assert length // CMP_STRIDE == p_len // CMP_STRIDE and p_len % SEL_BLK == 0
        a, bm = _cmp_partials_paged(cache, layer, page_table, cmp_pos, cmp_w1)
        qpos0 = p_len
    n_sel = -(-length // SEL_BLK)
    oc, sel = _cmp_select(z, a, bm, layer, cmp_b1, cmp_w2, tq=tq, qpos0=qpos0, n_sel=n_sel)
    if past is None:
        osel = _sel_attn_prompt(z, sel, rel_bias, tq=tq)
        ow = _win_attn_prompt(z, rel_bias, tq=tq)
        new_win = win_new[:, t - min(WIN, t):].reshape(b, min(WIN, t), 2, C_KV_HEADS, HEAD_DIM)
    else:
        osel = _sel_attn_paged(z, sel, cache, layer, page_table, rel_bias)
        win_full = jnp.concatenate([win_hist.reshape(b, -1, 2 * C_KV), win_new], axis=1)
        ow = _win_attn_sample(z, win_full, rel_bias)
        lk = win_full.shape[1]
        keep = min(WIN, length)
        new_win = win_full[:, lk - keep:].reshape(b, keep, 2, C_KV_HEADS, HEAD_DIM)
    mix = _odd_mix(oc.reshape(m, C_WIDTH), osel.reshape(m, C_WIDTH), ow.reshape(m, C_WIDTH), z2, tm=min(tm, 256))
    y = _matmul_res([mix], w_out, layer, x2, tm=tm, tn=tn_out)
    return y.reshape(b, t, d), rows, new_win


def _run_trunk(x, a_caches, conv_state, c_cache, c_win, page_table, rel_bias, norm_even, w_in_even, conv_w,
               conv_b, conv_ln_g, conv_ln_b, conv_pw_w, conv_pw_b, w_out_even, norm_odd, w_in_odd_packed,
               cmp_pos, cmp_w1, cmp_b1, cmp_w2, w_out_odd, final_norm, *, tm, tt, tq):
    b, t, d = x.shape
    prompt = a_caches is None
    depth = norm_even.shape[0] + norm_odd.shape[0]
    new_a = [[] for _ in A_GROUPS]
    a_new = None
    new_conv, new_rows, new_win = [], [], []
    for depth_i in range(depth):
        i = depth_i // 2
        if depth_i % 2 == 0:
            conv_hist = jnp.zeros((b, CONV_W - 1, B_WIDTH), F32) if prompt else conv_state[i]
            x, hists, conv = _even_layer(x, a_caches, a_new, conv_hist, rel_bias, i, norm_even, w_in_even, conv_w,
                                         conv_b, conv_ln_g, conv_ln_b, conv_pw_w, conv_pw_b,
                                         w_out_even, tm=tm, tn_in=512, tn_out=512, tt=tt)
            if prompt:
                for g in range(A_NG):
                    new_a[g].append(hists[g])
            else:
                a_new = hists
            new_conv.append(conv)
        else:
            past = None if prompt else (c_cache, page_table)
            x, rows, win = _odd_layer(x, past, None if prompt else c_win[i], rel_bias, i, norm_odd,
                                      w_in_odd_packed, cmp_pos, cmp_w1, cmp_b1, cmp_w2,
                                      w_out_odd, tm=tm, tn_in=512, tn_out=512, tq=tq)
            new_rows.append(rows)
            new_win.append(win)
    y = _rms_norm(x.reshape(b * t, d), final_norm, tm=min(tm, 256)).reshape(b, t, d)
    if prompt:
        a_out = [jnp.stack(a) for a in new_a]
    else:
        a_out = [n.reshape(c.shape) for n, c in zip(a_new, a_caches)]
    return (y, a_out, jnp.stack(new_conv), jnp.stack(new_rows), jnp.stack(new_win))


def kernel(x_prompt, x_sample, cache_a_kv0, cache_a_kv1, cache_a_kv2, state_b_conv, cache_c_kv, cache_c_win, page_table, rel_bias, norm_even, w_in_even, conv_w, conv_b, conv_ln_g, conv_ln_b, conv_pw_w, conv_pw_b, w_out_even, norm_odd, w_in_odd, cmp_pos, cmp_w1, cmp_b1, cmp_w2, w_out_odd, final_norm):
    w_in_odd_packed = _repack_w_in_odd(w_in_odd)
    weights = (rel_bias, norm_even, w_in_even, conv_w, conv_b, conv_ln_g, conv_ln_b, conv_pw_w, conv_pw_b,
               w_out_even, norm_odd, w_in_odd_packed, cmp_pos, cmp_w1, cmp_b1, cmp_w2, w_out_odd, final_norm)
    y_p, a_p, conv_p, rows_p, win_p = _run_trunk(x_prompt, None, None, None, None, None, *weights,
                                                 tm=1024, tt=256, tq=128)
    n_odd, n_pool, page = cache_c_kv.shape[:3]
    c_cache = cache_c_kv.reshape(n_odd, n_pool, page * CACHE_ROW_KINDS, HEAD_DIM)
    db, dt = x_sample.shape[:2]
    y_s, a_s, conv_s, rows_s, win_s = _run_trunk(x_sample, (cache_a_kv0, cache_a_kv1, cache_a_kv2), state_b_conv,
                                                 c_cache, cache_c_win, page_table, *weights,
                                                 tm=db * dt, tt=dt, tq=dt)
    return (y_p, y_s, a_p[0], a_p[1], a_p[2], conv_p, rows_p, win_p, a_s[0], a_s[1], a_s[2], conv_s, rows_s, win_s)
```

```python
import functools
import math

import numpy as np
import jax
import jax.numpy as jnp
from jax import lax
from jax.experimental import pallas as pl
from jax.experimental.pallas import tpu as pltpu

F32 = jnp.float32
BF16 = jnp.bfloat16

D_MODEL = 2048
HEAD_DIM = 128
LANES = 128
SUBLANES = 8
A_GROUPS = ((128, 1), (512, 4), (2048, 16))
A_NG = 3
A_HPG = 8
A_QKV = A_NG * A_HPG * HEAD_DIM
A_WIDTH = A_HPG * HEAD_DIM
B_WIDTH = 1024
CONV_W = 31
CONV_HALO = 32
EVEN_IN = 3 * A_QKV + A_WIDTH + 3 * B_WIDTH
C_HEADS = 16
C_KV_HEADS = 2
C_HPG = C_HEADS // C_KV_HEADS
C_WIDTH = C_HEADS * HEAD_DIM
C_KV = C_KV_HEADS * HEAD_DIM
CMP_BLK = 32
CMP_STRIDE = 16
CMP_HID = 128
SEL_BLK = 64
SEL_N = 16
WIN = 512
NUM_BUCKETS = 32
MAX_DIST = 2048
EPS = 1e-6
NEG = -1e30
SCALE = HEAD_DIM ** -0.5
LOG2E = math.log2(math.e)
VMEM_LIMIT = 56 * 1024 * 1024


def _cparams(*sem):
    return pltpu.CompilerParams(dimension_semantics=sem, vmem_limit_bytes=VMEM_LIMIT)


def _t5_bucket_np(dist):
    max_exact = NUM_BUCKETS // 2
    d = np.maximum(np.asarray(dist, np.int64), 0)
    ratio = np.log(np.maximum(d, 1).astype(np.float64) / max_exact) / math.log(MAX_DIST / max_exact)
    large = np.minimum(max_exact + (ratio * (NUM_BUCKETS - max_exact)).astype(np.int64), NUM_BUCKETS - 1)
    return np.where(d < max_exact, d, large).astype(np.int32)


def _silu(x):
    return x * jax.nn.sigmoid(x)


NORM_ROWS = 256


def _norm_matmul_kernel(x_ref, g_ref, w_ref, o_ref, xn_ref):
    @pl.when(pl.program_id(1) == 0)
    def _():
        rows = min(NORM_ROWS, x_ref.shape[0])

        def norm(c, carry):
            cur = pl.ds(pl.multiple_of(c * rows, rows), rows)
            x = x_ref[cur, :]
            ms = jnp.mean(x * x, axis=-1, keepdims=True)
            xn_ref[cur, :] = (x * lax.rsqrt(ms + EPS) * g_ref[...]).astype(BF16)
            return carry

        lax.fori_loop(0, x_ref.shape[0] // rows, norm, 0)

    o_ref[...] = jnp.dot(xn_ref[...], w_ref[...].astype(BF16), preferred_element_type=F32)


def _norm_matmul(x, g, w, layer, *, tm, tn, n_out=None):
    m, k = x.shape
    n = w.shape[2] if n_out is None else n_out
    assert n % tn == 0 and n <= w.shape[2]
    return pl.pallas_call(
        _norm_matmul_kernel,
        grid=(m // tm, n // tn),
        in_specs=[pl.BlockSpec((tm, k), lambda i, j: (i, 0), pipeline_mode=pl.Buffered(1)),
                  pl.BlockSpec((1, k), lambda i, j: (0, 0)),
                  pl.BlockSpec((None, k, tn), lambda i, j: (layer, 0, j))],
        out_specs=pl.BlockSpec((tm, tn), lambda i, j: (i, j)),
        out_shape=jax.ShapeDtypeStruct((m, n), F32),
        scratch_shapes=[pltpu.VMEM((tm, k), BF16)],
        compiler_params=_cparams("parallel", "arbitrary"),
        name="norm_matmul",
    )(x, g.reshape(1, k), w)


def _matmul_res_kernel(*refs, n_parts):
    a_refs = refs[:n_parts]
    w_ref, r_ref, o_ref = refs[n_parts:]
    acc = r_ref[...]
    k0 = 0
    for a_ref in a_refs:
        kp = a_ref.shape[1]
        acc = acc + jnp.dot(a_ref[...], w_ref[k0:k0 + kp, :].astype(BF16), preferred_element_type=F32)
        k0 += kp
    o_ref[...] = acc


def _matmul_res(parts, w, layer, res, *, tm, tn):
    m = res.shape[0]
    k, n = w.shape[1:]
    assert sum(p.shape[1] for p in parts) == k
    return pl.pallas_call(
        functools.partial(_matmul_res_kernel, n_parts=len(parts)),
        grid=(m // tm, n // tn),
        in_specs=[pl.BlockSpec((tm, p.shape[1]), lambda i, j: (i, 0), pipeline_mode=pl.Buffered(1))
                  for p in parts]
                 + [pl.BlockSpec((None, k, tn), lambda i, j: (layer, 0, j)),
                    pl.BlockSpec((tm, tn), lambda i, j: (i, j))],
        out_specs=pl.BlockSpec((tm, tn), lambda i, j: (i, j)),
        out_shape=jax.ShapeDtypeStruct((m, n), F32),
        compiler_params=_cparams("parallel", "arbitrary"),
        name="matmul_res",
    )(*parts, w, res)


def _rms_kernel(x_ref, g_ref, o_ref):
    x = x_ref[...]
    ms = jnp.mean(x * x, axis=-1, keepdims=True)
    o_ref[...] = x * lax.rsqrt(ms + EPS) * g_ref[...]


def _rms_norm(x, g, *, tm):
    m, k = x.shape
    return pl.pallas_call(
        _rms_kernel,
        grid=(m // tm,),
        in_specs=[pl.BlockSpec((tm, k), lambda i: (i, 0)), pl.BlockSpec((1, k), lambda i: (0, 0))],
        out_specs=pl.BlockSpec((tm, k), lambda i: (i, 0)),
        out_shape=jax.ShapeDtypeStruct((m, k), F32),
        compiler_params=_cparams("parallel"),
        name="rms_norm",
    )(x, g.reshape(1, k))


def _c_tab(rel_bias):
    return rel_bias[:, :C_HEADS]


def _c_rev_table(rel_bias, length, tq, max_dist):
    dist = length - np.arange(length + tq)
    valid = (dist >= 0) & (dist <= max_dist)
    tab = _c_tab(rel_bias)[_t5_bucket_np(np.clip(dist, 0, None))]
    tab = jnp.where(jnp.asarray(valid)[:, None], tab, NEG).astype(F32)
    return tab.T.reshape(C_HEADS, 1, length + tq)


def _toeplitz(vec, tq):
    w = vec.shape[1]
    return pltpu.roll(jnp.broadcast_to(vec, (tq, w)), 0, 1, stride=1, stride_axis=0)[:, tq:]


def _tile_attn_kernel(*refs, nh, rep, n_tiles, k_col, v_col, shared_kv, with_lse, lead_axis, toeplitz=False):
    if toeplitz:
        rev_ref, bias_scr = refs[-1 - int(with_lse) - 2], refs[-1]
        tq_ = refs[0].shape[0]

        @pl.when(pl.program_id(lead_axis) == 0)
        def _():
            for h in range(nh):
                bias_scr[h] = _toeplitz(rev_ref[h], tq_)

        refs = list(refs[:-1])
        refs[-1 - int(with_lse) - 1] = bias_scr
    q_ref = refs[0]
    if shared_kv:
        k_refs = v_refs = refs[1:1 + n_tiles]
        nxt = 1 + n_tiles
    else:
        k_refs = refs[1:1 + n_tiles]
        v_refs = refs[1 + n_tiles:1 + 2 * n_tiles]
        nxt = 1 + 2 * n_tiles
    b_ref = refs[nxt]
    o_ref = refs[nxt + 1]
    lse_ref = refs[nxt + 2] if with_lse else None
    tq = q_ref.shape[0]
    tk = k_refs[0].shape[0]
    lk = n_tiles * tk
    if n_tiles > 1:
        qi = pl.program_id(lead_axis)
        col = lax.broadcasted_iota(jnp.int32, (tq, lk), 1)
        pad_mask = jnp.where(col < (n_tiles - 1 - qi) * tk, NEG, 0.0).astype(F32)
    else:
        pad_mask = None
    if with_lse:
        lane = lax.broadcasted_iota(jnp.int32, (tq, LANES), 1)
        lse_t = jnp.zeros((tq, LANES), F32)
    for j in range(nh // rep):
        kc = k_col + j * HEAD_DIM
        vc = v_col + j * HEAD_DIM
        heads = range(j * rep, (j + 1) * rep)
        q = jnp.concatenate([(q_ref[:, h * HEAD_DIM:(h + 1) * HEAD_DIM] * SCALE).astype(BF16) for h in heads], axis=0)
        if n_tiles > 1:
            k = jnp.concatenate([r[:, kc:kc + HEAD_DIM] for r in k_refs], axis=0).astype(BF16)
            v = jnp.concatenate([r[:, vc:vc + HEAD_DIM] for r in v_refs], axis=0).astype(BF16)
        else:
            k = k_refs[0][:, kc:kc + HEAD_DIM].astype(BF16)
            v = v_refs[0][:, vc:vc + HEAD_DIM].astype(BF16)
        s = lax.dot_general(q, k, (((1,), (1,)), ((), ())), preferred_element_type=F32)
        s = s.reshape(rep, tq, lk) + b_ref[j * rep:(j + 1) * rep]
        if pad_mask is not None:
            s = s + pad_mask[None]
        s = s.reshape(rep * tq, lk)
        m = jnp.max(s, axis=-1, keepdims=True)
        e = jnp.exp(s - m)
        den = jnp.sum(e, axis=-1, keepdims=True)
        o = jnp.dot(e.astype(BF16), v, preferred_element_type=F32) / den
        for r, h in enumerate(heads):
            o_ref[:, h * HEAD_DIM:(h + 1) * HEAD_DIM] = o[r * tq:(r + 1) * tq]
        if with_lse:
            lse = m + jnp.log(den)
            for r, h in enumerate(heads):
                lse_t = jnp.where(lane == h, lse[r * tq:(r + 1) * tq], lse_t)
    if with_lse:
        lse_ref[...] = lse_t


A_TU = 128


def _a_rev_table(rel_bias):
    j = 2 * A_TU - np.arange(3 * A_TU)
    rows = []
    for g, (win, dil) in enumerate(A_GROUPS):
        valid = (j >= 0) & (j <= win // dil)
        tab = rel_bias[:, g * A_HPG:(g + 1) * A_HPG][_t5_bucket_np(np.clip(j, 0, None) * dil)]
        rows.append(jnp.where(jnp.asarray(valid)[:, None], tab, NEG).T)
    return jnp.stack(rows, axis=1).astype(F32)


def _a_prompt_kernel(q0_ref, q1_ref, q2_ref, k0_ref, k1_ref, k2_ref, v0_ref, v1_ref, v2_ref, ga_ref, rev_ref,
                     mix_ref, o_scr, l_scr, bias_scr):
    t = q0_ref.shape[0]
    tu = A_TU
    q_refs, k_refs, v_refs = (q0_ref, q1_ref, q2_ref), (k0_ref, k1_ref, k2_ref), (v0_ref, v1_ref, v2_ref)
    for g in range(A_NG):
        bias_scr[g] = _toeplitz(rev_ref[g:g + 1, :], tu)
    col = lax.broadcasted_iota(jnp.int32, (tu, 2 * tu), 1)
    for g, (win, dil) in enumerate(A_GROUPS):
        n_u = t // (dil * tu)
        q_ref, k_ref, v_ref = q_refs[g], k_refs[g], v_refs[g]

        def rows(r, u, dil=dil):
            start = r + u * (tu * dil)
            if dil == 1:
                return pl.ds(pl.multiple_of(start, tu), tu)
            return pl.ds(start, tu, stride=dil)

        def body(it, carry, g=g, n_u=n_u, q_ref=q_ref, k_ref=k_ref, v_ref=v_ref, rows=rows):
            r = it // n_u
            u = it % n_u
            cur = rows(r, u)
            q = (q_ref[cur, :] * SCALE).astype(BF16)
            if n_u > 1:
                prev = rows(r, jnp.maximum(u - 1, 0))
                k = jnp.concatenate([k_ref[prev, :], k_ref[cur, :]], axis=0).astype(BF16)
                v = jnp.concatenate([v_ref[prev, :], v_ref[cur, :]], axis=0).astype(BF16)
                bias = bias_scr[g] + jnp.where((col < tu) & (u == 0), NEG, 0.0)
            else:
                k = k_ref[cur, :].astype(BF16)
                v = v_ref[cur, :].astype(BF16)
                bias = bias_scr[g][:, tu:]
            s = lax.dot_general(q, k, (((1,), (1,)), ((), ())), preferred_element_type=F32) + bias
            m = jnp.max(s, axis=-1, keepdims=True)
            e = jnp.exp(s - m)
            den = jnp.sum(e, axis=-1, keepdims=True)
            o_scr[g, cur, :] = jnp.dot(e.astype(BF16), v, preferred_element_type=F32) / den
            l_scr[g, cur, :] = jnp.broadcast_to(m + jnp.log(den), (tu, HEAD_DIM))
            return carry

        lax.fori_loop(0, t // tu, body, 0, unroll=4)

    def combine(c, carry):
        cur = pl.ds(pl.multiple_of(c * tu, tu), tu)
        ls = [l_scr[g, cur, :] for g in range(A_NG)]
        m = jnp.maximum(jnp.maximum(ls[0], ls[1]), ls[2])
        es = [jnp.exp(l - m) for l in ls]
        acc = es[0] * o_scr[0, cur, :] + es[1] * o_scr[1, cur, :] + es[2] * o_scr[2, cur, :]
        out = acc / (es[0] + es[1] + es[2])
        mix_ref[cur, :] = (out * _silu(ga_ref[cur, :])).astype(BF16)
        return carry

    lax.fori_loop(0, t // tu, combine, 0)


def _a_attn_prompt(z, rel_bias):
    b, t, n = z.shape
    for win, dil in A_GROUPS:
        assert win // dil == A_TU and t % (dil * A_TU) == 0
    rev = _a_rev_table(rel_bias)
    nq = A_QKV // HEAD_DIM
    col = lambda base: pl.BlockSpec((None, t, HEAD_DIM), lambda bi, h: (bi, 0, base + h))
    return pl.pallas_call(
        _a_prompt_kernel,
        grid=(b, A_HPG),
        in_specs=[col(g * A_HPG) for g in range(A_NG)]
                 + [col(nq + g * A_HPG) for g in range(A_NG)]
                 + [col(2 * nq + g * A_HPG) for g in range(A_NG)]
                 + [col(3 * nq), pl.BlockSpec((None, A_NG, 3 * A_TU), lambda bi, h: (h, 0, 0))],
        out_specs=pl.BlockSpec((None, t, HEAD_DIM), lambda bi, h: (bi, 0, h)),
        out_shape=jax.ShapeDtypeStruct((b, t, A_WIDTH), BF16),
        scratch_shapes=[pltpu.VMEM((A_NG, t, HEAD_DIM), F32), pltpu.VMEM((A_NG, t, HEAD_DIM), F32),
                        pltpu.VMEM((A_NG, A_TU, 2 * A_TU), F32)],
        compiler_params=_cparams("parallel", "parallel"),
        name="a_attn_prompt",
    )(*([z] * 10), rev)


def _a_bias_sample(rel_bias, g, hist_len, tq):
    win, dil = A_GROUPS[g]
    lk = hist_len + tq
    qidx = hist_len + np.arange(tq)[:, None]
    kidx = np.arange(lk)[None, :]
    dist = qidx - kidx
    valid = (dist >= 0) & (dist <= win) & (dist % dil == 0)
    bucket = _t5_bucket_np(np.clip(dist, 0, None))
    tab = rel_bias[:, g * A_HPG:(g + 1) * A_HPG]
    bias = jnp.transpose(tab[bucket], (2, 0, 1))
    return jnp.where(jnp.asarray(valid)[None], bias, NEG).astype(F32)


A_ROWS = 2 * A_HPG
A_CHUNK = 512


def _a_sample_kernel(*refs, aliased):
    q_ref, kn_ref, vn_ref, old_ref, head_ref, bias_ref, bnew_ref = refs[:7]
    o_ref, lse_ref, new_ref, m_ref, l_ref, acc_ref = refs[7 + int(aliased):]
    c = pl.program_id(1)
    last = pl.num_programs(1) - 1
    tq = q_ref.shape[0]
    n_tok = old_ref.shape[0] // A_ROWS
    shift = tq * A_ROWS

    @pl.when(c == 0)
    def _():
        m_ref[...] = jnp.full(m_ref.shape, NEG, F32)
        l_ref[...] = jnp.zeros(l_ref.shape, F32)
        acc_ref[...] = jnp.zeros(acc_ref.shape, F32)

    def update(h, q, k, v, bias):
        s = lax.dot_general(q, k, (((1,), (1,)), ((), ())), preferred_element_type=F32) + bias
        m_old = m_ref[h]
        m_new = jnp.maximum(m_old, jnp.max(s, axis=-1, keepdims=True))
        alpha = jnp.exp(m_old - m_new)
        e = jnp.exp(s - m_new)
        m_ref[h] = m_new
        l_ref[h] = alpha * l_ref[h] + jnp.sum(e, axis=-1, keepdims=True)
        acc_ref[h] = alpha * acc_ref[h] + jnp.dot(e.astype(BF16), v, preferred_element_type=F32)

    qs = [(q_ref[:, h * HEAD_DIM:(h + 1) * HEAD_DIM] * SCALE).astype(BF16) for h in range(A_HPG)]
    for h in range(A_HPG):
        k = old_ref[pl.ds(h, n_tok, stride=A_ROWS), :].astype(BF16)
        v = old_ref[pl.ds(A_HPG + h, n_tok, stride=A_ROWS), :].astype(BF16)
        update(h, qs[h], k, v, bias_ref[h])

    new_ref[0:n_tok * A_ROWS - shift, :] = old_ref[shift:, :]

    @pl.when(c < last)
    def _():
        new_ref[n_tok * A_ROWS - shift:, :] = head_ref[...]

    @pl.when(c == last)
    def _():
        lane = lax.broadcasted_iota(jnp.int32, (tq, LANES), 1)
        lse_t = jnp.zeros((tq, LANES), F32)
        for h in range(A_HPG):
            cs = slice(h * HEAD_DIM, (h + 1) * HEAD_DIM)
            update(h, qs[h], kn_ref[:, cs].astype(BF16), vn_ref[:, cs].astype(BF16), bnew_ref[h])
            o_ref[:, cs] = acc_ref[h] / l_ref[h]
            lse_t = jnp.where(lane == h, m_ref[h] + jnp.log(l_ref[h]), lse_t)
            base = n_tok * A_ROWS - shift
            new_ref[pl.ds(base + h, tq, stride=A_ROWS), :] = kn_ref[:, cs]
            new_ref[pl.ds(base + A_HPG + h, tq, stride=A_ROWS), :] = vn_ref[:, cs]
        lse_ref[...] = lse_t


def _a_attn_sample(z, cache, layer, new_cache, rel_bias, g):
    b, tq, n = z.shape
    n_layers, _, w = cache.shape[:3]
    assert cache.shape[3:] == (2, A_HPG, HEAD_DIM) and w % LANES == 0 and tq * A_ROWS == LANES
    ch = min(w, A_CHUNK)
    n_ch = w // ch
    old = cache.reshape(n_layers, b, w * A_ROWS, HEAD_DIM)
    bias_full = _a_bias_sample(rel_bias, g, w, tq)
    bias = jnp.stack([bias_full[:, :, i * ch:(i + 1) * ch] for i in range(n_ch)])
    bnew = bias_full[:, :, w:]
    nq = A_QKV // A_WIDTH
    zcol = lambda cb: pl.BlockSpec((None, tq, A_WIDTH), lambda bi, c: (bi, 0, cb))
    heads_per_chunk = ch * A_ROWS // LANES
    in_specs = [zcol(g), zcol(nq + g), zcol(2 * nq + g),
                pl.BlockSpec((None, None, ch * A_ROWS, HEAD_DIM), lambda bi, c: (layer, bi, c, 0)),
                pl.BlockSpec((None, None, LANES, HEAD_DIM),
                             lambda bi, c: (layer, bi, jnp.minimum(c + 1, n_ch - 1) * heads_per_chunk, 0)),
                pl.BlockSpec((None, A_HPG, tq, ch), lambda bi, c: (c, 0, 0, 0)),
                pl.BlockSpec((A_HPG, tq, tq), lambda bi, c: (0, 0, 0))]
    args = [z, z, z, old, old, bias, bnew]
    aliases = {}
    if new_cache is not None:
        in_specs.append(pl.BlockSpec(memory_space=pl.ANY))
        args.append(new_cache)
        aliases = {len(args) - 1: 2}
    o, lse, new = pl.pallas_call(
        functools.partial(_a_sample_kernel, aliased=new_cache is not None),
        grid=(b, n_ch),
        in_specs=in_specs,
        out_specs=[pl.BlockSpec((None, tq, A_WIDTH), lambda bi, c: (bi, 0, 0)),
                   pl.BlockSpec((None, tq, LANES), lambda bi, c: (bi, 0, 0)),
                   pl.BlockSpec((None, None, ch * A_ROWS, HEAD_DIM), lambda bi, c: (layer, bi, c, 0))],
        out_shape=[jax.ShapeDtypeStruct((b, tq, A_WIDTH), F32),
                   jax.ShapeDtypeStruct((b, tq, LANES), F32),
                   jax.ShapeDtypeStruct(old.shape, F32)],
        scratch_shapes=[pltpu.VMEM((A_HPG, tq, 1), F32), pltpu.VMEM((A_HPG, tq, 1), F32),
                        pltpu.VMEM((A_HPG, tq, HEAD_DIM), F32)],
        input_output_aliases=aliases,
        compiler_params=_cparams("parallel", "arbitrary"),
        name=f"a_attn_sample_g{g}",
    )(*args)
    return o, lse, new


def _conv_kernel(a_ref, gt_ref, gb_ref, hist_ref, cw_ref, cb_ref, lg_ref, lb_ref, pw_ref, pb_ref,
                 o_ref, nc_ref, ubuf, ybuf, pwb):
    ti = pl.program_id(1)
    nt = pl.num_programs(1)
    tt = a_ref.shape[0]
    nhist = CONV_W - 1
    pad = CONV_HALO - nhist

    @pl.when(ti == 0)
    def _():
        pwb[...] = pw_ref[...].astype(BF16)
        ubuf[0:pad, :] = jnp.zeros((pad, B_WIDTH), F32)
        ubuf[pad:CONV_HALO, :] = hist_ref[...]

    ubuf[CONV_HALO:CONV_HALO + tt, :] = a_ref[...] * jax.nn.sigmoid(gt_ref[...])
    for c in range(B_WIDTH // LANES):
        cs = slice(c * LANES, (c + 1) * LANES)
        acc = jnp.zeros((tt, LANES), F32) + cb_ref[:, cs]
        for k in range(CONV_W):
            acc = acc + ubuf[pad + k:pad + k + tt, cs] * cw_ref[k:k + 1, cs]
        ybuf[:, cs] = acc
    y = ybuf[...]
    mu = jnp.mean(y, axis=-1, keepdims=True)
    yc = y - mu
    var = jnp.mean(yc * yc, axis=-1, keepdims=True)
    yn = yc * lax.rsqrt(var + EPS) * lg_ref[...] + lb_ref[...]
    act = _silu(yn).astype(BF16)
    ob = jnp.dot(act, pwb[...], preferred_element_type=F32) + pb_ref[...]
    o_ref[...] = (ob * _silu(gb_ref[...])).astype(BF16)

    @pl.when(ti == nt - 1)
    def _():
        nc_ref[...] = ubuf[tt + pad:tt + CONV_HALO, :]

    @pl.when(ti < nt - 1)
    def _():
        ubuf[0:CONV_HALO, :] = ubuf[tt:tt + CONV_HALO, :]


def _conv_module(z, hist, layer, conv_w, conv_b, ln_g, ln_b, pw_w, pw_b, *, tt):
    b, t, n = z.shape
    assert t % tt == 0 and (tt >= CONV_HALO or t == tt)
    glu0 = (3 * A_QKV + A_WIDTH) // B_WIDTH
    row = lambda a: a.reshape(a.shape[0], 1, B_WIDTH)
    blk = (None, tt, B_WIDTH)
    lyr = lambda shape: pl.BlockSpec((None,) + shape, lambda bi, ti: (layer,) + (0,) * len(shape))
    return pl.pallas_call(
        _conv_kernel,
        grid=(b, t // tt),
        in_specs=[pl.BlockSpec(blk, lambda bi, ti: (bi, ti, glu0)),
                  pl.BlockSpec(blk, lambda bi, ti: (bi, ti, glu0 + 1)),
                  pl.BlockSpec(blk, lambda bi, ti: (bi, ti, glu0 + 2)),
                  pl.BlockSpec((None, CONV_W - 1, B_WIDTH), lambda bi, ti: (bi, 0, 0)),
                  lyr((CONV_W, B_WIDTH)), lyr((1, B_WIDTH)), lyr((1, B_WIDTH)), lyr((1, B_WIDTH)),
                  lyr((B_WIDTH, B_WIDTH)), lyr((1, B_WIDTH))],
        out_specs=[pl.BlockSpec(blk, lambda bi, ti: (bi, ti, 0)),
                   pl.BlockSpec((None, CONV_W - 1, B_WIDTH), lambda bi, ti: (bi, 0, 0))],
        out_shape=[jax.ShapeDtypeStruct((b, t, B_WIDTH), BF16),
                   jax.ShapeDtypeStruct((b, CONV_W - 1, B_WIDTH), F32)],
        scratch_shapes=[pltpu.VMEM((CONV_HALO + tt, B_WIDTH), F32),
                        pltpu.VMEM((tt, B_WIDTH), F32),
                        pltpu.VMEM((B_WIDTH, B_WIDTH), BF16)],
        compiler_params=_cparams("parallel", "arbitrary"),
        name="conv_module",
    )(z, z, z, hist, conv_w, row(conv_b), row(ln_g), row(ln_b), pw_w, row(pw_b))


def _even_mix_kernel(o0_ref, o1_ref, o2_ref, l0_ref, l1_ref, l2_ref, ga_ref, mix_ref):
    ls = [l0_ref[...], l1_ref[...], l2_ref[...]]
    m = jnp.maximum(jnp.maximum(ls[0], ls[1]), ls[2])
    es = [jnp.exp(l - m) for l in ls]
    inv = 1.0 / (es[0] + es[1] + es[2])
    ws = [e * inv for e in es]
    o_refs = (o0_ref, o1_ref, o2_ref)
    for h in range(A_HPG):
        cs = slice(h * HEAD_DIM, (h + 1) * HEAD_DIM)
        acc = ws[0][:, h:h + 1] * o_refs[0][:, cs]
        for g in range(1, A_NG):
            acc = acc + ws[g][:, h:h + 1] * o_refs[g][:, cs]
        mix_ref[:, cs] = (acc * _silu(ga_ref[:, cs])).astype(BF16)


def _even_mix(oas, lses, z2, *, tm):
    m = z2.shape[0]
    ga_blk = 3 * A_QKV // A_WIDTH
    wide = lambda c: pl.BlockSpec((tm, A_WIDTH), lambda i: (i, c))
    narrow = pl.BlockSpec((tm, LANES), lambda i: (i, 0))
    return pl.pallas_call(
        _even_mix_kernel,
        grid=(m // tm,),
        in_specs=[wide(0), wide(0), wide(0), narrow, narrow, narrow, wide(ga_blk)],
        out_specs=wide(0),
        out_shape=jax.ShapeDtypeStruct((m, A_WIDTH), BF16),
        compiler_params=_cparams("parallel"),
        name="even_mix",
    )(*oas, *lses, z2)


def _even_layer(x, a_caches, a_new, conv_hist, rel_bias, layer, norm_g, w_in, conv_w, conv_b, ln_g, ln_b, pw_w,
                pw_b, w_out, *, tm, tn_in, tn_out, tt):
    b, t, d = x.shape
    m = b * t
    x2 = x.reshape(m, d)
    z2 = _norm_matmul(x2, norm_g[layer], w_in, layer, tm=tm, tn=tn_in)
    z = z2.reshape(b, t, EVEN_IN)
    oas, lses, new_hists = [], [], []
    for g, (win, _) in enumerate(A_GROUPS):
        if a_caches is None:
            kcol = A_QKV + g * A_WIDTH
            vcol = 2 * A_QKV + g * A_WIDTH
            new_kv = jnp.stack([z[:, :, kcol:kcol + A_WIDTH], z[:, :, vcol:vcol + A_WIDTH]], axis=2)
            new_hists.append(new_kv[:, t - min(win, t):].reshape(b, min(win, t), 2, A_HPG, HEAD_DIM))
        else:
            assert a_caches[g].shape[2] == win
            o, lse, new = _a_attn_sample(z, a_caches[g], layer, None if a_new is None else a_new[g], rel_bias, g)
            new_hists.append(new)
            oas.append(o.reshape(m, A_WIDTH))
            lses.append(lse.reshape(m, LANES))
    if a_caches is None:
        mix_a = _a_attn_prompt(z, rel_bias).reshape(m, A_WIDTH)
    else:
        mix_a = _even_mix(oas, lses, z2, tm=min(tm, 256))
    mix_b, new_conv = _conv_module(z, conv_hist, layer, conv_w, conv_b, ln_g, ln_b, pw_w, pw_b, tt=tt)
    y = _matmul_res([mix_a, mix_b.reshape(m, B_WIDTH)], w_out, layer, x2, tm=tm, tn=tn_out)
    return y.reshape(b, t, d), new_hists, new_conv


ODD_KV0 = C_WIDTH
ODD_GL0 = C_WIDTH + 6 * C_KV
ODD_GP0 = ODD_GL0 + 3 * C_HEADS
ODD_Z = ODD_GL0 + 512
CMP_HALF = CMP_BLK // 2
CMP_K = CMP_HALF * HEAD_DIM
PAGES_PER_STEP = 16
CACHE_HALVES = 2
CACHE_KINDS = 2 * C_KV_HEADS


def _cmp_partials_kernel(*refs, n_src, n_prefetch, interleaved):
    refs = refs[n_prefetch:]
    src = refs[:n_src]
    pos_ref, w1_ref, a_ref, b_ref, rows_ref = refs[n_src:n_src + 5]
    n = rows_ref.shape[1] // CMP_HALF
    r0 = 0
    for r in src:
        nr = r.shape[0]
        for c in range(CACHE_KINDS):
            if interleaved:
                rows_ref[c, r0:r0 + nr, :] = r[:, c, :]
            else:
                rows_ref[c, r0:r0 + nr, :] = r[:, c * HEAD_DIM:(c + 1) * HEAD_DIM]
        r0 += nr
    for kv in range(2):
        x = jnp.concatenate(
            [jnp.concatenate([rows_ref[kv * C_KV_HEADS + g, pl.ds(l, n, stride=CMP_HALF), :]
                              for l in range(CMP_HALF)], axis=1) for g in range(C_KV_HEADS)], axis=0)
        for half, out in ((0, a_ref), (1, b_ref)):
            y = jnp.dot((x + pos_ref[kv, half]).astype(BF16), w1_ref[kv, half].astype(BF16),
                        preferred_element_type=F32)
            for g in range(C_KV_HEADS):
                out[kv * C_KV_HEADS + g] = y[g * n:(g + 1) * n]


def _cmp_weights(cmp_pos, cmp_w1):
    n = cmp_pos.shape[0]
    return cmp_pos.reshape(n, 2, 2, 1, CMP_K), cmp_w1.reshape(n, 2, 2, CMP_K, CMP_HID)


def _cmp_partials_prompt(z, layer, cmp_pos, cmp_w1):
    b, t, _ = z.shape
    nch = t // CMP_STRIDE
    pos, w1 = _cmp_weights(cmp_pos, cmp_w1)
    kern = functools.partial(_cmp_partials_kernel, n_src=1, n_prefetch=0, interleaved=False)
    out = jax.ShapeDtypeStruct((b, 4, nch, CMP_HID), F32)
    ospec = pl.BlockSpec((None, 4, nch, CMP_HID), lambda bi: (bi, 0, 0, 0))
    return pl.pallas_call(
        kern,
        grid=(b,),
        in_specs=[pl.BlockSpec((None, t, 2 * C_KV), lambda bi: (bi, 0, ODD_KV0 // (2 * C_KV))),
                  pl.BlockSpec((None,) + pos.shape[1:], lambda bi: (layer, 0, 0, 0, 0)),
                  pl.BlockSpec((None,) + w1.shape[1:], lambda bi: (layer, 0, 0, 0, 0))],
        out_specs=[ospec, ospec],
        out_shape=[out, out],
        scratch_shapes=[pltpu.VMEM((2 * C_KV_HEADS, t, HEAD_DIM), F32)],
        compiler_params=_cparams("parallel"),
        name="cmp_partials_prompt",
    )(z, pos, w1)


def _cmp_partials_paged(cache, layer, page_table, cmp_pos, cmp_w1):
    b, n_pages = page_table.shape
    page = cache.shape[2]
    pps = PAGES_PER_STEP
    assert n_pages % pps == 0 and page % CMP_STRIDE == 0
    nch_step = pps * page // CMP_STRIDE
    nch = n_pages * page // CMP_STRIDE
    pos, w1 = _cmp_weights(cmp_pos, cmp_w1)
    kern = functools.partial(_cmp_partials_kernel, n_src=pps, n_prefetch=1, interleaved=True)
    out = jax.ShapeDtypeStruct((b, 4, nch, CMP_HID), F32)
    ospec = pl.BlockSpec((None, 4, nch_step, CMP_HID), lambda bi, pg, pt: (bi, 0, pg, 0))

    def page_spec(j):
        return pl.BlockSpec((None, None, page, None, CACHE_KINDS, HEAD_DIM),
                            lambda bi, pg, pt: (layer, pt[bi, pg * pps + j], 0, 0, 0, 0))

    grid_spec = pltpu.PrefetchScalarGridSpec(
        num_scalar_prefetch=1,
        grid=(b, n_pages // pps),
        in_specs=[page_spec(j) for j in range(pps)] + [
            pl.BlockSpec((None,) + pos.shape[1:], lambda bi, pg, pt: (layer, 0, 0, 0, 0)),
            pl.BlockSpec((None,) + w1.shape[1:], lambda bi, pg, pt: (layer, 0, 0, 0, 0))],
        out_specs=[ospec, ospec],
        scratch_shapes=[pltpu.VMEM((2 * C_KV_HEADS, pps * page, HEAD_DIM), F32)])
    return pl.pallas_call(
        kern, grid_spec=grid_spec, out_shape=[out, out],
        compiler_params=_cparams("parallel", "arbitrary"),
        name="cmp_partials_paged",
    )(page_table, *([cache] * pps), pos, w1)


def _overlap_np(n_ch, n_sel, ns_pad):
    ci = np.arange(n_ch)[:, None]
    si = np.arange(ns_pad)[None, :]
    ov = (ci * CMP_STRIDE < (si + 1) * SEL_BLK) & (ci * CMP_STRIDE + CMP_BLK > si * SEL_BLK)
    ov &= (ci < n_ch - 1) & (si < n_sel)
    return ov.astype(np.float32)


def _cmp_select_kernel(q_ref, ak_ref, bk_ref, av_ref, bv_ref, b1_ref, w2_ref, ov_ref, oc_ref, sel_ref,
                       *, qpos0, n_sel):
    qi = pl.program_id(2)
    tq = q_ref.shape[0]
    nch = ak_ref.shape[0]
    ns_pad = ov_ref.shape[1]

    def finish(a_ref, b_ref, kv):
        hid = a_ref[...] + pltpu.roll(b_ref[...], nch - 1, 0) + b1_ref[kv:kv + 1, :]
        return jnp.dot(_silu(hid).astype(BF16), w2_ref[kv].astype(BF16),
                       preferred_element_type=F32).astype(BF16)

    k_cmp = finish(ak_ref, bk_ref, 0)
    v_cmp = finish(av_ref, bv_ref, 1)
    qpos = qpos0 + qi * tq + lax.broadcasted_iota(jnp.int32, (tq, nch), 0)
    cmp_end = lax.broadcasted_iota(jnp.int32, (tq, nch), 1) * CMP_STRIDE + (CMP_BLK - 1)
    ok = (cmp_end <= qpos) & (cmp_end < (nch - 1) * CMP_STRIDE + CMP_BLK - 1)
    q = jnp.concatenate([(q_ref[:, h * HEAD_DIM:(h + 1) * HEAD_DIM] * SCALE).astype(BF16) for h in range(C_HPG)],
                        axis=0)
    s = lax.dot_general(q, k_cmp, (((1,), (1,)), ((), ())), preferred_element_type=F32)
    s = jnp.where(ok[None], s.reshape(C_HPG, tq, nch), NEG)
    m = jnp.max(s, axis=-1, keepdims=True)
    e = jnp.where(ok[None], jnp.exp(s - m), 0.0)
    den = jnp.sum(e, axis=-1, keepdims=True)
    p = e / jnp.where(den > 0.0, den, 1.0)
    psum = jnp.sum(p, axis=0)
    oc = jnp.dot(p.reshape(C_HPG * tq, nch).astype(BF16), v_cmp, preferred_element_type=F32)
    for h in range(C_HPG):
        oc_ref[:, h * HEAD_DIM:(h + 1) * HEAD_DIM] = oc[h * tq:(h + 1) * tq]
    imp = jnp.dot(psum, ov_ref[...], preferred_element_type=F32, precision=lax.Precision.HIGHEST)
    blk = lax.broadcasted_iota(jnp.int32, (tq, ns_pad), 1)
    qblk = (qpos0 + qi * tq + lax.broadcasted_iota(jnp.int32, (tq, ns_pad), 0)) // SEL_BLK
    forced = (blk == 0) | (blk == qblk) | (blk == qblk - 1)
    allowed = (blk <= qblk) & (blk < n_sel)
    score = jnp.where(allowed, jnp.where(forced, -NEG, imp), -1.0)
    rank = jnp.zeros((tq, ns_pad), jnp.int32)
    for j in range(n_sel):
        cj = score[:, j:j + 1]
        before = (cj > score) | ((cj == score) & (j < blk))
        rank = rank + before.astype(jnp.int32)
    sel_ref[...] = (allowed & (rank < SEL_N)).astype(F32)


def _cmp_select(z, a, bm, layer, cmp_b1, cmp_w2, *, tq, qpos0, n_sel):
    b, t, _ = z.shape
    nch = a.shape[2]
    ns_pad = -(-n_sel // LANES) * LANES
    ov = jnp.asarray(_overlap_np(nch, n_sel, ns_pad))
    kern = functools.partial(_cmp_select_kernel, qpos0=qpos0, n_sel=n_sel)
    part = lambda kv: pl.BlockSpec((None, None, nch, CMP_HID), lambda bi, g, qi: (bi, kv * C_KV_HEADS + g, 0, 0))
    return pl.pallas_call(
        kern,
        grid=(b, C_KV_HEADS, t // tq),
        in_specs=[pl.BlockSpec((None, tq, C_HPG * HEAD_DIM), lambda bi, g, qi: (bi, qi, g)),
                  part(0), part(0), part(1), part(1),
                  pl.BlockSpec((None, 2, CMP_HID), lambda bi, g, qi: (layer, 0, 0)),
                  pl.BlockSpec((None, 2, CMP_HID, HEAD_DIM), lambda bi, g, qi: (layer, 0, 0, 0)),
                  pl.BlockSpec((nch, ns_pad), lambda bi, g, qi: (0, 0))],
        out_specs=[pl.BlockSpec((None, tq, C_HPG * HEAD_DIM), lambda bi, g, qi: (bi, qi, g)),
                   pl.BlockSpec((None, None, tq, ns_pad), lambda bi, g, qi: (bi, g, qi, 0))],
        out_shape=[jax.ShapeDtypeStruct((b, t, C_WIDTH), F32),
                   jax.ShapeDtypeStruct((b, C_KV_HEADS, t, ns_pad), F32)],
        compiler_params=_cparams("parallel", "parallel", "arbitrary"),
        name="cmp_select",
    )(z, a, bm, a, bm, cmp_b1, cmp_w2, ov)


def _sel_prompt_kernel(q_ref, k_ref, v_ref, rev_ref, sel_ref, o_ref,
                       bias_ref, msk_ref, qb_ref, m_ref, l_ref, acc_ref):
    qi = pl.program_id(2)
    tq = q_ref.shape[0]
    nt = bias_ref.shape[0] - 1
    tk = 2 * tq
    ns_pad = sel_ref.shape[1]

    @pl.when(qi == 0)
    def _():
        bias_ref[0] = jnp.full((C_HPG, tq, tq), NEG, F32)
        for h in range(C_HPG):
            for delta in range(nt):
                off = (nt - 1 - delta) * tq
                bias_ref[delta + 1, h] = _toeplitz(rev_ref[h, :, off:off + 2 * tq], tq).T * LOG2E

    sel_t = sel_ref[...].T.astype(BF16)
    erow = lax.broadcasted_iota(jnp.int32, (tk, ns_pad), 0) // SEL_BLK
    ecol = lax.broadcasted_iota(jnp.int32, (tk, ns_pad), 1)
    for t in range(nt // 2):
        expand = (ecol == erow + t * (tk // SEL_BLK)).astype(BF16)
        hit = jnp.dot(expand, sel_t, preferred_element_type=F32)
        msk_ref[t] = jnp.where(hit > 0.5, 0.0, NEG)
    for h in range(C_HPG):
        qb_ref[h] = (q_ref[:, h * HEAD_DIM:(h + 1) * HEAD_DIM] * (SCALE * LOG2E)).astype(BF16)
    m_ref[...] = jnp.full(m_ref.shape, NEG, F32)
    l_ref[...] = jnp.zeros(l_ref.shape, F32)
    acc_ref[...] = jnp.zeros(acc_ref.shape, F32)

    def body(kj, carry):
        keys = pl.ds(pl.multiple_of(kj * tk, tk), tk)
        k = k_ref[keys, :].astype(BF16)
        v_t = v_ref[keys, :].T.astype(BF16)
        msk = msk_ref[kj]
        d0 = qi - 2 * kj
        for h in range(C_HPG):
            bias = jnp.concatenate([bias_ref[d0 + 1, h], bias_ref[d0, h]], axis=0)
            s = lax.dot_general(k, qb_ref[h], (((1,), (1,)), ((), ())), preferred_element_type=F32)
            s = s + bias + msk
            m_old = m_ref[h:h + 1, :]
            m_new = jnp.maximum(m_old, jnp.max(s, axis=0, keepdims=True))
            alpha = jnp.exp2(m_old - m_new)
            e = jnp.exp2(s - m_new)
            m_ref[h:h + 1, :] = m_new
            l_ref[h:h + 1, :] = alpha * l_ref[h:h + 1, :] + jnp.sum(e, axis=0, keepdims=True)
            acc_ref[h] = alpha * acc_ref[h] + jnp.dot(v_t, e.astype(BF16), preferred_element_type=F32)
        return carry

    lax.fori_loop(0, (qi + 2) // 2, body, 0)
    for h in range(C_HPG):
        o_ref[:, h * HEAD_DIM:(h + 1) * HEAD_DIM] = (acc_ref[h] / l_ref[h:h + 1, :]).T


def _sel_attn_prompt(z, sel, rel_bias, *, tq):
    b, t, _ = z.shape
    nt = t // tq
    assert nt % 2 == 0
    ns_pad = sel.shape[3]
    rev = _c_rev_table(rel_bias, t, tq, t)
    kcol = (ODD_KV0 + 2 * C_KV) // HEAD_DIM
    vcol = (ODD_KV0 + 3 * C_KV) // HEAD_DIM
    return pl.pallas_call(
        _sel_prompt_kernel,
        grid=(b, C_KV_HEADS, nt),
        in_specs=[pl.BlockSpec((None, tq, C_HPG * HEAD_DIM), lambda bi, g, qi: (bi, qi, g)),
                  pl.BlockSpec((None, t, HEAD_DIM), lambda bi, g, qi: (bi, 0, kcol + g)),
                  pl.BlockSpec((None, t, HEAD_DIM), lambda bi, g, qi: (bi, 0, vcol + g)),
                  pl.BlockSpec((C_HPG, 1, t + tq), lambda bi, g, qi: (g, 0, 0)),
                  pl.BlockSpec((None, None, tq, ns_pad), lambda bi, g, qi: (bi, g, qi, 0))],
        out_specs=pl.BlockSpec((None, tq, C_HPG * HEAD_DIM), lambda bi, g, qi: (bi, qi, g)),
        out_shape=jax.ShapeDtypeStruct((b, t, C_WIDTH), F32),
        scratch_shapes=[pltpu.VMEM((nt + 1, C_HPG, tq, tq), F32),
                        pltpu.VMEM((nt // 2, 2 * tq, tq), F32),
                        pltpu.VMEM((C_HPG, tq, HEAD_DIM), BF16),
                        pltpu.VMEM((C_HPG, tq), F32),
                        pltpu.VMEM((C_HPG, tq), F32),
                        pltpu.VMEM((C_HPG, HEAD_DIM, tq), F32)],
        compiler_params=_cparams("parallel", "parallel", "arbitrary"),
        name="sel_attn_prompt",
    )(z, z, z, rev, sel)


def _sel_paged_kernel(pt_ref, *refs, n_src, past):
    del pt_ref
    pages = refs[:n_src]
    q_ref, new_ref, sel_ref, bias_ref, bnew_ref, o_ref, m_ref, l_ref, acc_ref = refs[n_src:]
    pg = pl.program_id(1)
    tq = q_ref.shape[0]
    rows = C_HPG * tq
    page = pages[0].shape[0]
    keys = n_src * page
    ns_pad = sel_ref.shape[2]

    def page_rows(r, kind):
        return r[:, kind, :]

    def q_rows(g):
        return jnp.concatenate(
            [(q_ref[:, (g * C_HPG + h) * HEAD_DIM:(g * C_HPG + h + 1) * HEAD_DIM] * SCALE).astype(BF16)
             for h in range(C_HPG)], axis=0)

    @pl.when(pg == 0)
    def _():
        for g in range(C_KV_HEADS):
            k = new_ref[:, g * HEAD_DIM:(g + 1) * HEAD_DIM].astype(BF16)
            v = new_ref[:, C_KV + g * HEAD_DIM:C_KV + (g + 1) * HEAD_DIM].astype(BF16)
            s = lax.dot_general(q_rows(g), k, (((1,), (1,)), ((), ())), preferred_element_type=F32)
            s = s + bnew_ref[g * C_HPG:(g + 1) * C_HPG].reshape(rows, tq)
            m = jnp.max(s, axis=-1, keepdims=True)
            e = jnp.exp(s - m)
            m_ref[g] = m
            l_ref[g] = jnp.sum(e, axis=-1, keepdims=True)
            acc_ref[g] = jnp.dot(e.astype(BF16), v, preferred_element_type=F32)

    srow = lax.broadcasted_iota(jnp.int32, (ns_pad, keys), 0)
    scol = lax.broadcasted_iota(jnp.int32, (ns_pad, keys), 1) // SEL_BLK + pg * (keys // SEL_BLK)
    expand = (srow == scol).astype(BF16)
    for g in range(C_KV_HEADS):
        k = jnp.concatenate([page_rows(r, g) for r in pages], axis=0).astype(BF16)
        v = jnp.concatenate([page_rows(r, C_KV_HEADS + g) for r in pages], axis=0).astype(BF16)
        hit = jnp.dot(sel_ref[g].astype(BF16), expand, preferred_element_type=F32)
        mask = jnp.where(hit > 0.5, 0.0, NEG)
        s = lax.dot_general(q_rows(g), k, (((1,), (1,)), ((), ())), preferred_element_type=F32)
        s = (s.reshape(C_HPG, tq, keys) + bias_ref[g * C_HPG:(g + 1) * C_HPG] + mask[None]).reshape(rows, keys)
        m_old = m_ref[g]
        m_new = jnp.maximum(m_old, jnp.max(s, axis=-1, keepdims=True))
        alpha = jnp.exp(m_old - m_new)
        e = jnp.exp(s - m_new)
        m_ref[g] = m_new
        l_ref[g] = alpha * l_ref[g] + jnp.sum(e, axis=-1, keepdims=True)
        acc_ref[g] = alpha * acc_ref[g] + jnp.dot(e.astype(BF16), v, preferred_element_type=F32)

    @pl.when(pg == pl.num_programs(1) - 1)
    def _():
        for g in range(C_KV_HEADS):
            o = acc_ref[g] / l_ref[g]
            for h in range(C_HPG):
                o_ref[:, (g * C_HPG + h) * HEAD_DIM:(g * C_HPG + h + 1) * HEAD_DIM] = o[h * tq:(h + 1) * tq]


def _sel_attn_paged(z, sel, cache, layer, page_table, rel_bias):
    b, tq, _ = z.shape
    n_pages = page_table.shape[1]
    page = cache.shape[2]
    past = n_pages * page
    pps = PAGES_PER_STEP
    keys = pps * page
    ns_pad = sel.shape[3]
    tab = _c_tab(rel_bias)
    rev = tab[_t5_bucket_np(past + tq - 1 - np.arange(past + tq - 1))].T.astype(F32)
    bias = jnp.stack([rev[:, tq - 1 - t:tq - 1 - t + past] for t in range(tq)], axis=1)
    dnew = np.arange(tq)[:, None] - np.arange(tq)[None, :]
    bnew = jnp.where(jnp.asarray(dnew >= 0)[None],
                     jnp.transpose(tab[_t5_bucket_np(np.clip(dnew, 0, None))], (2, 0, 1)), NEG).astype(F32)
    kern = functools.partial(_sel_paged_kernel, n_src=pps, past=past)

    def page_spec(j):
        return pl.BlockSpec((None, None, page, None, CACHE_KINDS, HEAD_DIM),
                            lambda bi, pg, pt: (layer, pt[bi, pg * pps + j], 0, 1, 0, 0))

    rows = C_HPG * tq
    grid_spec = pltpu.PrefetchScalarGridSpec(
        num_scalar_prefetch=1,
        grid=(b, n_pages // pps),
        in_specs=[page_spec(j) for j in range(pps)] + [
            pl.BlockSpec((None, tq, C_WIDTH), lambda bi, pg, pt: (bi, 0, 0)),
            pl.BlockSpec((None, tq, 2 * C_KV), lambda bi, pg, pt: (bi, 0, (ODD_KV0 + 2 * C_KV) // (2 * C_KV))),
            pl.BlockSpec((None, C_KV_HEADS, tq, ns_pad), lambda bi, pg, pt: (bi, 0, 0, 0)),
            pl.BlockSpec((C_HEADS, tq, keys), lambda bi, pg, pt: (0, 0, pg)),
            pl.BlockSpec((C_HEADS, tq, tq), lambda bi, pg, pt: (0, 0, 0))],
        out_specs=pl.BlockSpec((None, tq, C_WIDTH), lambda bi, pg, pt: (bi, 0, 0)),
        scratch_shapes=[pltpu.VMEM((C_KV_HEADS, rows, 1), F32), pltpu.VMEM((C_KV_HEADS, rows, 1), F32),
                        pltpu.VMEM((C_KV_HEADS, rows, HEAD_DIM), F32)])
    return pl.pallas_call(
        kern, grid_spec=grid_spec, out_shape=jax.ShapeDtypeStruct((b, tq, C_WIDTH), F32),
        compiler_params=_cparams("parallel", "arbitrary"),
        name="sel_attn_paged",
    )(page_table, *([cache] * pps), z, z, sel, bias, bnew)


def _win_attn_prompt(z, rel_bias, *, tq):
    b, t, _ = z.shape
    n_tiles = WIN // tq + 1
    rev = _c_rev_table(rel_bias, n_tiles * tq, tq, WIN)
    kv_blk = (ODD_KV0 + 4 * C_KV) // (2 * C_KV)
    kern = functools.partial(_tile_attn_kernel, nh=C_HEADS, rep=C_HPG, n_tiles=n_tiles, k_col=0, v_col=C_KV,
                             shared_kv=True, with_lse=False, lead_axis=1, toeplitz=True)

    def kv_spec(p):
        return pl.BlockSpec((None, tq, 2 * C_KV),
                            lambda bi, qi: (bi, jnp.maximum(qi - (n_tiles - 1 - p), 0), kv_blk))

    return pl.pallas_call(
        kern,
        grid=(b, t // tq),
        in_specs=[pl.BlockSpec((None, tq, C_WIDTH), lambda bi, qi: (bi, qi, 0))]
                 + [kv_spec(p) for p in range(n_tiles)]
                 + [pl.BlockSpec((C_HEADS, 1, (n_tiles + 1) * tq), lambda bi, qi: (0, 0, 0))],
        out_specs=pl.BlockSpec((None, tq, C_WIDTH), lambda bi, qi: (bi, qi, 0)),
        out_shape=jax.ShapeDtypeStruct((b, t, C_WIDTH), F32),
        scratch_shapes=[pltpu.VMEM((C_HEADS, tq, n_tiles * tq), F32)],
        compiler_params=_cparams("parallel", "arbitrary"),
        name="win_attn_prompt",
    )(z, *([z] * n_tiles), rev)


def _win_attn_sample(z, win_full, rel_bias):
    b, tq, _ = z.shape
    lk = win_full.shape[1]
    dist = (lk - tq) + np.arange(tq)[:, None] - np.arange(lk)[None, :]
    valid = (dist >= 0) & (dist <= WIN)
    bias = jnp.transpose(_c_tab(rel_bias)[_t5_bucket_np(np.clip(dist, 0, None))], (2, 0, 1))
    bias = jnp.where(jnp.asarray(valid)[None], bias, NEG).astype(F32)
    kern = functools.partial(_tile_attn_kernel, nh=C_HEADS, rep=C_HPG, n_tiles=1, k_col=0, v_col=C_KV,
                             shared_kv=True, with_lse=False, lead_axis=0)
    return pl.pallas_call(
        kern,
        grid=(b,),
        in_specs=[pl.BlockSpec((None, tq, C_WIDTH), lambda bi: (bi, 0, 0)),
                  pl.BlockSpec((None, lk, 2 * C_KV), lambda bi: (bi, 0, 0)),
                  pl.BlockSpec((C_HEADS, tq, lk), lambda bi: (0, 0, 0))],
        out_specs=pl.BlockSpec((None, tq, C_WIDTH), lambda bi: (bi, 0, 0)),
        out_shape=jax.ShapeDtypeStruct((b, tq, C_WIDTH), F32),
        compiler_params=_cparams("parallel"),
        name="win_attn_sample",
    )(z, win_full, bias)


def _odd_mix_kernel(oc_ref, os_ref, ow_ref, gl_ref, gp_ref, mix_ref):
    gates = jax.nn.sigmoid(gl_ref[...])
    branches = (oc_ref, os_ref, ow_ref)
    for h in range(C_HEADS):
        cs = slice(h * HEAD_DIM, (h + 1) * HEAD_DIM)
        acc = gates[:, h:h + 1] * branches[0][:, cs]
        for br in range(1, 3):
            lane = br * C_HEADS + h
            acc = acc + gates[:, lane:lane + 1] * branches[br][:, cs]
        mix_ref[:, cs] = (acc * _silu(gp_ref[:, cs])).astype(BF16)


def _odd_mix(oc, osel, ow, z2, zgp, *, tm):
    m = z2.shape[0]
    wide = pl.BlockSpec((tm, C_WIDTH), lambda i: (i, 0))
    return pl.pallas_call(
        _odd_mix_kernel,
        grid=(m // tm,),
        in_specs=[wide, wide, wide, pl.BlockSpec((tm, LANES), lambda i: (i, ODD_GL0 // LANES)), wide],
        out_specs=wide,
        out_shape=jax.ShapeDtypeStruct((m, C_WIDTH), BF16),
        compiler_params=_cparams("parallel"),
        name="odd_mix",
    )(oc, osel, ow, z2, zgp)


def _odd_layer(x, past, win_hist, rel_bias, layer, norm_g, w_in, w_in_gp, cmp_pos, cmp_w1, cmp_b1, cmp_w2, w_out,
               *, tm, tn_in, tn_out, tq):
    b, t, d = x.shape
    m = b * t
    x2 = x.reshape(m, d)
    z2 = _norm_matmul(x2, norm_g[layer], w_in, layer, tm=tm, tn=tn_in, n_out=ODD_Z)
    zgp = _norm_matmul(x2, norm_g[layer], w_in_gp, layer, tm=tm, tn=tn_in)
    z = z2.reshape(b, t, ODD_Z)
    rows = z[:, :, ODD_KV0:ODD_KV0 + 4 * C_KV].reshape(b, t, 4, C_KV_HEADS, HEAD_DIM)
    win_new = z[:, :, ODD_KV0 + 4 * C_KV:ODD_KV0 + 6 * C_KV]
    if past is None:
        length = t
        a, bm = _cmp_partials_prompt(z, layer, cmp_pos, cmp_w1)
        qpos0 = 0
    else:
        cache, page_table = past
        p_len = page_table.shape[1] * cache.shape[2]
        length = p_len + t
        assert length // CMP_STRIDE == p_len // CMP_STRIDE and p_len % SEL_BLK == 0
        a, bm = _cmp_partials_paged(cache, layer, page_table, cmp_pos, cmp_w1)
        qpos0 = p_len
    n_sel = -(-length // SEL_BLK)
    oc, sel = _cmp_select(z, a, bm, layer, cmp_b1, cmp_w2, tq=tq, qpos0=qpos0, n_sel=n_sel)
    if past is None:
        osel = _sel_attn_prompt(z, sel, rel_bias, tq=tq)
        ow = _win_attn_prompt(z, rel_bias, tq=tq)
        new_win = win_new[:, t - min(WIN, t):].reshape(b, min(WIN, t), 2, C_KV_HEADS, HEAD_DIM)
    else:
        osel = _sel_attn_paged(z, sel, cache, layer, page_table, rel_bias)
        win_full = jnp.concatenate([win_hist.reshape(b, -1, 2 * C_KV), win_new], axis=1)
        ow = _win_attn_sample(z, win_full, rel_bias)
        lk = win_full.shape[1]
        keep = min(WIN, length)
        new_win = win_full[:, lk - keep:].reshape(b, keep, 2, C_KV_HEADS, HEAD_DIM)
    mix = _odd_mix(oc.reshape(m, C_WIDTH), osel.reshape(m, C_WIDTH), ow.reshape(m, C_WIDTH), z2, zgp,
                   tm=min(tm, 256))
    y = _matmul_res([mix], w_out, layer, x2, tm=tm, tn=tn_out)
    return y.reshape(b, t, d), rows, new_win


def _run_trunk(x, a_caches, conv_state, c_cache, c_win, page_table, rel_bias, norm_even, w_in_even, conv_w,
               conv_b, conv_ln_g, conv_ln_b, conv_pw_w, conv_pw_b, w_out_even, norm_odd, w_in_odd, w_in_odd_gp,
               cmp_pos, cmp_w1, cmp_b1, cmp_w2, w_out_odd, final_norm, *, tm, tt, tq):
    b, t, d = x.shape
    prompt = a_caches is None
    depth = norm_even.shape[0] + norm_odd.shape[0]
    new_a = [[] for _ in A_GROUPS]
    a_new = None
    new_conv, new_rows, new_win = [], [], []
    for depth_i in range(depth):
        i = depth_i // 2
        if depth_i % 2 == 0:
            conv_hist = jnp.zeros((b, CONV_W - 1, B_WIDTH), F32) if prompt else conv_state[i]
            x, hists, conv = _even_layer(x, a_caches, a_new, conv_hist, rel_bias, i, norm_even, w_in_even, conv_w,
                                         conv_b, conv_ln_g, conv_ln_b, conv_pw_w, conv_pw_b,
                                         w_out_even, tm=tm, tn_in=512, tn_out=512, tt=tt)
            if prompt:
                for g in range(A_NG):
                    new_a[g].append(hists[g])
            else:
                a_new = hists
            new_conv.append(conv)
        else:
            past = None if prompt else (c_cache, page_table)
            x, rows, win = _odd_layer(x, past, None if prompt else c_win[i], rel_bias, i, norm_odd,
                                      w_in_odd, w_in_odd_gp, cmp_pos, cmp_w1, cmp_b1, cmp_w2,
                                      w_out_odd, tm=tm, tn_in=512, tn_out=512, tq=tq)
            new_rows.append(rows)
            new_win.append(win)
    y = _rms_norm(x.reshape(b * t, d), final_norm, tm=min(tm, 256)).reshape(b, t, d)
    if prompt:
        a_out = [jnp.stack(a) for a in new_a]
    else:
        a_out = [n.reshape(c.shape) for n, c in zip(a_new, a_caches)]
    return (y, a_out, jnp.stack(new_conv), jnp.stack(new_rows), jnp.stack(new_win))


def kernel(x_prompt, x_sample, cache_a_kv0, cache_a_kv1, cache_a_kv2, state_b_conv, cache_c_kv, cache_c_win, page_table, rel_bias, norm_even, w_in_even, conv_w, conv_b, conv_ln_g, conv_ln_b, conv_pw_w, conv_pw_b, w_out_even, norm_odd, w_in_odd, cmp_pos, cmp_w1, cmp_b1, cmp_w2, w_out_odd, final_norm):
    w_in_odd_gp = w_in_odd[:, :, ODD_GP0:]
    weights = (rel_bias, norm_even, w_in_even, conv_w, conv_b, conv_ln_g, conv_ln_b, conv_pw_w, conv_pw_b,
               w_out_even, norm_odd, w_in_odd, w_in_odd_gp, cmp_pos, cmp_w1, cmp_b1, cmp_w2, w_out_odd, final_norm)
    y_p, a_p, conv_p, rows_p, win_p = _run_trunk(x_prompt, None, None, None, None, None, *weights,
                                                 tm=2048, tt=256, tq=128)
    n_odd, n_pool, page = cache_c_kv.shape[:3]
    c_cache = cache_c_kv.reshape(n_odd, n_pool, page, CACHE_HALVES, CACHE_KINDS, HEAD_DIM)
    db, dt = x_sample.shape[:2]
    y_s, a_s, conv_s, rows_s, win_s = _run_trunk(x_sample, (cache_a_kv0, cache_a_kv1, cache_a_kv2), state_b_conv,
                                                 c_cache, cache_c_win, page_table, *weights,
                                                 tm=db * dt, tt=dt, tq=dt)
    return (y_p, y_s, a_p[0], a_p[1], a_p[2], conv_p, rows_p, win_p, a_s[0], a_s[1], a_s[2], conv_s, rows_s, win_s)
```

```python
import functools
import math

import numpy as np
import jax
import jax.numpy as jnp
from jax import lax
from jax.experimental import pallas as pl
from jax.experimental.pallas import tpu as pltpu

F32 = jnp.float32
BF16 = jnp.bfloat16

D_MODEL = 2048
HEAD_DIM = 128
LANES = 128
SUBLANES = 8
A_GROUPS = ((128, 1), (512, 4), (2048, 16))
A_NG = 3
A_HPG = 8
A_QKV = A_NG * A_HPG * HEAD_DIM
A_WIDTH = A_HPG * HEAD_DIM
B_WIDTH = 1024
CONV_W = 31
CONV_HALO = 32
EVEN_IN = 3 * A_QKV + A_WIDTH + 3 * B_WIDTH
C_HEADS = 16
C_KV_HEADS = 2
C_HPG = C_HEADS // C_KV_HEADS
C_WIDTH = C_HEADS * HEAD_DIM
C_KV = C_KV_HEADS * HEAD_DIM
CMP_BLK = 32
CMP_STRIDE = 16
CMP_HID = 128
SEL_BLK = 64
SEL_N = 16
WIN = 512
NUM_BUCKETS = 32
MAX_DIST = 2048
EPS = 1e-6
NEG = -1e30
SCALE = HEAD_DIM ** -0.5
LOG2E = math.log2(math.e)
VMEM_LIMIT = 56 * 1024 * 1024


def _cparams(*sem):
    return pltpu.CompilerParams(dimension_semantics=sem, vmem_limit_bytes=VMEM_LIMIT)


def _t5_bucket_np(dist):
    max_exact = NUM_BUCKETS // 2
    d = np.maximum(np.asarray(dist, np.int64), 0)
    ratio = np.log(np.maximum(d, 1).astype(np.float64) / max_exact) / math.log(MAX_DIST / max_exact)
    large = np.minimum(max_exact + (ratio * (NUM_BUCKETS - max_exact)).astype(np.int64), NUM_BUCKETS - 1)
    return np.where(d < max_exact, d, large).astype(np.int32)


def _silu(x):
    return x * jax.nn.sigmoid(x)


NORM_ROWS = 256


def _norm_matmul_kernel(x_ref, g_ref, w_ref, o_ref, xn_ref):
    @pl.when(pl.program_id(1) == 0)
    def _():
        rows = min(NORM_ROWS, x_ref.shape[0])

        def norm(c, carry):
            cur = pl.ds(pl.multiple_of(c * rows, rows), rows)
            x = x_ref[cur, :]
            ms = jnp.mean(x * x, axis=-1, keepdims=True)
            xn_ref[cur, :] = (x * lax.rsqrt(ms + EPS) * g_ref[...]).astype(BF16)
            return carry

        lax.fori_loop(0, x_ref.shape[0] // rows, norm, 0)

    o_ref[...] = jnp.dot(xn_ref[...], w_ref[...].astype(BF16), preferred_element_type=F32)


def _norm_matmul(x, g, w, layer, *, tm, tn, n_out=None):
    m, k = x.shape
    n = w.shape[2] if n_out is None else n_out
    assert n % tn == 0 and n <= w.shape[2]
    return pl.pallas_call(
        _norm_matmul_kernel,
        grid=(m // tm, n // tn),
        in_specs=[pl.BlockSpec((tm, k), lambda i, j: (i, 0), pipeline_mode=pl.Buffered(1)),
                  pl.BlockSpec((1, k), lambda i, j: (0, 0)),
                  pl.BlockSpec((None, k, tn), lambda i, j: (layer, 0, j))],
        out_specs=pl.BlockSpec((tm, tn), lambda i, j: (i, j)),
        out_shape=jax.ShapeDtypeStruct((m, n), F32),
        scratch_shapes=[pltpu.VMEM((tm, k), BF16)],
        compiler_params=_cparams("parallel", "arbitrary"),
        name="norm_matmul",
    )(x, g.reshape(1, k), w)


def _matmul_res_kernel(*refs, n_parts):
    a_refs = refs[:n_parts]
    w_ref, r_ref, o_ref = refs[n_parts:]
    acc = r_ref[...]
    k0 = 0
    for a_ref in a_refs:
        kp = a_ref.shape[1]
        acc = acc + jnp.dot(a_ref[...], w_ref[k0:k0 + kp, :].astype(BF16), preferred_element_type=F32)
        k0 += kp
    o_ref[...] = acc


def _matmul_res(parts, w, layer, res, *, tm, tn):
    m = res.shape[0]
    k, n = w.shape[1:]
    assert sum(p.shape[1] for p in parts) == k
    return pl.pallas_call(
        functools.partial(_matmul_res_kernel, n_parts=len(parts)),
        grid=(m // tm, n // tn),
        in_specs=[pl.BlockSpec((tm, p.shape[1]), lambda i, j: (i, 0), pipeline_mode=pl.Buffered(1))
                  for p in parts]
                 + [pl.BlockSpec((None, k, tn), lambda i, j: (layer, 0, j)),
                    pl.BlockSpec((tm, tn), lambda i, j: (i, j))],
        out_specs=pl.BlockSpec((tm, tn), lambda i, j: (i, j)),
        out_shape=jax.ShapeDtypeStruct((m, n), F32),
        compiler_params=_cparams("parallel", "arbitrary"),
        name="matmul_res",
    )(*parts, w, res)


def _rms_kernel(x_ref, g_ref, o_ref):
    x = x_ref[...]
    ms = jnp.mean(x * x, axis=-1, keepdims=True)
    o_ref[...] = x * lax.rsqrt(ms + EPS) * g_ref[...]


def _rms_norm(x, g, *, tm):
    m, k = x.shape
    return pl.pallas_call(
        _rms_kernel,
        grid=(m // tm,),
        in_specs=[pl.BlockSpec((tm, k), lambda i: (i, 0)), pl.BlockSpec((1, k), lambda i: (0, 0))],
        out_specs=pl.BlockSpec((tm, k), lambda i: (i, 0)),
        out_shape=jax.ShapeDtypeStruct((m, k), F32),
        compiler_params=_cparams("parallel"),
        name="rms_norm",
    )(x, g.reshape(1, k))


def _c_tab(rel_bias):
    return rel_bias[:, :C_HEADS]


def _c_rev_table(rel_bias, length, tq, max_dist):
    dist = length - np.arange(length + tq)
    valid = (dist >= 0) & (dist <= max_dist)
    tab = _c_tab(rel_bias)[_t5_bucket_np(np.clip(dist, 0, None))]
    tab = jnp.where(jnp.asarray(valid)[:, None], tab, NEG).astype(F32)
    return tab.T.reshape(C_HEADS, 1, length + tq)


def _toeplitz(vec, tq):
    w = vec.shape[1]
    return pltpu.roll(jnp.broadcast_to(vec, (tq, w)), 0, 1, stride=1, stride_axis=0)[:, tq:]


def _tile_attn_kernel(*refs, nh, rep, n_tiles, k_col, v_col, shared_kv, with_lse, lead_axis, toeplitz=False):
    if toeplitz:
        rev_ref, bias_scr = refs[-1 - int(with_lse) - 2], refs[-1]
        tq_ = refs[0].shape[0]

        @pl.when(pl.program_id(lead_axis) == 0)
        def _():
            for h in range(nh):
                bias_scr[h] = _toeplitz(rev_ref[h], tq_)

        refs = list(refs[:-1])
        refs[-1 - int(with_lse) - 1] = bias_scr
    q_ref = refs[0]
    if shared_kv:
        k_refs = v_refs = refs[1:1 + n_tiles]
        nxt = 1 + n_tiles
    else:
        k_refs = refs[1:1 + n_tiles]
        v_refs = refs[1 + n_tiles:1 + 2 * n_tiles]
        nxt = 1 + 2 * n_tiles
    b_ref = refs[nxt]
    o_ref = refs[nxt + 1]
    lse_ref = refs[nxt + 2] if with_lse else None
    tq = q_ref.shape[0]
    tk = k_refs[0].shape[0]
    lk = n_tiles * tk
    if n_tiles > 1:
        qi = pl.program_id(lead_axis)
        col = lax.broadcasted_iota(jnp.int32, (tq, lk), 1)
        pad_mask = jnp.where(col < (n_tiles - 1 - qi) * tk, NEG, 0.0).astype(F32)
    else:
        pad_mask = None
    if with_lse:
        lane = lax.broadcasted_iota(jnp.int32, (tq, LANES), 1)
        lse_t = jnp.zeros((tq, LANES), F32)
    for j in range(nh // rep):
        kc = k_col + j * HEAD_DIM
        vc = v_col + j * HEAD_DIM
        heads = range(j * rep, (j + 1) * rep)
        q = jnp.concatenate([(q_ref[:, h * HEAD_DIM:(h + 1) * HEAD_DIM] * SCALE).astype(BF16) for h in heads], axis=0)
        if n_tiles > 1:
            k = jnp.concatenate([r[:, kc:kc + HEAD_DIM] for r in k_refs], axis=0).astype(BF16)
            v = jnp.concatenate([r[:, vc:vc + HEAD_DIM] for r in v_refs], axis=0).astype(BF16)
        else:
            k = k_refs[0][:, kc:kc + HEAD_DIM].astype(BF16)
            v = v_refs[0][:, vc:vc + HEAD_DIM].astype(BF16)
        s = lax.dot_general(q, k, (((1,), (1,)), ((), ())), preferred_element_type=F32)
        s = s.reshape(rep, tq, lk) + b_ref[j * rep:(j + 1) * rep]
        if pad_mask is not None:
            s = s + pad_mask[None]
        s = s.reshape(rep * tq, lk)
        m = jnp.max(s, axis=-1, keepdims=True)
        e = jnp.exp(s - m)
        den = jnp.sum(e, axis=-1, keepdims=True)
        o = jnp.dot(e.astype(BF16), v, preferred_element_type=F32) / den
        for r, h in enumerate(heads):
            o_ref[:, h * HEAD_DIM:(h + 1) * HEAD_DIM] = o[r * tq:(r + 1) * tq]
        if with_lse:
            lse = m + jnp.log(den)
            for r, h in enumerate(heads):
                lse_t = jnp.where(lane == h, lse[r * tq:(r + 1) * tq], lse_t)
    if with_lse:
        lse_ref[...] = lse_t


A_TU = 128


def _a_rev_table(rel_bias):
    j = 2 * A_TU - np.arange(3 * A_TU)
    rows = []
    for g, (win, dil) in enumerate(A_GROUPS):
        valid = (j >= 0) & (j <= win // dil)
        tab = rel_bias[:, g * A_HPG:(g + 1) * A_HPG][_t5_bucket_np(np.clip(j, 0, None) * dil)]
        rows.append(jnp.where(jnp.asarray(valid)[:, None], tab, NEG).T)
    return jnp.stack(rows, axis=1).astype(F32)


def _a_prompt_kernel(q0_ref, q1_ref, q2_ref, k0_ref, k1_ref, k2_ref, v0_ref, v1_ref, v2_ref, ga_ref, rev_ref,
                     mix_ref, o_scr, l_scr, bias_scr):
    t = q0_ref.shape[0]
    tu = A_TU
    q_refs, k_refs, v_refs = (q0_ref, q1_ref, q2_ref), (k0_ref, k1_ref, k2_ref), (v0_ref, v1_ref, v2_ref)
    for g in range(A_NG):
        bias_scr[g] = _toeplitz(rev_ref[g:g + 1, :], tu)
    col = lax.broadcasted_iota(jnp.int32, (tu, 2 * tu), 1)
    for g, (win, dil) in enumerate(A_GROUPS):
        n_u = t // (dil * tu)
        q_ref, k_ref, v_ref = q_refs[g], k_refs[g], v_refs[g]

        def rows(r, u, dil=dil):
            start = r + u * (tu * dil)
            if dil == 1:
                return pl.ds(pl.multiple_of(start, tu), tu)
            return pl.ds(start, tu, stride=dil)

        def body(it, carry, g=g, n_u=n_u, q_ref=q_ref, k_ref=k_ref, v_ref=v_ref, rows=rows):
            r = it // n_u
            u = it % n_u
            cur = rows(r, u)
            q = (q_ref[cur, :] * SCALE).astype(BF16)
            if n_u > 1:
                prev = rows(r, jnp.maximum(u - 1, 0))
                k = jnp.concatenate([k_ref[prev, :], k_ref[cur, :]], axis=0).astype(BF16)
                v = jnp.concatenate([v_ref[prev, :], v_ref[cur, :]], axis=0).astype(BF16)
                bias = bias_scr[g] + jnp.where((col < tu) & (u == 0), NEG, 0.0)
            else:
                k = k_ref[cur, :].astype(BF16)
                v = v_ref[cur, :].astype(BF16)
                bias = bias_scr[g][:, tu:]
            s = lax.dot_general(q, k, (((1,), (1,)), ((), ())), preferred_element_type=F32) + bias
            m = jnp.max(s, axis=-1, keepdims=True)
            e = jnp.exp(s - m)
            den = jnp.sum(e, axis=-1, keepdims=True)
            o_scr[g, cur, :] = jnp.dot(e.astype(BF16), v, preferred_element_type=F32) / den
            l_scr[g, cur, :] = jnp.broadcast_to(m + jnp.log(den), (tu, HEAD_DIM))
            return carry

        lax.fori_loop(0, t // tu, body, 0, unroll=4)

    def combine(c, carry):
        cur = pl.ds(pl.multiple_of(c * tu, tu), tu)
        ls = [l_scr[g, cur, :] for g in range(A_NG)]
        m = jnp.maximum(jnp.maximum(ls[0], ls[1]), ls[2])
        es = [jnp.exp(l - m) for l in ls]
        acc = es[0] * o_scr[0, cur, :] + es[1] * o_scr[1, cur, :] + es[2] * o_scr[2, cur, :]
        out = acc / (es[0] + es[1] + es[2])
        mix_ref[cur, :] = (out * _silu(ga_ref[cur, :])).astype(BF16)
        return carry

    lax.fori_loop(0, t // tu, combine, 0)


def _a_attn_prompt(z, rel_bias):
    b, t, n = z.shape
    for win, dil in A_GROUPS:
        assert win // dil == A_TU and t % (dil * A_TU) == 0
    rev = _a_rev_table(rel_bias)
    nq = A_QKV // HEAD_DIM
    col = lambda base: pl.BlockSpec((None, t, HEAD_DIM), lambda bi, h: (bi, 0, base + h))
    return pl.pallas_call(
        _a_prompt_kernel,
        grid=(b, A_HPG),
        in_specs=[col(g * A_HPG) for g in range(A_NG)]
                 + [col(nq + g * A_HPG) for g in range(A_NG)]
                 + [col(2 * nq + g * A_HPG) for g in range(A_NG)]
                 + [col(3 * nq), pl.BlockSpec((None, A_NG, 3 * A_TU), lambda bi, h: (h, 0, 0))],
        out_specs=pl.BlockSpec((None, t, HEAD_DIM), lambda bi, h: (bi, 0, h)),
        out_shape=jax.ShapeDtypeStruct((b, t, A_WIDTH), BF16),
        scratch_shapes=[pltpu.VMEM((A_NG, t, HEAD_DIM), F32), pltpu.VMEM((A_NG, t, HEAD_DIM), F32),
                        pltpu.VMEM((A_NG, A_TU, 2 * A_TU), F32)],
        compiler_params=_cparams("parallel", "parallel"),
        name="a_attn_prompt",
    )(*([z] * 10), rev)


def _a_bias_sample(rel_bias, g, hist_len, tq):
    win, dil = A_GROUPS[g]
    lk = hist_len + tq
    dist = hist_len + tq - 1 - np.arange(lk + tq - 1)
    valid = (dist >= 0) & (dist <= win) & (dist % dil == 0)
    tab = rel_bias[:, g * A_HPG:(g + 1) * A_HPG][_t5_bucket_np(np.clip(dist, 0, None))]
    rev = jnp.where(jnp.asarray(valid)[:, None], tab, NEG).T.astype(F32)
    return jnp.stack([rev[:, tq - 1 - t:tq - 1 - t + lk] for t in range(tq)], axis=1)


A_ROWS = 2 * A_HPG
A_CHUNK = 512


def _a_sample_kernel(*refs, aliased):
    q_ref, kn_ref, vn_ref, old_ref, head_ref, bias_ref, bnew_ref = refs[:7]
    o_ref, lse_ref, new_ref, m_ref, l_ref, acc_ref = refs[7 + int(aliased):]
    c = pl.program_id(1)
    last = pl.num_programs(1) - 1
    tq = q_ref.shape[0]
    n_tok = old_ref.shape[0] // A_ROWS
    shift = tq * A_ROWS

    @pl.when(c == 0)
    def _():
        m_ref[...] = jnp.full(m_ref.shape, NEG, F32)
        l_ref[...] = jnp.zeros(l_ref.shape, F32)
        acc_ref[...] = jnp.zeros(acc_ref.shape, F32)

    def update(h, q, k, v, bias):
        s = lax.dot_general(q, k, (((1,), (1,)), ((), ())), preferred_element_type=F32) + bias
        m_old = m_ref[h]
        m_new = jnp.maximum(m_old, jnp.max(s, axis=-1, keepdims=True))
        alpha = jnp.exp(m_old - m_new)
        e = jnp.exp(s - m_new)
        m_ref[h] = m_new
        l_ref[h] = alpha * l_ref[h] + jnp.sum(e, axis=-1, keepdims=True)
        acc_ref[h] = alpha * acc_ref[h] + jnp.dot(e.astype(BF16), v, preferred_element_type=F32)

    qs = [(q_ref[:, h * HEAD_DIM:(h + 1) * HEAD_DIM] * SCALE).astype(BF16) for h in range(A_HPG)]
    for h in range(A_HPG):
        k = old_ref[pl.ds(h, n_tok, stride=A_ROWS), :].astype(BF16)
        v = old_ref[pl.ds(A_HPG + h, n_tok, stride=A_ROWS), :].astype(BF16)
        update(h, qs[h], k, v, bias_ref[h])

    new_ref[0:n_tok * A_ROWS - shift, :] = old_ref[shift:, :]

    @pl.when(c < last)
    def _():
        new_ref[n_tok * A_ROWS - shift:, :] = head_ref[...]

    @pl.when(c == last)
    def _():
        lane = lax.broadcasted_iota(jnp.int32, (tq, LANES), 1)
        lse_t = jnp.zeros((tq, LANES), F32)
        for h in range(A_HPG):
            cs = slice(h * HEAD_DIM, (h + 1) * HEAD_DIM)
            update(h, qs[h], kn_ref[:, cs].astype(BF16), vn_ref[:, cs].astype(BF16), bnew_ref[h])
            o_ref[:, cs] = acc_ref[h] / l_ref[h]
            lse_t = jnp.where(lane == h, m_ref[h] + jnp.log(l_ref[h]), lse_t)
            base = n_tok * A_ROWS - shift
            new_ref[pl.ds(base + h, tq, stride=A_ROWS), :] = kn_ref[:, cs]
            new_ref[pl.ds(base + A_HPG + h, tq, stride=A_ROWS), :] = vn_ref[:, cs]
        lse_ref[...] = lse_t


def _a_attn_sample(z, cache, layer, new_cache, rel_bias, g):
    b, tq, n = z.shape
    n_layers, _, w = cache.shape[:3]
    assert cache.shape[3:] == (2, A_HPG, HEAD_DIM) and w % LANES == 0 and tq * A_ROWS == LANES
    ch = min(w, A_CHUNK)
    n_ch = w // ch
    old = cache.reshape(n_layers, b, w * A_ROWS, HEAD_DIM)
    bias_full = _a_bias_sample(rel_bias, g, w, tq)
    bias = jnp.stack([bias_full[:, :, i * ch:(i + 1) * ch] for i in range(n_ch)])
    bnew = bias_full[:, :, w:]
    nq = A_QKV // A_WIDTH
    zcol = lambda cb: pl.BlockSpec((None, tq, A_WIDTH), lambda bi, c: (bi, 0, cb))
    heads_per_chunk = ch * A_ROWS // LANES
    in_specs = [zcol(g), zcol(nq + g), zcol(2 * nq + g),
                pl.BlockSpec((None, None, ch * A_ROWS, HEAD_DIM), lambda bi, c: (layer, bi, c, 0)),
                pl.BlockSpec((None, None, LANES, HEAD_DIM),
                             lambda bi, c: (layer, bi, jnp.minimum(c + 1, n_ch - 1) * heads_per_chunk, 0)),
                pl.BlockSpec((None, A_HPG, tq, ch), lambda bi, c: (c, 0, 0, 0)),
                pl.BlockSpec((A_HPG, tq, tq), lambda bi, c: (0, 0, 0))]
    args = [z, z, z, old, old, bias, bnew]
    aliases = {}
    if new_cache is not None:
        in_specs.append(pl.BlockSpec(memory_space=pl.ANY))
        args.append(new_cache)
        aliases = {len(args) - 1: 2}
    o, lse, new = pl.pallas_call(
        functools.partial(_a_sample_kernel, aliased=new_cache is not None),
        grid=(b, n_ch),
        in_specs=in_specs,
        out_specs=[pl.BlockSpec((None, tq, A_WIDTH), lambda bi, c: (bi, 0, 0)),
                   pl.BlockSpec((None, tq, LANES), lambda bi, c: (bi, 0, 0)),
                   pl.BlockSpec((None, None, ch * A_ROWS, HEAD_DIM), lambda bi, c: (layer, bi, c, 0))],
        out_shape=[jax.ShapeDtypeStruct((b, tq, A_WIDTH), F32),
                   jax.ShapeDtypeStruct((b, tq, LANES), F32),
                   jax.ShapeDtypeStruct(old.shape, F32)],
        scratch_shapes=[pltpu.VMEM((A_HPG, tq, 1), F32), pltpu.VMEM((A_HPG, tq, 1), F32),
                        pltpu.VMEM((A_HPG, tq, HEAD_DIM), F32)],
        input_output_aliases=aliases,
        compiler_params=_cparams("parallel", "arbitrary"),
        name=f"a_attn_sample_g{g}",
    )(*args)
    return o, lse, new


def _conv_kernel(a_ref, gt_ref, gb_ref, hist_ref, cw_ref, cb_ref, lg_ref, lb_ref, pw_ref, pb_ref,
                 o_ref, nc_ref, ubuf, ybuf, pwb):
    ti = pl.program_id(1)
    nt = pl.num_programs(1)
    tt = a_ref.shape[0]
    nhist = CONV_W - 1
    pad = CONV_HALO - nhist

    @pl.when(ti == 0)
    def _():
        pwb[...] = pw_ref[...].astype(BF16)
        ubuf[0:pad, :] = jnp.zeros((pad, B_WIDTH), F32)
        ubuf[pad:CONV_HALO, :] = hist_ref[...]

    ubuf[CONV_HALO:CONV_HALO + tt, :] = a_ref[...] * jax.nn.sigmoid(gt_ref[...])
    for c in range(B_WIDTH // LANES):
        cs = slice(c * LANES, (c + 1) * LANES)
        acc = jnp.zeros((tt, LANES), F32) + cb_ref[:, cs]
        for r in range(SUBLANES):
            taps = [a for a in range(CONV_HALO // SUBLANES + 1) if 0 <= SUBLANES * a + r - pad < CONV_W]
            span = tt if r == 0 else tt + SUBLANES
            z = None
            for a in taps:
                k = SUBLANES * a + r - pad
                term = ubuf[SUBLANES * a:SUBLANES * a + span, cs] * cw_ref[k:k + 1, cs]
                z = term if z is None else z + term
            acc = acc + (z if r == 0 else z[r:r + tt])
        ybuf[:, cs] = acc
    y = ybuf[...]
    mu = jnp.mean(y, axis=-1, keepdims=True)
    yc = y - mu
    var = jnp.mean(yc * yc, axis=-1, keepdims=True)
    yn = yc * lax.rsqrt(var + EPS) * lg_ref[...] + lb_ref[...]
    act = _silu(yn).astype(BF16)
    ob = jnp.dot(act, pwb[...], preferred_element_type=F32) + pb_ref[...]
    o_ref[...] = (ob * _silu(gb_ref[...])).astype(BF16)

    @pl.when(ti == nt - 1)
    def _():
        nc_ref[...] = ubuf[tt + pad:tt + CONV_HALO, :]

    @pl.when(ti < nt - 1)
    def _():
        ubuf[0:CONV_HALO, :] = ubuf[tt:tt + CONV_HALO, :]


def _conv_module(z, hist, layer, conv_w, conv_b, ln_g, ln_b, pw_w, pw_b, *, tt):
    b, t, n = z.shape
    assert t % tt == 0 and (tt >= CONV_HALO or t == tt)
    glu0 = (3 * A_QKV + A_WIDTH) // B_WIDTH
    row = lambda a: a.reshape(a.shape[0], 1, B_WIDTH)
    blk = (None, tt, B_WIDTH)
    lyr = lambda shape: pl.BlockSpec((None,) + shape, lambda bi, ti: (layer,) + (0,) * len(shape))
    return pl.pallas_call(
        _conv_kernel,
        grid=(b, t // tt),
        in_specs=[pl.BlockSpec(blk, lambda bi, ti: (bi, ti, glu0)),
                  pl.BlockSpec(blk, lambda bi, ti: (bi, ti, glu0 + 1)),
                  pl.BlockSpec(blk, lambda bi, ti: (bi, ti, glu0 + 2)),
                  pl.BlockSpec((None, CONV_W - 1, B_WIDTH), lambda bi, ti: (bi, 0, 0)),
                  lyr((CONV_W, B_WIDTH)), lyr((1, B_WIDTH)), lyr((1, B_WIDTH)), lyr((1, B_WIDTH)),
                  lyr((B_WIDTH, B_WIDTH)), lyr((1, B_WIDTH))],
        out_specs=[pl.BlockSpec(blk, lambda bi, ti: (bi, ti, 0)),
                   pl.BlockSpec((None, CONV_W - 1, B_WIDTH), lambda bi, ti: (bi, 0, 0))],
        out_shape=[jax.ShapeDtypeStruct((b, t, B_WIDTH), BF16),
                   jax.ShapeDtypeStruct((b, CONV_W - 1, B_WIDTH), F32)],
        scratch_shapes=[pltpu.VMEM((CONV_HALO + tt, B_WIDTH), F32),
                        pltpu.VMEM((tt, B_WIDTH), F32),
                        pltpu.VMEM((B_WIDTH, B_WIDTH), BF16)],
        compiler_params=_cparams("parallel", "arbitrary"),
        name="conv_module",
    )(z, z, z, hist, conv_w, row(conv_b), row(ln_g), row(ln_b), pw_w, row(pw_b))


def _even_mix_kernel(o0_ref, o1_ref, o2_ref, l0_ref, l1_ref, l2_ref, ga_ref, mix_ref):
    ls = [l0_ref[...], l1_ref[...], l2_ref[...]]
    m = jnp.maximum(jnp.maximum(ls[0], ls[1]), ls[2])
    es = [jnp.exp(l - m) for l in ls]
    inv = 1.0 / (es[0] + es[1] + es[2])
    ws = [e * inv for e in es]
    o_refs = (o0_ref, o1_ref, o2_ref)
    for h in range(A_HPG):
        cs = slice(h * HEAD_DIM, (h + 1) * HEAD_DIM)
        acc = ws[0][:, h:h + 1] * o_refs[0][:, cs]
        for g in range(1, A_NG):
            acc = acc + ws[g][:, h:h + 1] * o_refs[g][:, cs]
        mix_ref[:, cs] = (acc * _silu(ga_ref[:, cs])).astype(BF16)


def _even_mix(oas, lses, z2, *, tm):
    m = z2.shape[0]
    ga_blk = 3 * A_QKV // A_WIDTH
    wide = lambda c: pl.BlockSpec((tm, A_WIDTH), lambda i: (i, c))
    narrow = pl.BlockSpec((tm, LANES), lambda i: (i, 0))
    return pl.pallas_call(
        _even_mix_kernel,
        grid=(m // tm,),
        in_specs=[wide(0), wide(0), wide(0), narrow, narrow, narrow, wide(ga_blk)],
        out_specs=wide(0),
        out_shape=jax.ShapeDtypeStruct((m, A_WIDTH), BF16),
        compiler_params=_cparams("parallel"),
        name="even_mix",
    )(*oas, *lses, z2)


def _even_layer(x, a_caches, a_new, conv_hist, rel_bias, layer, norm_g, w_in, conv_w, conv_b, ln_g, ln_b, pw_w,
                pw_b, w_out, *, tm, tn_in, tn_out, tt):
    b, t, d = x.shape
    m = b * t
    x2 = x.reshape(m, d)
    z2 = _norm_matmul(x2, norm_g[layer], w_in, layer, tm=tm, tn=tn_in)
    z = z2.reshape(b, t, EVEN_IN)
    oas, lses, new_hists = [], [], []
    for g, (win, _) in enumerate(A_GROUPS):
        if a_caches is None:
            kcol = A_QKV + g * A_WIDTH
            vcol = 2 * A_QKV + g * A_WIDTH
            new_kv = jnp.stack([z[:, :, kcol:kcol + A_WIDTH], z[:, :, vcol:vcol + A_WIDTH]], axis=2)
            new_hists.append(new_kv[:, t - min(win, t):].reshape(b, min(win, t), 2, A_HPG, HEAD_DIM))
        else:
            assert a_caches[g].shape[2] == win
            o, lse, new = _a_attn_sample(z, a_caches[g], layer, None if a_new is None else a_new[g], rel_bias, g)
            new_hists.append(new)
            oas.append(o.reshape(m, A_WIDTH))
            lses.append(lse.reshape(m, LANES))
    if a_caches is None:
        mix_a = _a_attn_prompt(z, rel_bias).reshape(m, A_WIDTH)
    else:
        mix_a = _even_mix(oas, lses, z2, tm=min(tm, 256))
    mix_b, new_conv = _conv_module(z, conv_hist, layer, conv_w, conv_b, ln_g, ln_b, pw_w, pw_b, tt=tt)
    y = _matmul_res([mix_a, mix_b.reshape(m, B_WIDTH)], w_out, layer, x2, tm=tm, tn=tn_out)
    return y.reshape(b, t, d), new_hists, new_conv


ODD_KV0 = C_WIDTH
ODD_GL0 = C_WIDTH + 6 * C_KV
ODD_GP0 = ODD_GL0 + 3 * C_HEADS
ODD_Z = ODD_GL0 + 512
CMP_HALF = CMP_BLK // 2
CMP_K = CMP_HALF * HEAD_DIM
PAGES_PER_STEP = 16
CACHE_KINDS = 2 * C_KV_HEADS
CACHE_ROW_KINDS = 2 * CACHE_KINDS


def _cmp_partials_kernel(*refs, n_src, n_prefetch, interleaved):
    refs = refs[n_prefetch:]
    src = refs[:n_src]
    pos_ref, w1_ref, a_ref, b_ref, rows_ref, w1b_ref = refs[n_src:n_src + 6]
    n = rows_ref.shape[1] // CMP_HALF

    if n_prefetch:
        @pl.when(pl.program_id(1) == 0)
        def _():
            w1b_ref[...] = w1_ref[...].astype(BF16)
    else:
        w1b_ref[...] = w1_ref[...].astype(BF16)

    r0 = 0
    for r in src:
        nr = r.shape[0] // CACHE_ROW_KINDS if interleaved else r.shape[0]
        for c in range(CACHE_KINDS):
            if interleaved:
                rows_ref[c, r0:r0 + nr, :] = r[pl.ds(c, nr, stride=CACHE_ROW_KINDS), :]
            else:
                rows_ref[c, r0:r0 + nr, :] = r[:, c * HEAD_DIM:(c + 1) * HEAD_DIM]
        r0 += nr
    for kv in range(2):
        x = jnp.concatenate(
            [jnp.concatenate([rows_ref[kv * C_KV_HEADS + g, pl.ds(l, n, stride=CMP_HALF), :]
                              for l in range(CMP_HALF)], axis=1) for g in range(C_KV_HEADS)], axis=0)
        for half, out in ((0, a_ref), (1, b_ref)):
            y = jnp.dot((x + pos_ref[kv, half]).astype(BF16), w1b_ref[kv, half], preferred_element_type=F32)
            for g in range(C_KV_HEADS):
                out[kv * C_KV_HEADS + g] = y[g * n:(g + 1) * n]


def _cmp_weights(cmp_pos, cmp_w1):
    n = cmp_pos.shape[0]
    return cmp_pos.reshape(n, 2, 2, 1, CMP_K), cmp_w1.reshape(n, 2, 2, CMP_K, CMP_HID)


def _cmp_partials_prompt(z, layer, cmp_pos, cmp_w1):
    b, t, _ = z.shape
    nch = t // CMP_STRIDE
    pos, w1 = _cmp_weights(cmp_pos, cmp_w1)
    kern = functools.partial(_cmp_partials_kernel, n_src=1, n_prefetch=0, interleaved=False)
    out = jax.ShapeDtypeStruct((b, 4, nch, CMP_HID), F32)
    ospec = pl.BlockSpec((None, 4, nch, CMP_HID), lambda bi: (bi, 0, 0, 0))
    return pl.pallas_call(
        kern,
        grid=(b,),
        in_specs=[pl.BlockSpec((None, t, 2 * C_KV), lambda bi: (bi, 0, ODD_KV0 // (2 * C_KV))),
                  pl.BlockSpec((None,) + pos.shape[1:], lambda bi: (layer, 0, 0, 0, 0)),
                  pl.BlockSpec((None,) + w1.shape[1:], lambda bi: (layer, 0, 0, 0, 0))],
        out_specs=[ospec, ospec],
        out_shape=[out, out],
        scratch_shapes=[pltpu.VMEM((2 * C_KV_HEADS, t, HEAD_DIM), F32), pltpu.VMEM(w1.shape[1:], BF16)],
        compiler_params=_cparams("parallel"),
        name="cmp_partials_prompt",
    )(z, pos, w1)


def _cmp_partials_paged(cache, layer, page_table, cmp_pos, cmp_w1):
    b, n_pages = page_table.shape
    page = cache.shape[2] // CACHE_ROW_KINDS
    pps = PAGES_PER_STEP
    assert n_pages % pps == 0 and page % CMP_STRIDE == 0
    nch_step = pps * page // CMP_STRIDE
    nch = n_pages * page // CMP_STRIDE
    pos, w1 = _cmp_weights(cmp_pos, cmp_w1)
    kern = functools.partial(_cmp_partials_kernel, n_src=pps, n_prefetch=1, interleaved=True)
    out = jax.ShapeDtypeStruct((b, 4, nch, CMP_HID), F32)
    ospec = pl.BlockSpec((None, 4, nch_step, CMP_HID), lambda bi, pg, pt: (bi, 0, pg, 0))

    def page_spec(j):
        return pl.BlockSpec((None, None, page * CACHE_ROW_KINDS, HEAD_DIM),
                            lambda bi, pg, pt: (layer, pt[bi, pg * pps + j], 0, 0))

    grid_spec = pltpu.PrefetchScalarGridSpec(
        num_scalar_prefetch=1,
        grid=(b, n_pages // pps),
        in_specs=[page_spec(j) for j in range(pps)] + [
            pl.BlockSpec((None,) + pos.shape[1:], lambda bi, pg, pt: (layer, 0, 0, 0, 0)),
            pl.BlockSpec((None,) + w1.shape[1:], lambda bi, pg, pt: (layer, 0, 0, 0, 0))],
        out_specs=[ospec, ospec],
        scratch_shapes=[pltpu.VMEM((2 * C_KV_HEADS, pps * page, HEAD_DIM), F32), pltpu.VMEM(w1.shape[1:], BF16)])
    return pl.pallas_call(
        kern, grid_spec=grid_spec, out_shape=[out, out],
        compiler_params=_cparams("parallel", "arbitrary"),
        name="cmp_partials_paged",
    )(page_table, *([cache] * pps), pos, w1)


def _overlap_np(n_ch, n_sel, ns_pad):
    ci = np.arange(n_ch)[:, None]
    si = np.arange(ns_pad)[None, :]
    ov = (ci * CMP_STRIDE < (si + 1) * SEL_BLK) & (ci * CMP_STRIDE + CMP_BLK > si * SEL_BLK)
    ov &= (ci < n_ch - 1) & (si < n_sel)
    return ov.astype(np.float32)


def _cmp_select_kernel(q_ref, ak_ref, bk_ref, av_ref, bv_ref, b1_ref, w2_ref, ov_ref, oc_ref, sel_ref,
                       *, qpos0, n_sel):
    qi = pl.program_id(2)
    tq = q_ref.shape[0]
    nch = ak_ref.shape[0]
    ns_pad = ov_ref.shape[1]

    def finish(a_ref, b_ref, kv):
        hid = a_ref[...] + pltpu.roll(b_ref[...], nch - 1, 0) + b1_ref[kv:kv + 1, :]
        return jnp.dot(_silu(hid).astype(BF16), w2_ref[kv].astype(BF16),
                       preferred_element_type=F32).astype(BF16)

    k_cmp = finish(ak_ref, bk_ref, 0)
    v_cmp = finish(av_ref, bv_ref, 1)
    qpos = qpos0 + qi * tq + lax.broadcasted_iota(jnp.int32, (tq, nch), 0)
    cmp_end = lax.broadcasted_iota(jnp.int32, (tq, nch), 1) * CMP_STRIDE + (CMP_BLK - 1)
    ok = (cmp_end <= qpos) & (cmp_end < (nch - 1) * CMP_STRIDE + CMP_BLK - 1)
    q = jnp.concatenate([(q_ref[:, h * HEAD_DIM:(h + 1) * HEAD_DIM] * SCALE).astype(BF16) for h in range(C_HPG)],
                        axis=0)
    s = lax.dot_general(q, k_cmp, (((1,), (1,)), ((), ())), preferred_element_type=F32)
    s = jnp.where(ok[None], s.reshape(C_HPG, tq, nch), NEG)
    m = jnp.max(s, axis=-1, keepdims=True)
    e = jnp.where(ok[None], jnp.exp(s - m), 0.0)
    den = jnp.sum(e, axis=-1, keepdims=True)
    p = e / jnp.where(den > 0.0, den, 1.0)
    psum = jnp.sum(p, axis=0)
    oc = jnp.dot(p.reshape(C_HPG * tq, nch).astype(BF16), v_cmp, preferred_element_type=F32)
    for h in range(C_HPG):
        oc_ref[:, h * HEAD_DIM:(h + 1) * HEAD_DIM] = oc[h * tq:(h + 1) * tq]
    imp = jnp.dot(psum, ov_ref[...], preferred_element_type=F32, precision=lax.Precision.HIGHEST)
    blk = lax.broadcasted_iota(jnp.int32, (tq, ns_pad), 1)
    qblk = (qpos0 + qi * tq + lax.broadcasted_iota(jnp.int32, (tq, ns_pad), 0)) // SEL_BLK
    forced = (blk == 0) | (blk == qblk) | (blk == qblk - 1)
    allowed = (blk <= qblk) & (blk < n_sel)
    score = jnp.where(allowed, jnp.where(forced, -NEG, imp), -1.0)
    rank = jnp.zeros((tq, ns_pad), jnp.int32)
    for j in range(n_sel):
        cj = score[:, j:j + 1]
        before = (cj > score) | ((cj == score) & (j < blk))
        rank = rank + before.astype(jnp.int32)
    sel_ref[...] = (allowed & (rank < SEL_N)).astype(F32)


def _cmp_select(z, a, bm, layer, cmp_b1, cmp_w2, *, tq, qpos0, n_sel):
    b, t, _ = z.shape
    nch = a.shape[2]
    ns_pad = -(-n_sel // LANES) * LANES
    ov = jnp.asarray(_overlap_np(nch, n_sel, ns_pad))
    kern = functools.partial(_cmp_select_kernel, qpos0=qpos0, n_sel=n_sel)
    part = lambda kv: pl.BlockSpec((None, None, nch, CMP_HID), lambda bi, g, qi: (bi, kv * C_KV_HEADS + g, 0, 0))
    return pl.pallas_call(
        kern,
        grid=(b, C_KV_HEADS, t // tq),
        in_specs=[pl.BlockSpec((None, tq, C_HPG * HEAD_DIM), lambda bi, g, qi: (bi, qi, g)),
                  part(0), part(0), part(1), part(1),
                  pl.BlockSpec((None, 2, CMP_HID), lambda bi, g, qi: (layer, 0, 0)),
                  pl.BlockSpec((None, 2, CMP_HID, HEAD_DIM), lambda bi, g, qi: (layer, 0, 0, 0)),
                  pl.BlockSpec((nch, ns_pad), lambda bi, g, qi: (0, 0))],
        out_specs=[pl.BlockSpec((None, tq, C_HPG * HEAD_DIM), lambda bi, g, qi: (bi, qi, g)),
                   pl.BlockSpec((None, None, tq, ns_pad), lambda bi, g, qi: (bi, g, qi, 0))],
        out_shape=[jax.ShapeDtypeStruct((b, t, C_WIDTH), F32),
                   jax.ShapeDtypeStruct((b, C_KV_HEADS, t, ns_pad), F32)],
        compiler_params=_cparams("parallel", "parallel", "arbitrary"),
        name="cmp_select",
    )(z, a, bm, a, bm, cmp_b1, cmp_w2, ov)


def _sel_prompt_kernel(q_ref, k_ref, v_ref, rev_ref, sel_ref, o_ref,
                       bias_ref, msk_ref, qb_ref, m_ref, l_ref, acc_ref):
    qi = pl.program_id(2)
    tq = q_ref.shape[0]
    nt = bias_ref.shape[0] - 1
    tk = 2 * tq
    ns_pad = sel_ref.shape[1]

    @pl.when(qi == 0)
    def _():
        bias_ref[0] = jnp.full((C_HPG, tq, tq), NEG, F32)
        for h in range(C_HPG):
            for delta in range(nt):
                off = (nt - 1 - delta) * tq
                bias_ref[delta + 1, h] = _toeplitz(rev_ref[h, :, off:off + 2 * tq], tq).T * LOG2E

    sel_t = sel_ref[...].T.astype(BF16)
    erow = lax.broadcasted_iota(jnp.int32, (tk, ns_pad), 0) // SEL_BLK
    ecol = lax.broadcasted_iota(jnp.int32, (tk, ns_pad), 1)
    for t in range(nt // 2):
        expand = (ecol == erow + t * (tk // SEL_BLK)).astype(BF16)
        hit = jnp.dot(expand, sel_t, preferred_element_type=F32)
        msk_ref[t] = jnp.where(hit > 0.5, 0.0, NEG)
    for h in range(C_HPG):
        qb_ref[h] = (q_ref[:, h * HEAD_DIM:(h + 1) * HEAD_DIM] * (SCALE * LOG2E)).astype(BF16)
    m_ref[...] = jnp.full(m_ref.shape, NEG, F32)
    l_ref[...] = jnp.zeros(l_ref.shape, F32)
    acc_ref[...] = jnp.zeros(acc_ref.shape, F32)

    def body(kj, carry):
        keys = pl.ds(pl.multiple_of(kj * tk, tk), tk)
        k = k_ref[keys, :].astype(BF16)
        v_t = v_ref[keys, :].T.astype(BF16)
        msk = msk_ref[kj]
        d0 = qi - 2 * kj
        for h in range(C_HPG):
            bias = jnp.concatenate([bias_ref[d0 + 1, h], bias_ref[d0, h]], axis=0)
            s = lax.dot_general(k, qb_ref[h], (((1,), (1,)), ((), ())), preferred_element_type=F32)
            s = s + bias + msk
            m_old = m_ref[h:h + 1, :]
            m_new = jnp.maximum(m_old, jnp.max(s, axis=0, keepdims=True))
            alpha = jnp.exp2(m_old - m_new)
            e = jnp.exp2(s - m_new)
            m_ref[h:h + 1, :] = m_new
            l_ref[h:h + 1, :] = alpha * l_ref[h:h + 1, :] + jnp.sum(e, axis=0, keepdims=True)
            acc_ref[h] = alpha * acc_ref[h] + jnp.dot(v_t, e.astype(BF16), preferred_element_type=F32)
        return carry

    lax.fori_loop(0, (qi + 2) // 2, body, 0)
    for h in range(C_HPG):
        o_ref[:, h * HEAD_DIM:(h + 1) * HEAD_DIM] = (acc_ref[h] / l_ref[h:h + 1, :]).T


SEL_TQ = 128


def _sel_attn_prompt(z, sel, rel_bias):
    b, t, _ = z.shape
    tq = SEL_TQ
    nt = t // tq
    assert nt % 2 == 0
    ns_pad = sel.shape[3]
    rev = _c_rev_table(rel_bias, t, tq, t)
    kcol = (ODD_KV0 + 2 * C_KV) // HEAD_DIM
    vcol = (ODD_KV0 + 3 * C_KV) // HEAD_DIM
    return pl.pallas_call(
        _sel_prompt_kernel,
        grid=(b, C_KV_HEADS, t // tq),
        in_specs=[pl.BlockSpec((None, tq, C_HPG * HEAD_DIM), lambda bi, g, qi: (bi, qi, g)),
                  pl.BlockSpec((None, t, HEAD_DIM), lambda bi, g, qi: (bi, 0, kcol + g)),
                  pl.BlockSpec((None, t, HEAD_DIM), lambda bi, g, qi: (bi, 0, vcol + g)),
                  pl.BlockSpec((C_HPG, 1, t + tq), lambda bi, g, qi: (g, 0, 0)),
                  pl.BlockSpec((None, None, tq, ns_pad), lambda bi, g, qi: (bi, g, qi, 0))],
        out_specs=pl.BlockSpec((None, tq, C_HPG * HEAD_DIM), lambda bi, g, qi: (bi, qi, g)),
        out_shape=jax.ShapeDtypeStruct((b, t, C_WIDTH), F32),
        scratch_shapes=[pltpu.VMEM((nt + 1, C_HPG, tq, tq), F32),
                        pltpu.VMEM((nt // 2, 2 * tq, tq), F32),
                        pltpu.VMEM((C_HPG, tq, HEAD_DIM), BF16),
                        pltpu.VMEM((C_HPG, tq), F32),
                        pltpu.VMEM((C_HPG, tq), F32),
                        pltpu.VMEM((C_HPG, HEAD_DIM, tq), F32)],
        compiler_params=_cparams("parallel", "parallel", "arbitrary"),
        name="sel_attn_prompt",
    )(z, z, z, rev, sel)


def _sel_paged_kernel(pt_ref, *refs, n_src, past):
    del pt_ref
    pages = refs[:n_src]
    q_ref, new_ref, sel_ref, bias_ref, bnew_ref, o_ref, m_ref, l_ref, acc_ref = refs[n_src:]
    pg = pl.program_id(1)
    tq = q_ref.shape[0]
    rows = C_HPG * tq
    page = pages[0].shape[0] // CACHE_ROW_KINDS
    keys = n_src * page
    ns_pad = sel_ref.shape[2]

    def page_rows(r, kind):
        return r[pl.ds(CACHE_KINDS + kind, page, stride=CACHE_ROW_KINDS), :]

    def q_rows(g):
        return jnp.concatenate(
            [(q_ref[:, (g * C_HPG + h) * HEAD_DIM:(g * C_HPG + h + 1) * HEAD_DIM] * SCALE).astype(BF16)
             for h in range(C_HPG)], axis=0)

    @pl.when(pg == 0)
    def _():
        for g in range(C_KV_HEADS):
            k = new_ref[:, g * HEAD_DIM:(g + 1) * HEAD_DIM].astype(BF16)
            v = new_ref[:, C_KV + g * HEAD_DIM:C_KV + (g + 1) * HEAD_DIM].astype(BF16)
            s = lax.dot_general(q_rows(g), k, (((1,), (1,)), ((), ())), preferred_element_type=F32)
            s = s + bnew_ref[g * C_HPG:(g + 1) * C_HPG].reshape(rows, tq)
            m = jnp.max(s, axis=-1, keepdims=True)
            e = jnp.exp(s - m)
            m_ref[g] = m
            l_ref[g] = jnp.sum(e, axis=-1, keepdims=True)
            acc_ref[g] = jnp.dot(e.astype(BF16), v, preferred_element_type=F32)

    srow = lax.broadcasted_iota(jnp.int32, (ns_pad, keys), 0)
    scol = lax.broadcasted_iota(jnp.int32, (ns_pad, keys), 1) // SEL_BLK + pg * (keys // SEL_BLK)
    expand = (srow == scol).astype(BF16)
    for g in range(C_KV_HEADS):
        k = jnp.concatenate([page_rows(r, g) for r in pages], axis=0).astype(BF16)
        v = jnp.concatenate([page_rows(r, C_KV_HEADS + g) for r in pages], axis=0).astype(BF16)
        hit = jnp.dot(sel_ref[g].astype(BF16), expand, preferred_element_type=F32)
        mask = jnp.where(hit > 0.5, 0.0, NEG)
        s = lax.dot_general(q_rows(g), k, (((1,), (1,)), ((), ())), preferred_element_type=F32)
        s = (s.reshape(C_HPG, tq, keys) + bias_ref[g * C_HPG:(g + 1) * C_HPG] + mask[None]).reshape(rows, keys)
        m_old = m_ref[g]
        m_new = jnp.maximum(m_old, jnp.max(s, axis=-1, keepdims=True))
        alpha = jnp.exp(m_old - m_new)
        e = jnp.exp(s - m_new)
        m_ref[g] = m_new
        l_ref[g] = alpha * l_ref[g] + jnp.sum(e, axis=-1, keepdims=True)
        acc_ref[g] = alpha * acc_ref[g] + jnp.dot(e.astype(BF16), v, preferred_element_type=F32)

    @pl.when(pg == pl.num_programs(1) - 1)
    def _():
        for g in range(C_KV_HEADS):
            o = acc_ref[g] / l_ref[g]
            for h in range(C_HPG):
                o_ref[:, (g * C_HPG + h) * HEAD_DIM:(g * C_HPG + h + 1) * HEAD_DIM] = o[h * tq:(h + 1) * tq]


def _sel_attn_paged(z, sel, cache, layer, page_table, rel_bias):
    b, tq, _ = z.shape
    n_pages = page_table.shape[1]
    page = cache.shape[2] // CACHE_ROW_KINDS
    past = n_pages * page
    pps = PAGES_PER_STEP
    keys = pps * page
    ns_pad = sel.shape[3]
    tab = _c_tab(rel_bias)
    rev = tab[_t5_bucket_np(past + tq - 1 - np.arange(past + tq - 1))].T.astype(F32)
    bias = jnp.stack([rev[:, tq - 1 - t:tq - 1 - t + past] for t in range(tq)], axis=1)
    dnew = np.arange(tq)[:, None] - np.arange(tq)[None, :]
    bnew = jnp.where(jnp.asarray(dnew >= 0)[None],
                     jnp.transpose(tab[_t5_bucket_np(np.clip(dnew, 0, None))], (2, 0, 1)), NEG).astype(F32)
    kern = functools.partial(_sel_paged_kernel, n_src=pps, past=past)

    def page_spec(j):
        return pl.BlockSpec((None, None, page * CACHE_ROW_KINDS, HEAD_DIM),
                            lambda bi, pg, pt: (layer, pt[bi, pg * pps + j], 0, 0))

    rows = C_HPG * tq
    grid_spec = pltpu.PrefetchScalarGridSpec(
        num_scalar_prefetch=1,
        grid=(b, n_pages // pps),
        in_specs=[page_spec(j) for j in range(pps)] + [
            pl.BlockSpec((None, tq, C_WIDTH), lambda bi, pg, pt: (bi, 0, 0)),
            pl.BlockSpec((None, tq, 2 * C_KV), lambda bi, pg, pt: (bi, 0, (ODD_KV0 + 2 * C_KV) // (2 * C_KV))),
            pl.BlockSpec((None, C_KV_HEADS, tq, ns_pad), lambda bi, pg, pt: (bi, 0, 0, 0)),
            pl.BlockSpec((C_HEADS, tq, keys), lambda bi, pg, pt: (0, 0, pg)),
            pl.BlockSpec((C_HEADS, tq, tq), lambda bi, pg, pt: (0, 0, 0))],
        out_specs=pl.BlockSpec((None, tq, C_WIDTH), lambda bi, pg, pt: (bi, 0, 0)),
        scratch_shapes=[pltpu.VMEM((C_KV_HEADS, rows, 1), F32), pltpu.VMEM((C_KV_HEADS, rows, 1), F32),
                        pltpu.VMEM((C_KV_HEADS, rows, HEAD_DIM), F32)])
    return pl.pallas_call(
        kern, grid_spec=grid_spec, out_shape=jax.ShapeDtypeStruct((b, tq, C_WIDTH), F32),
        compiler_params=_cparams("parallel", "arbitrary"),
        name="sel_attn_paged",
    )(page_table, *([cache] * pps), z, z, sel, bias, bnew)


def _win_attn_prompt(z, rel_bias, *, tq):
    b, t, _ = z.shape
    n_tiles = WIN // tq + 1
    rev = _c_rev_table(rel_bias, n_tiles * tq, tq, WIN)
    kv_blk = (ODD_KV0 + 4 * C_KV) // (2 * C_KV)
    kern = functools.partial(_tile_attn_kernel, nh=C_HEADS, rep=C_HPG, n_tiles=n_tiles, k_col=0, v_col=C_KV,
                             shared_kv=True, with_lse=False, lead_axis=1, toeplitz=True)

    def kv_spec(p):
        return pl.BlockSpec((None, tq, 2 * C_KV),
                            lambda bi, qi: (bi, jnp.maximum(qi - (n_tiles - 1 - p), 0), kv_blk))

    return pl.pallas_call(
        kern,
        grid=(b, t // tq),
        in_specs=[pl.BlockSpec((None, tq, C_WIDTH), lambda bi, qi: (bi, qi, 0))]
                 + [kv_spec(p) for p in range(n_tiles)]
                 + [pl.BlockSpec((C_HEADS, 1, (n_tiles + 1) * tq), lambda bi, qi: (0, 0, 0))],
        out_specs=pl.BlockSpec((None, tq, C_WIDTH), lambda bi, qi: (bi, qi, 0)),
        out_shape=jax.ShapeDtypeStruct((b, t, C_WIDTH), F32),
        scratch_shapes=[pltpu.VMEM((C_HEADS, tq, n_tiles * tq), F32)],
        compiler_params=_cparams("parallel", "arbitrary"),
        name="win_attn_prompt",
    )(z, *([z] * n_tiles), rev)


def _win_attn_sample(z, win_full, rel_bias):
    b, tq, _ = z.shape
    lk = win_full.shape[1]
    dist = (lk - tq) + np.arange(tq)[:, None] - np.arange(lk)[None, :]
    valid = (dist >= 0) & (dist <= WIN)
    bias = jnp.transpose(_c_tab(rel_bias)[_t5_bucket_np(np.clip(dist, 0, None))], (2, 0, 1))
    bias = jnp.where(jnp.asarray(valid)[None], bias, NEG).astype(F32)
    kern = functools.partial(_tile_attn_kernel, nh=C_HEADS, rep=C_HPG, n_tiles=1, k_col=0, v_col=C_KV,
                             shared_kv=True, with_lse=False, lead_axis=0)
    return pl.pallas_call(
        kern,
        grid=(b,),
        in_specs=[pl.BlockSpec((None, tq, C_WIDTH), lambda bi: (bi, 0, 0)),
                  pl.BlockSpec((None, lk, 2 * C_KV), lambda bi: (bi, 0, 0)),
                  pl.BlockSpec((C_HEADS, tq, lk), lambda bi: (0, 0, 0))],
        out_specs=pl.BlockSpec((None, tq, C_WIDTH), lambda bi: (bi, 0, 0)),
        out_shape=jax.ShapeDtypeStruct((b, tq, C_WIDTH), F32),
        compiler_params=_cparams("parallel"),
        name="win_attn_sample",
    )(z, win_full, bias)


def _odd_mix_kernel(oc_ref, os_ref, ow_ref, gl_ref, gp_ref, mix_ref):
    gates = jax.nn.sigmoid(gl_ref[...])
    branches = (oc_ref, os_ref, ow_ref)
    for h in range(C_HEADS):
        cs = slice(h * HEAD_DIM, (h + 1) * HEAD_DIM)
        acc = gates[:, h:h + 1] * branches[0][:, cs]
        for br in range(1, 3):
            lane = br * C_HEADS + h
            acc = acc + gates[:, lane:lane + 1] * branches[br][:, cs]
        mix_ref[:, cs] = (acc * _silu(gp_ref[:, cs])).astype(BF16)


def _odd_mix(oc, osel, ow, z2, zgp, *, tm):
    m = z2.shape[0]
    wide = pl.BlockSpec((tm, C_WIDTH), lambda i: (i, 0))
    return pl.pallas_call(
        _odd_mix_kernel,
        grid=(m // tm,),
        in_specs=[wide, wide, wide, pl.BlockSpec((tm, LANES), lambda i: (i, ODD_GL0 // LANES)), wide],
        out_specs=wide,
        out_shape=jax.ShapeDtypeStruct((m, C_WIDTH), BF16),
        compiler_params=_cparams("parallel"),
        name="odd_mix",
    )(oc, osel, ow, z2, zgp)


def _odd_layer(x, past, win_hist, rel_bias, layer, norm_g, w_in, w_in_gp, cmp_pos, cmp_w1, cmp_b1, cmp_w2, w_out,
               *, tm, tn_in, tn_out, tq):
    b, t, d = x.shape
    m = b * t
    x2 = x.reshape(m, d)
    z2 = _norm_matmul(x2, norm_g[layer], w_in, layer, tm=tm, tn=tn_in, n_out=ODD_Z)
    zgp = _norm_matmul(x2, norm_g[layer], w_in_gp, layer, tm=tm, tn=tn_in)
    z = z2.reshape(b, t, ODD_Z)
    rows = z[:, :, ODD_KV0:ODD_KV0 + 4 * C_KV].reshape(b, t, 4, C_KV_HEADS, HEAD_DIM)
    win_new = z[:, :, ODD_KV0 + 4 * C_KV:ODD_KV0 + 6 * C_KV]
    if past is None:
        length = t
        a, bm = _cmp_partials_prompt(z, layer, cmp_pos, cmp_w1)
        qpos0 = 0
    else:
        cache, page_table = past
        p_len = page_table.shape[1] * (cache.shape[2] // CACHE_ROW_KINDS)
        length = p_len + t
        assert length // CMP_STRIDE == p_len // CMP_STRIDE and p_len % SEL_BLK == 0
        a, bm = _cmp_partials_paged(cache, layer, page_table, cmp_pos, cmp_w1)
        qpos0 = p_len
    n_sel = -(-length // SEL_BLK)
    oc, sel = _cmp_select(z, a, bm, layer, cmp_b1, cmp_w2, tq=tq, qpos0=qpos0, n_sel=n_sel)
    if past is None:
        osel = _sel_attn_prompt(z, sel, rel_bias)
        ow = _win_attn_prompt(z, rel_bias, tq=tq)
        new_win = win_new[:, t - min(WIN, t):].reshape(b, min(WIN, t), 2, C_KV_HEADS, HEAD_DIM)
    else:
        osel = _sel_attn_paged(z, sel, cache, layer, page_table, rel_bias)
        win_full = jnp.concatenate([win_hist.reshape(b, -1, 2 * C_KV), win_new], axis=1)
        ow = _win_attn_sample(z, win_full, rel_bias)
        lk = win_full.shape[1]
        keep = min(WIN, length)
        new_win = win_full[:, lk - keep:].reshape(b, keep, 2, C_KV_HEADS, HEAD_DIM)
    mix = _odd_mix(oc.reshape(m, C_WIDTH), osel.reshape(m, C_WIDTH), ow.reshape(m, C_WIDTH), z2, zgp,
                   tm=min(tm, 256))
    y = _matmul_res([mix], w_out, layer, x2, tm=tm, tn=tn_out)
    return y.reshape(b, t, d), rows, new_win


def _run_trunk(x, a_caches, conv_state, c_cache, c_win, page_table, rel_bias, norm_even, w_in_even, conv_w,
               conv_b, conv_ln_g, conv_ln_b, conv_pw_w, conv_pw_b, w_out_even, norm_odd, w_in_odd, w_in_odd_gp,
               cmp_pos, cmp_w1, cmp_b1, cmp_w2, w_out_odd, final_norm, *, tm, tt, tq):
    b, t, d = x.shape
    prompt = a_caches is None
    depth = norm_even.shape[0] + norm_odd.shape[0]
    new_a = [[] for _ in A_GROUPS]
    a_new = None
    new_conv, new_rows, new_win = [], [], []
    for depth_i in range(depth):
        i = depth_i // 2
        if depth_i % 2 == 0:
            conv_hist = jnp.zeros((b, CONV_W - 1, B_WIDTH), F32) if prompt else conv_state[i]
            x, hists, conv = _even_layer(x, a_caches, a_new, conv_hist, rel_bias, i, norm_even, w_in_even, conv_w,
                                         conv_b, conv_ln_g, conv_ln_b, conv_pw_w, conv_pw_b,
                                         w_out_even, tm=tm, tn_in=512, tn_out=512, tt=tt)
            if prompt:
                for g in range(A_NG):
                    new_a[g].append(hists[g])
            else:
                a_new = hists
            new_conv.append(conv)
        else:
            past = None if prompt else (c_cache, page_table)
            x, rows, win = _odd_layer(x, past, None if prompt else c_win[i], rel_bias, i, norm_odd,
                                      w_in_odd, w_in_odd_gp, cmp_pos, cmp_w1, cmp_b1, cmp_w2,
                                      w_out_odd, tm=tm, tn_in=512, tn_out=512, tq=tq)
            new_rows.append(rows)
            new_win.append(win)
    y = _rms_norm(x.reshape(b * t, d), final_norm, tm=min(tm, 256)).reshape(b, t, d)
    if prompt:
        a_out = [jnp.stack(a) for a in new_a]
    else:
        a_out = [n.reshape(c.shape) for n, c in zip(a_new, a_caches)]
    return (y, a_out, jnp.stack(new_conv), jnp.stack(new_rows), jnp.stack(new_win))


def kernel(x_prompt, x_sample, cache_a_kv0, cache_a_kv1, cache_a_kv2, state_b_conv, cache_c_kv, cache_c_win, page_table, rel_bias, norm_even, w_in_even, conv_w, conv_b, conv_ln_g, conv_ln_b, conv_pw_w, conv_pw_b, w_out_even, norm_odd, w_in_odd, cmp_pos, cmp_w1, cmp_b1, cmp_w2, w_out_odd, final_norm):
    w_in_odd_gp = w_in_odd[:, :, ODD_GP0:]
    weights = (rel_bias, norm_even, w_in_even, conv_w, conv_b, conv_ln_g, conv_ln_b, conv_pw_w, conv_pw_b,
               w_out_even, norm_odd, w_in_odd, w_in_odd_gp, cmp_pos, cmp_w1, cmp_b1, cmp_w2, w_out_odd, final_norm)
    y_p, a_p, conv_p, rows_p, win_p = _run_trunk(x_prompt, None, None, None, None, None, *weights,
                                                 tm=2048, tt=256, tq=128)
    n_odd, n_pool, page = cache_c_kv.shape[:3]
    c_cache = cache_c_kv.reshape(n_odd, n_pool, page * CACHE_ROW_KINDS, HEAD_DIM)
    db, dt = x_sample.shape[:2]
    y_s, a_s, conv_s, rows_s, win_s = _run_trunk(x_sample, (cache_a_kv0, cache_a_kv1, cache_a_kv2), state_b_conv,
                                                 c_cache, cache_c_win, page_table, *weights,
                                                 tm=db * dt, tt=dt, tq=dt)
    return (y_p, y_s, a_p[0], a_p[1], a_p[2], conv_p, rows_p, win_p, a_s[0], a_s[1], a_s[2], conv_s, rows_s, win_s)
```

```python
import functools
import math

import numpy as np
import jax
import jax.numpy as jnp
from jax import lax
from jax.experimental import pallas as pl
from jax.experimental.pallas import tpu as pltpu

F32 = jnp.float32
BF16 = jnp.bfloat16

D_MODEL = 2048
HEAD_DIM = 128
LANES = 128
SUBLANES = 8
A_GROUPS = ((128, 1), (512, 4), (2048, 16))
A_NG = 3
A_HPG = 8
A_QKV = A_NG * A_HPG * HEAD_DIM
A_WIDTH = A_HPG * HEAD_DIM
B_WIDTH = 1024
CONV_W = 31
CONV_HALO = 32
EVEN_IN = 3 * A_QKV + A_WIDTH + 3 * B_WIDTH
C_HEADS = 16
C_KV_HEADS = 2
C_HPG = C_HEADS // C_KV_HEADS
C_WIDTH = C_HEADS * HEAD_DIM
C_KV = C_KV_HEADS * HEAD_DIM
CMP_BLK = 32
CMP_STRIDE = 16
CMP_HID = 128
SEL_BLK = 64
SEL_N = 16
WIN = 512
NUM_BUCKETS = 32
MAX_DIST = 2048
EPS = 1e-6
NEG = -1e30
SCALE = HEAD_DIM ** -0.5
LOG2E = math.log2(math.e)
VMEM_LIMIT = 56 * 1024 * 1024


def _cparams(*sem):
    return pltpu.CompilerParams(dimension_semantics=sem, vmem_limit_bytes=VMEM_LIMIT)


def _t5_bucket_np(dist):
    max_exact = NUM_BUCKETS // 2
    d = np.maximum(np.asarray(dist, np.int64), 0)
    ratio = np.log(np.maximum(d, 1).astype(np.float64) / max_exact) / math.log(MAX_DIST / max_exact)
    large = np.minimum(max_exact + (ratio * (NUM_BUCKETS - max_exact)).astype(np.int64), NUM_BUCKETS - 1)
    return np.where(d < max_exact, d, large).astype(np.int32)


def _silu(x):
    return x * jax.nn.sigmoid(x)


NORM_ROWS = 256


def _norm_matmul_kernel(x_ref, g_ref, w_ref, o_ref, xn_ref):
    @pl.when(pl.program_id(1) == 0)
    def _():
        rows = min(NORM_ROWS, x_ref.shape[0])

        def norm(c, carry):
            cur = pl.ds(pl.multiple_of(c * rows, rows), rows)
            x = x_ref[cur, :]
            ms = jnp.mean(x * x, axis=-1, keepdims=True)
            xn_ref[cur, :] = (x * lax.rsqrt(ms + EPS) * g_ref[...]).astype(BF16)
            return carry

        lax.fori_loop(0, x_ref.shape[0] // rows, norm, 0)

    o_ref[...] = jnp.dot(xn_ref[...], w_ref[...].astype(BF16), preferred_element_type=F32)


def _norm_matmul(x, g, w, layer, *, tm, tn, n_out=None):
    m, k = x.shape
    n = w.shape[2] if n_out is None else n_out
    assert n % tn == 0 and n <= w.shape[2]
    return pl.pallas_call(
        _norm_matmul_kernel,
        grid=(m // tm, n // tn),
        in_specs=[pl.BlockSpec((tm, k), lambda i, j: (i, 0), pipeline_mode=pl.Buffered(1)),
                  pl.BlockSpec((1, k), lambda i, j: (0, 0)),
                  pl.BlockSpec((None, k, tn), lambda i, j: (layer, 0, j))],
        out_specs=pl.BlockSpec((tm, tn), lambda i, j: (i, j)),
        out_shape=jax.ShapeDtypeStruct((m, n), F32),
        scratch_shapes=[pltpu.VMEM((tm, k), BF16)],
        compiler_params=_cparams("parallel", "arbitrary"),
        name="norm_matmul",
    )(x, g.reshape(1, k), w)


def _matmul_res_kernel(*refs, n_parts):
    a_refs = refs[:n_parts]
    w_ref, r_ref, o_ref = refs[n_parts:]
    acc = r_ref[...]
    k0 = 0
    for a_ref in a_refs:
        kp = a_ref.shape[1]
        acc = acc + jnp.dot(a_ref[...], w_ref[k0:k0 + kp, :].astype(BF16), preferred_element_type=F32)
        k0 += kp
    o_ref[...] = acc


def _matmul_res(parts, w, layer, res, *, tm, tn):
    m = res.shape[0]
    k, n = w.shape[1:]
    assert sum(p.shape[1] for p in parts) == k
    return pl.pallas_call(
        functools.partial(_matmul_res_kernel, n_parts=len(parts)),
        grid=(m // tm, n // tn),
        in_specs=[pl.BlockSpec((tm, p.shape[1]), lambda i, j: (i, 0), pipeline_mode=pl.Buffered(1))
                  for p in parts]
                 + [pl.BlockSpec((None, k, tn), lambda i, j: (layer, 0, j)),
                    pl.BlockSpec((tm, tn), lambda i, j: (i, j))],
        out_specs=pl.BlockSpec((tm, tn), lambda i, j: (i, j)),
        out_shape=jax.ShapeDtypeStruct((m, n), F32),
        compiler_params=_cparams("parallel", "arbitrary"),
        name="matmul_res",
    )(*parts, w, res)


def _rms_kernel(x_ref, g_ref, o_ref):
    x = x_ref[...]
    ms = jnp.mean(x * x, axis=-1, keepdims=True)
    o_ref[...] = x * lax.rsqrt(ms + EPS) * g_ref[...]


def _rms_norm(x, g, *, tm):
    m, k = x.shape
    return pl.pallas_call(
        _rms_kernel,
        grid=(m // tm,),
        in_specs=[pl.BlockSpec((tm, k), lambda i: (i, 0)), pl.BlockSpec((1, k), lambda i: (0, 0))],
        out_specs=pl.BlockSpec((tm, k), lambda i: (i, 0)),
        out_shape=jax.ShapeDtypeStruct((m, k), F32),
        compiler_params=_cparams("parallel"),
        name="rms_norm",
    )(x, g.reshape(1, k))


def _c_tab(rel_bias):
    return rel_bias[:, :C_HEADS]


def _c_rev_table(rel_bias, length, tq, max_dist):
    dist = length - np.arange(length + tq)
    valid = (dist >= 0) & (dist <= max_dist)
    tab = _c_tab(rel_bias)[_t5_bucket_np(np.clip(dist, 0, None))]
    tab = jnp.where(jnp.asarray(valid)[:, None], tab, NEG).astype(F32)
    return tab.T.reshape(C_HEADS, 1, length + tq)


def _toeplitz(vec, tq):
    w = vec.shape[1]
    return pltpu.roll(jnp.broadcast_to(vec, (tq, w)), 0, 1, stride=1, stride_axis=0)[:, tq:]


def _tile_attn_kernel(*refs, nh, rep, n_tiles, k_col, v_col, shared_kv, with_lse, lead_axis, toeplitz=False):
    if toeplitz:
        rev_ref, bias_scr = refs[-1 - int(with_lse) - 2], refs[-1]
        tq_ = refs[0].shape[0]

        @pl.when(pl.program_id(lead_axis) == 0)
        def _():
            for h in range(nh):
                bias_scr[h] = _toeplitz(rev_ref[h], tq_)

        refs = list(refs[:-1])
        refs[-1 - int(with_lse) - 1] = bias_scr
    q_ref = refs[0]
    if shared_kv:
        k_refs = v_refs = refs[1:1 + n_tiles]
        nxt = 1 + n_tiles
    else:
        k_refs = refs[1:1 + n_tiles]
        v_refs = refs[1 + n_tiles:1 + 2 * n_tiles]
        nxt = 1 + 2 * n_tiles
    b_ref = refs[nxt]
    o_ref = refs[nxt + 1]
    lse_ref = refs[nxt + 2] if with_lse else None
    tq = q_ref.shape[0]
    tk = k_refs[0].shape[0]
    lk = n_tiles * tk
    if n_tiles > 1:
        qi = pl.program_id(lead_axis)
        col = lax.broadcasted_iota(jnp.int32, (tq, lk), 1)
        pad_mask = jnp.where(col < (n_tiles - 1 - qi) * tk, NEG, 0.0).astype(F32)
    else:
        pad_mask = None
    if with_lse:
        lane = lax.broadcasted_iota(jnp.int32, (tq, LANES), 1)
        lse_t = jnp.zeros((tq, LANES), F32)
    for j in range(nh // rep):
        kc = k_col + j * HEAD_DIM
        vc = v_col + j * HEAD_DIM
        heads = range(j * rep, (j + 1) * rep)
        q = jnp.concatenate([(q_ref[:, h * HEAD_DIM:(h + 1) * HEAD_DIM] * SCALE).astype(BF16) for h in heads], axis=0)
        if n_tiles > 1:
            k = jnp.concatenate([r[:, kc:kc + HEAD_DIM] for r in k_refs], axis=0).astype(BF16)
            v = jnp.concatenate([r[:, vc:vc + HEAD_DIM] for r in v_refs], axis=0).astype(BF16)
        else:
            k = k_refs[0][:, kc:kc + HEAD_DIM].astype(BF16)
            v = v_refs[0][:, vc:vc + HEAD_DIM].astype(BF16)
        s = lax.dot_general(q, k, (((1,), (1,)), ((), ())), preferred_element_type=F32)
        s = s.reshape(rep, tq, lk) + b_ref[j * rep:(j + 1) * rep]
        if pad_mask is not None:
            s = s + pad_mask[None]
        s = s.reshape(rep * tq, lk)
        m = jnp.max(s, axis=-1, keepdims=True)
        e = jnp.exp(s - m)
        den = jnp.sum(e, axis=-1, keepdims=True)
        o = jnp.dot(e.astype(BF16), v, preferred_element_type=F32) / den
        for r, h in enumerate(heads):
            o_ref[:, h * HEAD_DIM:(h + 1) * HEAD_DIM] = o[r * tq:(r + 1) * tq]
        if with_lse:
            lse = m + jnp.log(den)
            for r, h in enumerate(heads):
                lse_t = jnp.where(lane == h, lse[r * tq:(r + 1) * tq], lse_t)
    if with_lse:
        lse_ref[...] = lse_t


A_TU = 128


def _a_rev_table(rel_bias):
    j = 2 * A_TU - np.arange(3 * A_TU)
    rows = []
    for g, (win, dil) in enumerate(A_GROUPS):
        valid = (j >= 0) & (j <= win // dil)
        tab = rel_bias[:, g * A_HPG:(g + 1) * A_HPG][_t5_bucket_np(np.clip(j, 0, None) * dil)]
        rows.append(jnp.where(jnp.asarray(valid)[:, None], tab, NEG).T)
    return jnp.stack(rows, axis=1).astype(F32)


def _a_prompt_kernel(q0_ref, q1_ref, q2_ref, k0_ref, k1_ref, k2_ref, v0_ref, v1_ref, v2_ref, ga_ref, rev_ref,
                     mix_ref, o_scr, l_scr, bias_scr):
    t = q0_ref.shape[0]
    tu = A_TU
    q_refs, k_refs, v_refs = (q0_ref, q1_ref, q2_ref), (k0_ref, k1_ref, k2_ref), (v0_ref, v1_ref, v2_ref)
    for g in range(A_NG):
        bias_scr[g] = _toeplitz(rev_ref[g:g + 1, :], tu)
    col = lax.broadcasted_iota(jnp.int32, (tu, 2 * tu), 1)
    for g, (win, dil) in enumerate(A_GROUPS):
        n_u = t // (dil * tu)
        q_ref, k_ref, v_ref = q_refs[g], k_refs[g], v_refs[g]

        def rows(r, u, dil=dil):
            start = r + u * (tu * dil)
            if dil == 1:
                return pl.ds(pl.multiple_of(start, tu), tu)
            return pl.ds(start, tu, stride=dil)

        def body(it, carry, g=g, n_u=n_u, q_ref=q_ref, k_ref=k_ref, v_ref=v_ref, rows=rows):
            r = it // n_u
            u = it % n_u
            cur = rows(r, u)
            q = (q_ref[cur, :] * SCALE).astype(BF16)
            if n_u > 1:
                prev = rows(r, jnp.maximum(u - 1, 0))
                k = jnp.concatenate([k_ref[prev, :], k_ref[cur, :]], axis=0).astype(BF16)
                v = jnp.concatenate([v_ref[prev, :], v_ref[cur, :]], axis=0).astype(BF16)
                bias = bias_scr[g] + jnp.where((col < tu) & (u == 0), NEG, 0.0)
            else:
                k = k_ref[cur, :].astype(BF16)
                v = v_ref[cur, :].astype(BF16)
                bias = bias_scr[g][:, tu:]
            s = lax.dot_general(q, k, (((1,), (1,)), ((), ())), preferred_element_type=F32) + bias
            m = jnp.max(s, axis=-1, keepdims=True)
            e = jnp.exp(s - m)
            den = jnp.sum(e, axis=-1, keepdims=True)
            o_scr[g, cur, :] = jnp.dot(e.astype(BF16), v, preferred_element_type=F32) / den
            l_scr[g, cur, :] = jnp.broadcast_to(m + jnp.log(den), (tu, HEAD_DIM))
            return carry

        lax.fori_loop(0, t // tu, body, 0, unroll=8)

    def combine(c, carry):
        cur = pl.ds(pl.multiple_of(c * tu, tu), tu)
        ls = [l_scr[g, cur, :] for g in range(A_NG)]
        m = jnp.maximum(jnp.maximum(ls[0], ls[1]), ls[2])
        es = [jnp.exp(l - m) for l in ls]
        acc = es[0] * o_scr[0, cur, :] + es[1] * o_scr[1, cur, :] + es[2] * o_scr[2, cur, :]
        out = acc / (es[0] + es[1] + es[2])
        mix_ref[cur, :] = (out * _silu(ga_ref[cur, :])).astype(BF16)
        return carry

    lax.fori_loop(0, t // tu, combine, 0)


def _a_attn_prompt(z, rel_bias):
    b, t, n = z.shape
    for win, dil in A_GROUPS:
        assert win // dil == A_TU and t % (dil * A_TU) == 0
    rev = _a_rev_table(rel_bias)
    nq = A_QKV // HEAD_DIM
    col = lambda base: pl.BlockSpec((None, t, HEAD_DIM), lambda bi, h: (bi, 0, base + h))
    return pl.pallas_call(
        _a_prompt_kernel,
        grid=(b, A_HPG),
        in_specs=[col(g * A_HPG) for g in range(A_NG)]
                 + [col(nq + g * A_HPG) for g in range(A_NG)]
                 + [col(2 * nq + g * A_HPG) for g in range(A_NG)]
                 + [col(3 * nq), pl.BlockSpec((None, A_NG, 3 * A_TU), lambda bi, h: (h, 0, 0))],
        out_specs=pl.BlockSpec((None, t, HEAD_DIM), lambda bi, h: (bi, 0, h)),
        out_shape=jax.ShapeDtypeStruct((b, t, A_WIDTH), BF16),
        scratch_shapes=[pltpu.VMEM((A_NG, t, HEAD_DIM), F32), pltpu.VMEM((A_NG, t, HEAD_DIM), F32),
                        pltpu.VMEM((A_NG, A_TU, 2 * A_TU), F32)],
        compiler_params=_cparams("parallel", "parallel"),
        name="a_attn_prompt",
    )(*([z] * 10), rev)


def _a_bias_sample(rel_bias, g, hist_len, tq):
    win, dil = A_GROUPS[g]
    lk = hist_len + tq
    dist = hist_len + tq - 1 - np.arange(lk + tq - 1)
    valid = (dist >= 0) & (dist <= win) & (dist % dil == 0)
    tab = rel_bias[:, g * A_HPG:(g + 1) * A_HPG][_t5_bucket_np(np.clip(dist, 0, None))]
    rev = jnp.where(jnp.asarray(valid)[:, None], tab, NEG).T.astype(F32)
    return jnp.stack([rev[:, tq - 1 - t:tq - 1 - t + lk] for t in range(tq)], axis=1)


A_ROWS = 2 * A_HPG
A_CHUNK = 512


def _a_sample_kernel(*refs, aliased):
    q_ref, kn_ref, vn_ref, old_ref, head_ref, bias_ref, bnew_ref = refs[:7]
    o_ref, lse_ref, new_ref, m_ref, l_ref, acc_ref = refs[7 + int(aliased):]
    c = pl.program_id(1)
    last = pl.num_programs(1) - 1
    tq = q_ref.shape[0]
    n_tok = old_ref.shape[0] // A_ROWS
    shift = tq * A_ROWS

    @pl.when(c == 0)
    def _():
        m_ref[...] = jnp.full(m_ref.shape, NEG, F32)
        l_ref[...] = jnp.zeros(l_ref.shape, F32)
        acc_ref[...] = jnp.zeros(acc_ref.shape, F32)

    def update(h, q, k, v, bias):
        s = lax.dot_general(q, k, (((1,), (1,)), ((), ())), preferred_element_type=F32) + bias
        m_old = m_ref[h]
        m_new = jnp.maximum(m_old, jnp.max(s, axis=-1, keepdims=True))
        alpha = jnp.exp(m_old - m_new)
        e = jnp.exp(s - m_new)
        m_ref[h] = m_new
        l_ref[h] = alpha * l_ref[h] + jnp.sum(e, axis=-1, keepdims=True)
        acc_ref[h] = alpha * acc_ref[h] + jnp.dot(e.astype(BF16), v, preferred_element_type=F32)

    qs = [(q_ref[:, h * HEAD_DIM:(h + 1) * HEAD_DIM] * SCALE).astype(BF16) for h in range(A_HPG)]
    for h in range(A_HPG):
        k = old_ref[pl.ds(h, n_tok, stride=A_ROWS), :].astype(BF16)
        v = old_ref[pl.ds(A_HPG + h, n_tok, stride=A_ROWS), :].astype(BF16)
        update(h, qs[h], k, v, bias_ref[h])

    new_ref[0:n_tok * A_ROWS - shift, :] = old_ref[shift:, :]

    @pl.when(c < last)
    def _():
        new_ref[n_tok * A_ROWS - shift:, :] = head_ref[...]

    @pl.when(c == last)
    def _():
        lane = lax.broadcasted_iota(jnp.int32, (tq, LANES), 1)
        lse_t = jnp.zeros((tq, LANES), F32)
        for h in range(A_HPG):
            cs = slice(h * HEAD_DIM, (h + 1) * HEAD_DIM)
            update(h, qs[h], kn_ref[:, cs].astype(BF16), vn_ref[:, cs].astype(BF16), bnew_ref[h])
            o_ref[:, cs] = acc_ref[h] / l_ref[h]
            lse_t = jnp.where(lane == h, m_ref[h] + jnp.log(l_ref[h]), lse_t)
            base = n_tok * A_ROWS - shift
            new_ref[pl.ds(base + h, tq, stride=A_ROWS), :] = kn_ref[:, cs]
            new_ref[pl.ds(base + A_HPG + h, tq, stride=A_ROWS), :] = vn_ref[:, cs]
        lse_ref[...] = lse_t


def _a_attn_sample(z, cache, layer, new_cache, rel_bias, g):
    b, tq, n = z.shape
    n_layers, _, w = cache.shape[:3]
    assert cache.shape[3:] == (2, A_HPG, HEAD_DIM) and w % LANES == 0 and tq * A_ROWS == LANES
    ch = min(w, A_CHUNK)
    n_ch = w // ch
    old = cache.reshape(n_layers, b, w * A_ROWS, HEAD_DIM)
    bias_full = _a_bias_sample(rel_bias, g, w, tq)
    bias = jnp.stack([bias_full[:, :, i * ch:(i + 1) * ch] for i in range(n_ch)])
    bnew = bias_full[:, :, w:]
    nq = A_QKV // A_WIDTH
    zcol = lambda cb: pl.BlockSpec((None, tq, A_WIDTH), lambda bi, c: (bi, 0, cb))
    heads_per_chunk = ch * A_ROWS // LANES
    in_specs = [zcol(g), zcol(nq + g), zcol(2 * nq + g),
                pl.BlockSpec((None, None, ch * A_ROWS, HEAD_DIM), lambda bi, c: (layer, bi, c, 0)),
                pl.BlockSpec((None, None, LANES, HEAD_DIM),
                             lambda bi, c: (layer, bi, jnp.minimum(c + 1, n_ch - 1) * heads_per_chunk, 0)),
                pl.BlockSpec((None, A_HPG, tq, ch), lambda bi, c: (c, 0, 0, 0)),
                pl.BlockSpec((A_HPG, tq, tq), lambda bi, c: (0, 0, 0))]
    args = [z, z, z, old, old, bias, bnew]
    aliases = {}
    if new_cache is not None:
        in_specs.append(pl.BlockSpec(memory_space=pl.ANY))
        args.append(new_cache)
        aliases = {len(args) - 1: 2}
    o, lse, new = pl.pallas_call(
        functools.partial(_a_sample_kernel, aliased=new_cache is not None),
        grid=(b, n_ch),
        in_specs=in_specs,
        out_specs=[pl.BlockSpec((None, tq, A_WIDTH), lambda bi, c: (bi, 0, 0)),
                   pl.BlockSpec((None, tq, LANES), lambda bi, c: (bi, 0, 0)),
                   pl.BlockSpec((None, None, ch * A_ROWS, HEAD_DIM), lambda bi, c: (layer, bi, c, 0))],
        out_shape=[jax.ShapeDtypeStruct((b, tq, A_WIDTH), F32),
                   jax.ShapeDtypeStruct((b, tq, LANES), F32),
                   jax.ShapeDtypeStruct(old.shape, F32)],
        scratch_shapes=[pltpu.VMEM((A_HPG, tq, 1), F32), pltpu.VMEM((A_HPG, tq, 1), F32),
                        pltpu.VMEM((A_HPG, tq, HEAD_DIM), F32)],
        input_output_aliases=aliases,
        compiler_params=_cparams("parallel", "arbitrary"),
        name=f"a_attn_sample_g{g}",
    )(*args)
    return o, lse, new


def _conv_kernel(a_ref, gt_ref, gb_ref, hist_ref, cw_ref, cb_ref, lg_ref, lb_ref, pw_ref, pb_ref,
                 o_ref, nc_ref, ubuf, ybuf, pwb):
    ti = pl.program_id(1)
    nt = pl.num_programs(1)
    tt = a_ref.shape[0]
    nhist = CONV_W - 1
    pad = CONV_HALO - nhist

    @pl.when(ti == 0)
    def _():
        pwb[...] = pw_ref[...].astype(BF16)
        ubuf[0:pad, :] = jnp.zeros((pad, B_WIDTH), F32)
        ubuf[pad:CONV_HALO, :] = hist_ref[...]

    ubuf[CONV_HALO:CONV_HALO + tt, :] = a_ref[...] * jax.nn.sigmoid(gt_ref[...])
    for c in range(B_WIDTH // LANES):
        cs = slice(c * LANES, (c + 1) * LANES)
        acc = jnp.zeros((tt, LANES), F32) + cb_ref[:, cs]
        for r in range(SUBLANES):
            taps = [a for a in range(CONV_HALO // SUBLANES + 1) if 0 <= SUBLANES * a + r - pad < CONV_W]
            span = tt if r == 0 else tt + SUBLANES
            z = None
            for a in taps:
                k = SUBLANES * a + r - pad
                term = ubuf[SUBLANES * a:SUBLANES * a + span, cs] * cw_ref[k:k + 1, cs]
                z = term if z is None else z + term
            acc = acc + (z if r == 0 else z[r:r + tt])
        ybuf[:, cs] = acc
    y = ybuf[...]
    mu = jnp.mean(y, axis=-1, keepdims=True)
    yc = y - mu
    var = jnp.mean(yc * yc, axis=-1, keepdims=True)
    yn = yc * lax.rsqrt(var + EPS) * lg_ref[...] + lb_ref[...]
    act = _silu(yn).astype(BF16)
    ob = jnp.dot(act, pwb[...], preferred_element_type=F32) + pb_ref[...]
    o_ref[...] = (ob * _silu(gb_ref[...])).astype(BF16)

    @pl.when(ti == nt - 1)
    def _():
        nc_ref[...] = ubuf[tt + pad:tt + CONV_HALO, :]

    @pl.when(ti < nt - 1)
    def _():
        ubuf[0:CONV_HALO, :] = ubuf[tt:tt + CONV_HALO, :]


def _conv_module(z, hist, layer, conv_w, conv_b, ln_g, ln_b, pw_w, pw_b, *, tt):
    b, t, n = z.shape
    assert t % tt == 0 and (tt >= CONV_HALO or t == tt)
    glu0 = (3 * A_QKV + A_WIDTH) // B_WIDTH
    row = lambda a: a.reshape(a.shape[0], 1, B_WIDTH)
    blk = (None, tt, B_WIDTH)
    lyr = lambda shape: pl.BlockSpec((None,) + shape, lambda bi, ti: (layer,) + (0,) * len(shape))
    return pl.pallas_call(
        _conv_kernel,
        grid=(b, t // tt),
        in_specs=[pl.BlockSpec(blk, lambda bi, ti: (bi, ti, glu0)),
                  pl.BlockSpec(blk, lambda bi, ti: (bi, ti, glu0 + 1)),
                  pl.BlockSpec(blk, lambda bi, ti: (bi, ti, glu0 + 2)),
                  pl.BlockSpec((None, CONV_W - 1, B_WIDTH), lambda bi, ti: (bi, 0, 0)),
                  lyr((CONV_W, B_WIDTH)), lyr((1, B_WIDTH)), lyr((1, B_WIDTH)), lyr((1, B_WIDTH)),
                  lyr((B_WIDTH, B_WIDTH)), lyr((1, B_WIDTH))],
        out_specs=[pl.BlockSpec(blk, lambda bi, ti: (bi, ti, 0)),
                   pl.BlockSpec((None, CONV_W - 1, B_WIDTH), lambda bi, ti: (bi, 0, 0))],
        out_shape=[jax.ShapeDtypeStruct((b, t, B_WIDTH), BF16),
                   jax.ShapeDtypeStruct((b, CONV_W - 1, B_WIDTH), F32)],
        scratch_shapes=[pltpu.VMEM((CONV_HALO + tt, B_WIDTH), F32),
                        pltpu.VMEM((tt, B_WIDTH), F32),
                        pltpu.VMEM((B_WIDTH, B_WIDTH), BF16)],
        compiler_params=_cparams("parallel", "arbitrary"),
        name="conv_module",
    )(z, z, z, hist, conv_w, row(conv_b), row(ln_g), row(ln_b), pw_w, row(pw_b))


def _even_mix_kernel(o0_ref, o1_ref, o2_ref, l0_ref, l1_ref, l2_ref, ga_ref, mix_ref):
    ls = [l0_ref[...], l1_ref[...], l2_ref[...]]
    m = jnp.maximum(jnp.maximum(ls[0], ls[1]), ls[2])
    es = [jnp.exp(l - m) for l in ls]
    inv = 1.0 / (es[0] + es[1] + es[2])
    ws = [e * inv for e in es]
    o_refs = (o0_ref, o1_ref, o2_ref)
    for h in range(A_HPG):
        cs = slice(h * HEAD_DIM, (h + 1) * HEAD_DIM)
        acc = ws[0][:, h:h + 1] * o_refs[0][:, cs]
        for g in range(1, A_NG):
            acc = acc + ws[g][:, h:h + 1] * o_refs[g][:, cs]
        mix_ref[:, cs] = (acc * _silu(ga_ref[:, cs])).astype(BF16)


def _even_mix(oas, lses, z2, *, tm):
    m = z2.shape[0]
    ga_blk = 3 * A_QKV // A_WIDTH
    wide = lambda c: pl.BlockSpec((tm, A_WIDTH), lambda i: (i, c))
    narrow = pl.BlockSpec((tm, LANES), lambda i: (i, 0))
    return pl.pallas_call(
        _even_mix_kernel,
        grid=(m // tm,),
        in_specs=[wide(0), wide(0), wide(0), narrow, narrow, narrow, wide(ga_blk)],
        out_specs=wide(0),
        out_shape=jax.ShapeDtypeStruct((m, A_WIDTH), BF16),
        compiler_params=_cparams("parallel"),
        name="even_mix",
    )(*oas, *lses, z2)


def _even_layer(x, a_caches, a_new, conv_hist, rel_bias, layer, norm_g, w_in, conv_w, conv_b, ln_g, ln_b, pw_w,
                pw_b, w_out, *, tm, tn_in, tn_out, tt):
    b, t, d = x.shape
    m = b * t
    x2 = x.reshape(m, d)
    z2 = _norm_matmul(x2, norm_g[layer], w_in, layer, tm=tm, tn=tn_in)
    z = z2.reshape(b, t, EVEN_IN)
    oas, lses, new_hists = [], [], []
    for g, (win, _) in enumerate(A_GROUPS):
        if a_caches is None:
            kcol = A_QKV + g * A_WIDTH
            vcol = 2 * A_QKV + g * A_WIDTH
            new_kv = jnp.stack([z[:, :, kcol:kcol + A_WIDTH], z[:, :, vcol:vcol + A_WIDTH]], axis=2)
            new_hists.append(new_kv[:, t - min(win, t):].reshape(b, min(win, t), 2, A_HPG, HEAD_DIM))
        else:
            assert a_caches[g].shape[2] == win
            o, lse, new = _a_attn_sample(z, a_caches[g], layer, None if a_new is None else a_new[g], rel_bias, g)
            new_hists.append(new)
            oas.append(o.reshape(m, A_WIDTH))
            lses.append(lse.reshape(m, LANES))
    if a_caches is None:
        mix_a = _a_attn_prompt(z, rel_bias).reshape(m, A_WIDTH)
    else:
        mix_a = _even_mix(oas, lses, z2, tm=min(tm, 256))
    mix_b, new_conv = _conv_module(z, conv_hist, layer, conv_w, conv_b, ln_g, ln_b, pw_w, pw_b, tt=tt)
    y = _matmul_res([mix_a, mix_b.reshape(m, B_WIDTH)], w_out, layer, x2, tm=tm, tn=tn_out)
    return y.reshape(b, t, d), new_hists, new_conv


ODD_KV0 = C_WIDTH
ODD_GL0 = C_WIDTH + 6 * C_KV
ODD_GP0 = ODD_GL0 + 3 * C_HEADS
ODD_Z = ODD_GL0 + 512
CMP_HALF = CMP_BLK // 2
CMP_K = CMP_HALF * HEAD_DIM
PAGES_PER_STEP = 16
CACHE_KINDS = 2 * C_KV_HEADS
CACHE_ROW_KINDS = 2 * CACHE_KINDS


def _cmp_partials_kernel(*refs, n_src, n_prefetch, interleaved):
    refs = refs[n_prefetch:]
    src = refs[:n_src]
    pos_ref, w1_ref, a_ref, b_ref, rows_ref, w1b_ref = refs[n_src:n_src + 6]
    n = rows_ref.shape[1] // CMP_HALF

    if n_prefetch:
        @pl.when(pl.program_id(1) == 0)
        def _():
            w1b_ref[...] = w1_ref[...].astype(BF16)
    else:
        w1b_ref[...] = w1_ref[...].astype(BF16)

    r0 = 0
    for r in src:
        nr = r.shape[0] // CACHE_ROW_KINDS if interleaved else r.shape[0]
        for c in range(CACHE_KINDS):
            if interleaved:
                rows_ref[c, r0:r0 + nr, :] = r[pl.ds(c, nr, stride=CACHE_ROW_KINDS), :]
            else:
                rows_ref[c, r0:r0 + nr, :] = r[:, c * HEAD_DIM:(c + 1) * HEAD_DIM]
        r0 += nr
    for kv in range(2):
        x = jnp.concatenate(
            [jnp.concatenate([rows_ref[kv * C_KV_HEADS + g, pl.ds(l, n, stride=CMP_HALF), :]
                              for l in range(CMP_HALF)], axis=1) for g in range(C_KV_HEADS)], axis=0)
        for half, out in ((0, a_ref), (1, b_ref)):
            y = jnp.dot((x + pos_ref[kv, half]).astype(BF16), w1b_ref[kv, half], preferred_element_type=F32)
            for g in range(C_KV_HEADS):
                out[kv * C_KV_HEADS + g] = y[g * n:(g + 1) * n]


def _cmp_weights(cmp_pos, cmp_w1):
    n = cmp_pos.shape[0]
    return cmp_pos.reshape(n, 2, 2, 1, CMP_K), cmp_w1.reshape(n, 2, 2, CMP_K, CMP_HID)


def _cmp_partials_prompt(z, layer, cmp_pos, cmp_w1):
    b, t, _ = z.shape
    nch = t // CMP_STRIDE
    pos, w1 = _cmp_weights(cmp_pos, cmp_w1)
    kern = functools.partial(_cmp_partials_kernel, n_src=1, n_prefetch=0, interleaved=False)
    out = jax.ShapeDtypeStruct((b, 4, nch, CMP_HID), F32)
    ospec = pl.BlockSpec((None, 4, nch, CMP_HID), lambda bi: (bi, 0, 0, 0))
    return pl.pallas_call(
        kern,
        grid=(b,),
        in_specs=[pl.BlockSpec((None, t, 2 * C_KV), lambda bi: (bi, 0, ODD_KV0 // (2 * C_KV))),
                  pl.BlockSpec((None,) + pos.shape[1:], lambda bi: (layer, 0, 0, 0, 0)),
                  pl.BlockSpec((None,) + w1.shape[1:], lambda bi: (layer, 0, 0, 0, 0))],
        out_specs=[ospec, ospec],
        out_shape=[out, out],
        scratch_shapes=[pltpu.VMEM((2 * C_KV_HEADS, t, HEAD_DIM), F32), pltpu.VMEM(w1.shape[1:], BF16)],
        compiler_params=_cparams("parallel"),
        name="cmp_partials_prompt",
    )(z, pos, w1)


def _cmp_partials_paged(cache, layer, page_table, cmp_pos, cmp_w1):
    b, n_pages = page_table.shape
    page = cache.shape[2] // CACHE_ROW_KINDS
    pps = PAGES_PER_STEP
    assert n_pages % pps == 0 and page % CMP_STRIDE == 0
    nch_step = pps * page // CMP_STRIDE
    nch = n_pages * page // CMP_STRIDE
    pos, w1 = _cmp_weights(cmp_pos, cmp_w1)
    kern = functools.partial(_cmp_partials_kernel, n_src=pps, n_prefetch=1, interleaved=True)
    out = jax.ShapeDtypeStruct((b, 4, nch, CMP_HID), F32)
    ospec = pl.BlockSpec((None, 4, nch_step, CMP_HID), lambda bi, pg, pt: (bi, 0, pg, 0))

    def page_spec(j):
        return pl.BlockSpec((None, None, page * CACHE_ROW_KINDS, HEAD_DIM),
                            lambda bi, pg, pt: (layer, pt[bi, pg * pps + j], 0, 0))

    grid_spec = pltpu.PrefetchScalarGridSpec(
        num_scalar_prefetch=1,
        grid=(b, n_pages // pps),
        in_specs=[page_spec(j) for j in range(pps)] + [
            pl.BlockSpec((None,) + pos.shape[1:], lambda bi, pg, pt: (layer, 0, 0, 0, 0)),
            pl.BlockSpec((None,) + w1.shape[1:], lambda bi, pg, pt: (layer, 0, 0, 0, 0))],
        out_specs=[ospec, ospec],
        scratch_shapes=[pltpu.VMEM((2 * C_KV_HEADS, pps * page, HEAD_DIM), F32), pltpu.VMEM(w1.shape[1:], BF16)])
    return pl.pallas_call(
        kern, grid_spec=grid_spec, out_shape=[out, out],
        compiler_params=_cparams("parallel", "arbitrary"),
        name="cmp_partials_paged",
    )(page_table, *([cache] * pps), pos, w1)


def _overlap_np(n_ch, n_sel, ns_pad):
    ci = np.arange(n_ch)[:, None]
    si = np.arange(ns_pad)[None, :]
    ov = (ci * CMP_STRIDE < (si + 1) * SEL_BLK) & (ci * CMP_STRIDE + CMP_BLK > si * SEL_BLK)
    ov &= (ci < n_ch - 1) & (si < n_sel)
    return ov.astype(np.float32)


def _cmp_select_kernel(q_ref, ak_ref, bk_ref, av_ref, bv_ref, b1_ref, w2_ref, ov_ref, oc_ref, sel_ref,
                       *, qpos0, n_sel):
    qi = pl.program_id(2)
    tq = q_ref.shape[0]
    nch = ak_ref.shape[0]
    ns_pad = ov_ref.shape[1]

    def finish(a_ref, b_ref, kv):
        hid = a_ref[...] + pltpu.roll(b_ref[...], nch - 1, 0) + b1_ref[kv:kv + 1, :]
        return jnp.dot(_silu(hid).astype(BF16), w2_ref[kv].astype(BF16),
                       preferred_element_type=F32).astype(BF16)

    k_cmp = finish(ak_ref, bk_ref, 0)
    v_cmp = finish(av_ref, bv_ref, 1)
    qpos = qpos0 + qi * tq + lax.broadcasted_iota(jnp.int32, (tq, nch), 0)
    cmp_end = lax.broadcasted_iota(jnp.int32, (tq, nch), 1) * CMP_STRIDE + (CMP_BLK - 1)
    ok = (cmp_end <= qpos) & (cmp_end < (nch - 1) * CMP_STRIDE + CMP_BLK - 1)
    q = jnp.concatenate([(q_ref[:, h * HEAD_DIM:(h + 1) * HEAD_DIM] * SCALE).astype(BF16) for h in range(C_HPG)],
                        axis=0)
    s = lax.dot_general(q, k_cmp, (((1,), (1,)), ((), ())), preferred_element_type=F32)
    s = jnp.where(ok[None], s.reshape(C_HPG, tq, nch), NEG)
    m = jnp.max(s, axis=-1, keepdims=True)
    e = jnp.where(ok[None], jnp.exp(s - m), 0.0)
    den = jnp.sum(e, axis=-1, keepdims=True)
    p = e / jnp.where(den > 0.0, den, 1.0)
    psum = jnp.sum(p, axis=0)
    oc = jnp.dot(p.reshape(C_HPG * tq, nch).astype(BF16), v_cmp, preferred_element_type=F32)
    for h in range(C_HPG):
        oc_ref[:, h * HEAD_DIM:(h + 1) * HEAD_DIM] = oc[h * tq:(h + 1) * tq]
    imp = jnp.dot(psum, ov_ref[...], preferred_element_type=F32, precision=lax.Precision.HIGHEST)
    blk = lax.broadcasted_iota(jnp.int32, (tq, ns_pad), 1)
    qblk = (qpos0 + qi * tq + lax.broadcasted_iota(jnp.int32, (tq, ns_pad), 0)) // SEL_BLK
    forced = (blk == 0) | (blk == qblk) | (blk == qblk - 1)
    allowed = (blk <= qblk) & (blk < n_sel)
    score = jnp.where(allowed, jnp.where(forced, -NEG, imp), -1.0)
    rank = jnp.zeros((tq, ns_pad), jnp.int32)
    for j in range(n_sel):
        cj = score[:, j:j + 1]
        before = (cj > score) | ((cj == score) & (j < blk))
        rank = rank + before.astype(jnp.int32)
    sel_ref[...] = (allowed & (rank < SEL_N)).astype(F32)


def _cmp_select(z, a, bm, layer, cmp_b1, cmp_w2, *, tq, qpos0, n_sel):
    b, t, _ = z.shape
    nch = a.shape[2]
    ns_pad = -(-n_sel // LANES) * LANES
    ov = jnp.asarray(_overlap_np(nch, n_sel, ns_pad))
    kern = functools.partial(_cmp_select_kernel, qpos0=qpos0, n_sel=n_sel)
    part = lambda kv: pl.BlockSpec((None, None, nch, CMP_HID), lambda bi, g, qi: (bi, kv * C_KV_HEADS + g, 0, 0))
    return pl.pallas_call(
        kern,
        grid=(b, C_KV_HEADS, t // tq),
        in_specs=[pl.BlockSpec((None, tq, C_HPG * HEAD_DIM), lambda bi, g, qi: (bi, qi, g)),
                  part(0), part(0), part(1), part(1),
                  pl.BlockSpec((None, 2, CMP_HID), lambda bi, g, qi: (layer, 0, 0)),
                  pl.BlockSpec((None, 2, CMP_HID, HEAD_DIM), lambda bi, g, qi: (layer, 0, 0, 0)),
                  pl.BlockSpec((nch, ns_pad), lambda bi, g, qi: (0, 0))],
        out_specs=[pl.BlockSpec((None, tq, C_HPG * HEAD_DIM), lambda bi, g, qi: (bi, qi, g)),
                   pl.BlockSpec((None, None, tq, ns_pad), lambda bi, g, qi: (bi, g, qi, 0))],
        out_shape=[jax.ShapeDtypeStruct((b, t, C_WIDTH), F32),
                   jax.ShapeDtypeStruct((b, C_KV_HEADS, t, ns_pad), F32)],
        compiler_params=_cparams("parallel", "parallel", "arbitrary"),
        name="cmp_select",
    )(z, a, bm, a, bm, cmp_b1, cmp_w2, ov)


def _sel_prompt_kernel(q_ref, k_ref, v_ref, rev_ref, sel_ref, o_ref,
                       bias_ref, msk_ref, qb_ref, m_ref, l_ref, acc_ref):
    qi = pl.program_id(2)
    tq = q_ref.shape[0]
    nt = bias_ref.shape[0] - 1
    tk = msk_ref.shape[1]
    ratio = tk // tq
    ns_pad = sel_ref.shape[1]

    @pl.when(qi == 0)
    def _():
        bias_ref[0] = jnp.full(bias_ref.shape[1:], NEG, F32)
        for h in range(C_HPG):
            for delta in range(nt):
                off = (nt - 1 - delta) * tq
                bias_ref[delta + 1, :, h * tq:(h + 1) * tq] = (
                    _toeplitz(rev_ref[h, :, off:off + 2 * tq], tq).T * LOG2E)

    sel_t = sel_ref[...].T.astype(BF16)
    erow = lax.broadcasted_iota(jnp.int32, (tk, ns_pad), 0) // SEL_BLK
    ecol = lax.broadcasted_iota(jnp.int32, (tk, ns_pad), 1)
    for t in range(msk_ref.shape[0]):
        expand = (ecol == erow + t * (tk // SEL_BLK)).astype(BF16)
        hit = jnp.dot(expand, sel_t, preferred_element_type=F32)
        msk_ref[t] = jnp.where(hit > 0.5, 0.0, NEG)
    for h in range(C_HPG):
        qb_ref[h * tq:(h + 1) * tq, :] = (q_ref[:, h * HEAD_DIM:(h + 1) * HEAD_DIM] * (SCALE * LOG2E)).astype(BF16)
    m_ref[...] = jnp.full(m_ref.shape, NEG, F32)
    l_ref[...] = jnp.zeros(l_ref.shape, F32)
    acc_ref[...] = jnp.zeros(acc_ref.shape, F32)

    def body(kj, carry):
        keys = pl.ds(pl.multiple_of(kj * tk, tk), tk)
        k = k_ref[keys, :].astype(BF16)
        v_t = v_ref[keys, :].T.astype(BF16)
        d0 = qi - ratio * kj
        parts = [bias_ref[d0 + 1 - i] for i in range(ratio)]
        bias = parts[0] if ratio == 1 else jnp.concatenate(parts, axis=0)
        s = lax.dot_general(k, qb_ref[...], (((1,), (1,)), ((), ())), preferred_element_type=F32)
        s = s + bias + jnp.concatenate([msk_ref[kj]] * C_HPG, axis=1)
        m_old = m_ref[...]
        m_new = jnp.maximum(m_old, jnp.max(s, axis=0, keepdims=True))
        alpha = jnp.exp2(m_old - m_new)
        e = jnp.exp2(s - m_new)
        m_ref[...] = m_new
        l_ref[...] = alpha * l_ref[...] + jnp.sum(e, axis=0, keepdims=True)
        acc_ref[...] = alpha * acc_ref[...] + jnp.dot(v_t, e.astype(BF16), preferred_element_type=F32)
        return carry

    lax.fori_loop(0, (qi + ratio) // ratio, body, 0)
    o = acc_ref[...] / l_ref[...]
    for h in range(C_HPG):
        o_ref[:, h * HEAD_DIM:(h + 1) * HEAD_DIM] = o[:, h * tq:(h + 1) * tq].T


SEL_TQ = 128
SEL_KEY_RATIO = 2


def _sel_attn_prompt(z, sel, rel_bias):
    b, t, _ = z.shape
    tq = SEL_TQ
    nt = t // tq
    ratio = SEL_KEY_RATIO
    assert nt % ratio == 0
    ns_pad = sel.shape[3]
    rev = _c_rev_table(rel_bias, t, tq, t)
    kcol = (ODD_KV0 + 2 * C_KV) // HEAD_DIM
    vcol = (ODD_KV0 + 3 * C_KV) // HEAD_DIM
    return pl.pallas_call(
        _sel_prompt_kernel,
        grid=(b, C_KV_HEADS, t // tq),
        in_specs=[pl.BlockSpec((None, tq, C_HPG * HEAD_DIM), lambda bi, g, qi: (bi, qi, g)),
                  pl.BlockSpec((None, t, HEAD_DIM), lambda bi, g, qi: (bi, 0, kcol + g)),
                  pl.BlockSpec((None, t, HEAD_DIM), lambda bi, g, qi: (bi, 0, vcol + g)),
                  pl.BlockSpec((C_HPG, 1, t + tq), lambda bi, g, qi: (g, 0, 0)),
                  pl.BlockSpec((None, None, tq, ns_pad), lambda bi, g, qi: (bi, g, qi, 0))],
        out_specs=pl.BlockSpec((None, tq, C_HPG * HEAD_DIM), lambda bi, g, qi: (bi, qi, g)),
        out_shape=jax.ShapeDtypeStruct((b, t, C_WIDTH), F32),
        scratch_shapes=[pltpu.VMEM((nt + 1, tq, C_HPG * tq), F32),
                        pltpu.VMEM((nt // ratio, ratio * tq, tq), F32),
                        pltpu.VMEM((C_HPG * tq, HEAD_DIM), BF16),
                        pltpu.VMEM((1, C_HPG * tq), F32),
                        pltpu.VMEM((1, C_HPG * tq), F32),
                        pltpu.VMEM((HEAD_DIM, C_HPG * tq), F32)],
        compiler_params=_cparams("parallel", "parallel", "arbitrary"),
        name="sel_attn_prompt",
    )(z, z, z, rev, sel)


def _sel_paged_kernel(pt_ref, *refs, n_src, past):
    del pt_ref
    pages = refs[:n_src]
    q_ref, new_ref, sel_ref, bias_ref, bnew_ref, o_ref, m_ref, l_ref, acc_ref = refs[n_src:]
    pg = pl.program_id(1)
    tq = q_ref.shape[0]
    rows = C_HPG * tq
    page = pages[0].shape[0] // CACHE_ROW_KINDS
    keys = n_src * page
    ns_pad = sel_ref.shape[2]

    def page_rows(r, kind):
        return r[pl.ds(CACHE_KINDS + kind, page, stride=CACHE_ROW_KINDS), :]

    def q_rows(g):
        return jnp.concatenate(
            [(q_ref[:, (g * C_HPG + h) * HEAD_DIM:(g * C_HPG + h + 1) * HEAD_DIM] * SCALE).astype(BF16)
             for h in range(C_HPG)], axis=0)

    @pl.when(pg == 0)
    def _():
        for g in range(C_KV_HEADS):
            k = new_ref[:, g * HEAD_DIM:(g + 1) * HEAD_DIM].astype(BF16)
            v = new_ref[:, C_KV + g * HEAD_DIM:C_KV + (g + 1) * HEAD_DIM].astype(BF16)
            s = lax.dot_general(q_rows(g), k, (((1,), (1,)), ((), ())), preferred_element_type=F32)
            s = s + bnew_ref[g * C_HPG:(g + 1) * C_HPG].reshape(rows, tq)
            m = jnp.max(s, axis=-1, keepdims=True)
            e = jnp.exp(s - m)
            m_ref[g] = m
            l_ref[g] = jnp.sum(e, axis=-1, keepdims=True)
            acc_ref[g] = jnp.dot(e.astype(BF16), v, preferred_element_type=F32)

    srow = lax.broadcasted_iota(jnp.int32, (ns_pad, keys), 0)
    scol = lax.broadcasted_iota(jnp.int32, (ns_pad, keys), 1) // SEL_BLK + pg * (keys // SEL_BLK)
    expand = (srow == scol).astype(BF16)
    for g in range(C_KV_HEADS):
        k = jnp.concatenate([page_rows(r, g) for r in pages], axis=0).astype(BF16)
        v = jnp.concatenate([page_rows(r, C_KV_HEADS + g) for r in pages], axis=0).astype(BF16)
        hit = jnp.dot(sel_ref[g].astype(BF16), expand, preferred_element_type=F32)
        mask = jnp.where(hit > 0.5, 0.0, NEG)
        s = lax.dot_general(q_rows(g), k, (((1,), (1,)), ((), ())), preferred_element_type=F32)
        s = (s.reshape(C_HPG, tq, keys) + bias_ref[g * C_HPG:(g + 1) * C_HPG] + mask[None]).reshape(rows, keys)
        m_old = m_ref[g]
        m_new = jnp.maximum(m_old, jnp.max(s, axis=-1, keepdims=True))
        alpha = jnp.exp(m_old - m_new)
        e = jnp.exp(s - m_new)
        m_ref[g] = m_new
        l_ref[g] = alpha * l_ref[g] + jnp.sum(e, axis=-1, keepdims=True)
        acc_ref[g] = alpha * acc_ref[g] + jnp.dot(e.astype(BF16), v, preferred_element_type=F32)

    @pl.when(pg == pl.num_programs(1) - 1)
    def _():
        for g in range(C_KV_HEADS):
            o = acc_ref[g] / l_ref[g]
            for h in range(C_HPG):
                o_ref[:, (g * C_HPG + h) * HEAD_DIM:(g * C_HPG + h + 1) * HEAD_DIM] = o[h * tq:(h + 1) * tq]


def _sel_attn_paged(z, sel, cache, layer, page_table, rel_bias):
    b, tq, _ = z.shape
    n_pages = page_table.shape[1]
    page = cache.shape[2] // CACHE_ROW_KINDS
    past = n_pages * page
    pps = PAGES_PER_STEP
    keys = pps * page
    ns_pad = sel.shape[3]
    tab = _c_tab(rel_bias)
    rev = tab[_t5_bucket_np(past + tq - 1 - np.arange(past + tq - 1))].T.astype(F32)
    bias = jnp.stack([rev[:, tq - 1 - t:tq - 1 - t + past] for t in range(tq)], axis=1)
    dnew = np.arange(tq)[:, None] - np.arange(tq)[None, :]
    bnew = jnp.where(jnp.asarray(dnew >= 0)[None],
                     jnp.transpose(tab[_t5_bucket_np(np.clip(dnew, 0, None))], (2, 0, 1)), NEG).astype(F32)
    kern = functools.partial(_sel_paged_kernel, n_src=pps, past=past)

    def page_spec(j):
        return pl.BlockSpec((None, None, page * CACHE_ROW_KINDS, HEAD_DIM),
                            lambda bi, pg, pt: (layer, pt[bi, pg * pps + j], 0, 0))

    rows = C_HPG * tq
    grid_spec = pltpu.PrefetchScalarGridSpec(
        num_scalar_prefetch=1,
        grid=(b, n_pages // pps),
        in_specs=[page_spec(j) for j in range(pps)] + [
            pl.BlockSpec((None, tq, C_WIDTH), lambda bi, pg, pt: (bi, 0, 0)),
            pl.BlockSpec((None, tq, 2 * C_KV), lambda bi, pg, pt: (bi, 0, (ODD_KV0 + 2 * C_KV) // (2 * C_KV))),
            pl.BlockSpec((None, C_KV_HEADS, tq, ns_pad), lambda bi, pg, pt: (bi, 0, 0, 0)),
            pl.BlockSpec((C_HEADS, tq, keys), lambda bi, pg, pt: (0, 0, pg)),
            pl.BlockSpec((C_HEADS, tq, tq), lambda bi, pg, pt: (0, 0, 0))],
        out_specs=pl.BlockSpec((None, tq, C_WIDTH), lambda bi, pg, pt: (bi, 0, 0)),
        scratch_shapes=[pltpu.VMEM((C_KV_HEADS, rows, 1), F32), pltpu.VMEM((C_KV_HEADS, rows, 1), F32),
                        pltpu.VMEM((C_KV_HEADS, rows, HEAD_DIM), F32)])
    return pl.pallas_call(
        kern, grid_spec=grid_spec, out_shape=jax.ShapeDtypeStruct((b, tq, C_WIDTH), F32),
        compiler_params=_cparams("parallel", "arbitrary"),
        name="sel_attn_paged",
    )(page_table, *([cache] * pps), z, z, sel, bias, bnew)


def _win_attn_prompt(z, rel_bias, *, tq):
    b, t, _ = z.shape
    n_tiles = WIN // tq + 1
    rev = _c_rev_table(rel_bias, n_tiles * tq, tq, WIN)
    kv_blk = (ODD_KV0 + 4 * C_KV) // (2 * C_KV)
    kern = functools.partial(_tile_attn_kernel, nh=C_HEADS, rep=C_HPG, n_tiles=n_tiles, k_col=0, v_col=C_KV,
                             shared_kv=True, with_lse=False, lead_axis=1, toeplitz=True)

    def kv_spec(p):
        return pl.BlockSpec((None, tq, 2 * C_KV),
                            lambda bi, qi: (bi, jnp.maximum(qi - (n_tiles - 1 - p), 0), kv_blk))

    return pl.pallas_call(
        kern,
        grid=(b, t // tq),
        in_specs=[pl.BlockSpec((None, tq, C_WIDTH), lambda bi, qi: (bi, qi, 0))]
                 + [kv_spec(p) for p in range(n_tiles)]
                 + [pl.BlockSpec((C_HEADS, 1, (n_tiles + 1) * tq), lambda bi, qi: (0, 0, 0))],
        out_specs=pl.BlockSpec((None, tq, C_WIDTH), lambda bi, qi: (bi, qi, 0)),
        out_shape=jax.ShapeDtypeStruct((b, t, C_WIDTH), F32),
        scratch_shapes=[pltpu.VMEM((C_HEADS, tq, n_tiles * tq), F32)],
        compiler_params=_cparams("parallel", "arbitrary"),
        name="win_attn_prompt",
    )(z, *([z] * n_tiles), rev)


def _win_attn_sample(z, win_full, rel_bias):
    b, tq, _ = z.shape
    lk = win_full.shape[1]
    dist = (lk - tq) + np.arange(tq)[:, None] - np.arange(lk)[None, :]
    valid = (dist >= 0) & (dist <= WIN)
    bias = jnp.transpose(_c_tab(rel_bias)[_t5_bucket_np(np.clip(dist, 0, None))], (2, 0, 1))
    bias = jnp.where(jnp.asarray(valid)[None], bias, NEG).astype(F32)
    kern = functools.partial(_tile_attn_kernel, nh=C_HEADS, rep=C_HPG, n_tiles=1, k_col=0, v_col=C_KV,
                             shared_kv=True, with_lse=False, lead_axis=0)
    return pl.pallas_call(
        kern,
        grid=(b,),
        in_specs=[pl.BlockSpec((None, tq, C_WIDTH), lambda bi: (bi, 0, 0)),
                  pl.BlockSpec((None, lk, 2 * C_KV), lambda bi: (bi, 0, 0)),
                  pl.BlockSpec((C_HEADS, tq, lk), lambda bi: (0, 0, 0))],
        out_specs=pl.BlockSpec((None, tq, C_WIDTH), lambda bi: (bi, 0, 0)),
        out_shape=jax.ShapeDtypeStruct((b, tq, C_WIDTH), F32),
        compiler_params=_cparams("parallel"),
        name="win_attn_sample",
    )(z, win_full, bias)


def _odd_mix_kernel(oc_ref, os_ref, ow_ref, gl_ref, gp_ref, mix_ref):
    gates = jax.nn.sigmoid(gl_ref[...])
    branches = (oc_ref, os_ref, ow_ref)
    for h in range(C_HEADS):
        cs = slice(h * HEAD_DIM, (h + 1) * HEAD_DIM)
        acc = gates[:, h:h + 1] * branches[0][:, cs]
        for br in range(1, 3):
            lane = br * C_HEADS + h
            acc = acc + gates[:, lane:lane + 1] * branches[br][:, cs]
        mix_ref[:, cs] = (acc * _silu(gp_ref[:, cs])).astype(BF16)


def _odd_mix(oc, osel, ow, z2, zgp, *, tm):
    m = z2.shape[0]
    wide = pl.BlockSpec((tm, C_WIDTH), lambda i: (i, 0))
    return pl.pallas_call(
        _odd_mix_kernel,
        grid=(m // tm,),
        in_specs=[wide, wide, wide, pl.BlockSpec((tm, LANES), lambda i: (i, ODD_GL0 // LANES)), wide],
        out_specs=wide,
        out_shape=jax.ShapeDtypeStruct((m, C_WIDTH), BF16),
        compiler_params=_cparams("parallel"),
        name="odd_mix",
    )(oc, osel, ow, z2, zgp)


def _odd_layer(x, past, win_hist, rel_bias, layer, norm_g, w_in, w_in_gp, cmp_pos, cmp_w1, cmp_b1, cmp_w2, w_out,
               *, tm, tn_in, tn_out, tq):
    b, t, d = x.shape
    m = b * t
    x2 = x.reshape(m, d)
    z2 = _norm_matmul(x2, norm_g[layer], w_in, layer, tm=tm, tn=tn_in, n_out=ODD_Z)
    zgp = _norm_matmul(x2, norm_g[layer], w_in_gp, layer, tm=tm, tn=tn_in)
    z = z2.reshape(b, t, ODD_Z)
    rows = z[:, :, ODD_KV0:ODD_KV0 + 4 * C_KV].reshape(b, t, 4, C_KV_HEADS, HEAD_DIM)
    win_new = z[:, :, ODD_KV0 + 4 * C_KV:ODD_KV0 + 6 * C_KV]
    if past is None:
        length = t
        a, bm = _cmp_partials_prompt(z, layer, cmp_pos, cmp_w1)
        qpos0 = 0
    else:
        cache, page_table = past
        p_len = page_table.shape[1] * (cache.shape[2] // CACHE_ROW_KINDS)
        length = p_len + t
        assert length // CMP_STRIDE == p_len // CMP_STRIDE and p_len % SEL_BLK == 0
        a, bm = _cmp_partials_paged(cache, layer, page_table, cmp_pos, cmp_w1)
        qpos0 = p_len
    n_sel = -(-length // SEL_BLK)
    oc, sel = _cmp_select(z, a, bm, layer, cmp_b1, cmp_w2, tq=tq, qpos0=qpos0, n_sel=n_sel)
    if past is None:
        osel = _sel_attn_prompt(z, sel, rel_bias)
        ow = _win_attn_prompt(z, rel_bias, tq=tq)
        new_win = win_new[:, t - min(WIN, t):].reshape(b, min(WIN, t), 2, C_KV_HEADS, HEAD_DIM)
    else:
        osel = _sel_attn_paged(z, sel, cache, layer, page_table, rel_bias)
        win_full = jnp.concatenate([win_hist.reshape(b, -1, 2 * C_KV), win_new], axis=1)
        ow = _win_attn_sample(z, win_full, rel_bias)
        lk = win_full.shape[1]
        keep = min(WIN, length)
        new_win = win_full[:, lk - keep:].reshape(b, keep, 2, C_KV_HEADS, HEAD_DIM)
    mix = _odd_mix(oc.reshape(m, C_WIDTH), osel.reshape(m, C_WIDTH), ow.reshape(m, C_WIDTH), z2, zgp,
                   tm=min(tm, 256))
    y = _matmul_res([mix], w_out, layer, x2, tm=tm, tn=tn_out)
    return y.reshape(b, t, d), rows, new_win


def _run_trunk(x, a_caches, conv_state, c_cache, c_win, page_table, rel_bias, norm_even, w_in_even, conv_w,
               conv_b, conv_ln_g, conv_ln_b, conv_pw_w, conv_pw_b, w_out_even, norm_odd, w_in_odd, w_in_odd_gp,
               cmp_pos, cmp_w1, cmp_b1, cmp_w2, w_out_odd, final_norm, *, tm, tt, tq):
    b, t, d = x.shape
    prompt = a_caches is None
    depth = norm_even.shape[0] + norm_odd.shape[0]
    new_a = [[] for _ in A_GROUPS]
    a_new = None
    new_conv, new_rows, new_win = [], [], []
    for depth_i in range(depth):
        i = depth_i // 2
        if depth_i % 2 == 0:
            conv_hist = jnp.zeros((b, CONV_W - 1, B_WIDTH), F32) if prompt else conv_state[i]
            x, hists, conv = _even_layer(x, a_caches, a_new, conv_hist, rel_bias, i, norm_even, w_in_even, conv_w,
                                         conv_b, conv_ln_g, conv_ln_b, conv_pw_w, conv_pw_b,
                                         w_out_even, tm=tm, tn_in=512, tn_out=512, tt=tt)
            if prompt:
                for g in range(A_NG):
                    new_a[g].append(hists[g])
            else:
                a_new = hists
            new_conv.append(conv)
        else:
            past = None if prompt else (c_cache, page_table)
            x, rows, win = _odd_layer(x, past, None if prompt else c_win[i], rel_bias, i, norm_odd,
                                      w_in_odd, w_in_odd_gp, cmp_pos, cmp_w1, cmp_b1, cmp_w2,
                                      w_out_odd, tm=tm, tn_in=512, tn_out=512, tq=tq)
            new_rows.append(rows)
            new_win.append(win)
    y = _rms_norm(x.reshape(b * t, d), final_norm, tm=min(tm, 256)).reshape(b, t, d)
    if prompt:
        a_out = [jnp.stack(a) for a in new_a]
    else:
        a_out = [n.reshape(c.shape) for n, c in zip(a_new, a_caches)]
    return (y, a_out, jnp.stack(new_conv), jnp.stack(new_rows), jnp.stack(new_win))


def kernel(x_prompt, x_sample, cache_a_kv0, cache_a_kv1, cache_a_kv2, state_b_conv, cache_c_kv, cache_c_win, page_table, rel_bias, norm_even, w_in_even, conv_w, conv_b, conv_ln_g, conv_ln_b, conv_pw_w, conv_pw_b, w_out_even, norm_odd, w_in_odd, cmp_pos, cmp_w1, cmp_b1, cmp_w2, w_out_odd, final_norm):
    w_in_odd_gp = w_in_odd[:, :, ODD_GP0:]
    weights = (rel_bias, norm_even, w_in_even, conv_w, conv_b, conv_ln_g, conv_ln_b, conv_pw_w, conv_pw_b,
               w_out_even, norm_odd, w_in_odd, w_in_odd_gp, cmp_pos, cmp_w1, cmp_b1, cmp_w2, w_out_odd, final_norm)
    y_p, a_p, conv_p, rows_p, win_p = _run_trunk(x_prompt, None, None, None, None, None, *weights,
                                                 tm=2048, tt=256, tq=128)
    n_odd, n_pool, page = cache_c_kv.shape[:3]
    c_cache = cache_c_kv.reshape(n_odd, n_pool, page * CACHE_ROW_KINDS, HEAD_DIM)
    db, dt = x_sample.shape[:2]
    y_s, a_s, conv_s, rows_s, win_s = _run_trunk(x_sample, (cache_a_kv0, cache_a_kv1, cache_a_kv2), state_b_conv,
                                                 c_cache, cache_c_win, page_table, *weights,
                                                 tm=db * dt, tt=dt, tq=dt)
    return (y_p, y_s, a_p[0], a_p[1], a_p[2], conv_p, rows_p, win_p, a_s[0], a_s[1], a_s[2], conv_s, rows_s, win_s)
```

```python
import functools
import math

import numpy as np
import jax
import jax.numpy as jnp
from jax import lax
from jax.experimental import pallas as pl
from jax.experimental.pallas import tpu as pltpu

F32 = jnp.float32
BF16 = jnp.bfloat16

D_MODEL = 2048
HEAD_DIM = 128
LANES = 128
SUBLANES = 8
A_GROUPS = ((128, 1), (512, 4), (2048, 16))
A_NG = 3
A_HPG = 8
A_QKV = A_NG * A_HPG * HEAD_DIM
A_WIDTH = A_HPG * HEAD_DIM
B_WIDTH = 1024
CONV_W = 31
CONV_HALO = 32
EVEN_IN = 3 * A_QKV + A_WIDTH + 3 * B_WIDTH
C_HEADS = 16
C_KV_HEADS = 2
C_HPG = C_HEADS // C_KV_HEADS
C_WIDTH = C_HEADS * HEAD_DIM
C_KV = C_KV_HEADS * HEAD_DIM
CMP_BLK = 32
CMP_STRIDE = 16
CMP_HID = 128
SEL_BLK = 64
SEL_N = 16
WIN = 512
NUM_BUCKETS = 32
MAX_DIST = 2048
EPS = 1e-6
NEG = -1e30
SCALE = HEAD_DIM ** -0.5
LOG2E = math.log2(math.e)
VMEM_LIMIT = 56 * 1024 * 1024


def _cparams(*sem):
    return pltpu.CompilerParams(dimension_semantics=sem, vmem_limit_bytes=VMEM_LIMIT)


def _t5_bucket_np(dist):
    max_exact = NUM_BUCKETS // 2
    d = np.maximum(np.asarray(dist, np.int64), 0)
    ratio = np.log(np.maximum(d, 1).astype(np.float64) / max_exact) / math.log(MAX_DIST / max_exact)
    large = np.minimum(max_exact + (ratio * (NUM_BUCKETS - max_exact)).astype(np.int64), NUM_BUCKETS - 1)
    return np.where(d < max_exact, d, large).astype(np.int32)


def _silu(x):
    return x * jax.nn.sigmoid(x)


NORM_ROWS = 256


def _norm_matmul_kernel(x_ref, g_ref, w_ref, o_ref, xn_ref):
    @pl.when(pl.program_id(1) == 0)
    def _():
        rows = min(NORM_ROWS, x_ref.shape[0])

        def norm(c, carry):
            cur = pl.ds(pl.multiple_of(c * rows, rows), rows)
            x = x_ref[cur, :]
            ms = jnp.mean(x * x, axis=-1, keepdims=True)
            xn_ref[cur, :] = (x * lax.rsqrt(ms + EPS) * g_ref[...]).astype(BF16)
            return carry

        lax.fori_loop(0, x_ref.shape[0] // rows, norm, 0)

    o_ref[...] = jnp.dot(xn_ref[...], w_ref[...].astype(BF16), preferred_element_type=F32)


def _norm_matmul(x, g, w, layer, *, tm, tn, n_out=None):
    m, k = x.shape
    n = w.shape[2] if n_out is None else n_out
    assert n % tn == 0 and n <= w.shape[2]
    return pl.pallas_call(
        _norm_matmul_kernel,
        grid=(m // tm, n // tn),
        in_specs=[pl.BlockSpec((tm, k), lambda i, j: (i, 0), pipeline_mode=pl.Buffered(1)),
                  pl.BlockSpec((1, k), lambda i, j: (0, 0)),
                  pl.BlockSpec((None, k, tn), lambda i, j: (layer, 0, j))],
        out_specs=pl.BlockSpec((tm, tn), lambda i, j: (i, j)),
        out_shape=jax.ShapeDtypeStruct((m, n), F32),
        scratch_shapes=[pltpu.VMEM((tm, k), BF16)],
        compiler_params=_cparams("parallel", "arbitrary"),
        name="norm_matmul",
    )(x, g.reshape(1, k), w)


def _matmul_res_kernel(*refs, n_parts):
    a_refs = refs[:n_parts]
    w_ref, r_ref, o_ref = refs[n_parts:]
    acc = r_ref[...]
    k0 = 0
    for a_ref in a_refs:
        kp = a_ref.shape[1]
        acc = acc + jnp.dot(a_ref[...], w_ref[k0:k0 + kp, :].astype(BF16), preferred_element_type=F32)
        k0 += kp
    o_ref[...] = acc


def _matmul_res(parts, w, layer, res, *, tm, tn):
    m = res.shape[0]
    k, n = w.shape[1:]
    assert sum(p.shape[1] for p in parts) == k
    return pl.pallas_call(
        functools.partial(_matmul_res_kernel, n_parts=len(parts)),
        grid=(m // tm, n // tn),
        in_specs=[pl.BlockSpec((tm, p.shape[1]), lambda i, j: (i, 0), pipeline_mode=pl.Buffered(1))
                  for p in parts]
                 + [pl.BlockSpec((None, k, tn), lambda i, j: (layer, 0, j)),
                    pl.BlockSpec((tm, tn), lambda i, j: (i, j))],
        out_specs=pl.BlockSpec((tm, tn), lambda i, j: (i, j)),
        out_shape=jax.ShapeDtypeStruct((m, n), F32),
        compiler_params=_cparams("parallel", "arbitrary"),
        name="matmul_res",
    )(*parts, w, res)


def _rms_kernel(x_ref, g_ref, o_ref):
    x = x_ref[...]
    ms = jnp.mean(x * x, axis=-1, keepdims=True)
    o_ref[...] = x * lax.rsqrt(ms + EPS) * g_ref[...]


def _rms_norm(x, g, *, tm):
    m, k = x.shape
    return pl.pallas_call(
        _rms_kernel,
        grid=(m // tm,),
        in_specs=[pl.BlockSpec((tm, k), lambda i: (i, 0)), pl.BlockSpec((1, k), lambda i: (0, 0))],
        out_specs=pl.BlockSpec((tm, k), lambda i: (i, 0)),
        out_shape=jax.ShapeDtypeStruct((m, k), F32),
        compiler_params=_cparams("parallel"),
        name="rms_norm",
    )(x, g.reshape(1, k))


def _c_tab(rel_bias):
    return rel_bias[:, :C_HEADS]


def _c_rev_table(rel_bias, length, tq, max_dist):
    dist = length - np.arange(length + tq)
    valid = (dist >= 0) & (dist <= max_dist)
    tab = _c_tab(rel_bias)[_t5_bucket_np(np.clip(dist, 0, None))]
    tab = jnp.where(jnp.asarray(valid)[:, None], tab, NEG).astype(F32)
    return tab.T.reshape(C_HEADS, 1, length + tq)


def _toeplitz(vec, tq):
    w = vec.shape[1]
    return pltpu.roll(jnp.broadcast_to(vec, (tq, w)), 0, 1, stride=1, stride_axis=0)[:, tq:]


def _tile_attn_kernel(*refs, nh, rep, n_tiles, k_col, v_col, shared_kv, with_lse, lead_axis, toeplitz=False):
    if toeplitz:
        rev_ref, bias_scr = refs[-1 - int(with_lse) - 2], refs[-1]
        tq_ = refs[0].shape[0]

        @pl.when(pl.program_id(lead_axis) == 0)
        def _():
            for h in range(nh):
                bias_scr[h] = _toeplitz(rev_ref[h], tq_)

        refs = list(refs[:-1])
        refs[-1 - int(with_lse) - 1] = bias_scr
    q_ref = refs[0]
    if shared_kv:
        k_refs = v_refs = refs[1:1 + n_tiles]
        nxt = 1 + n_tiles
    else:
        k_refs = refs[1:1 + n_tiles]
        v_refs = refs[1 + n_tiles:1 + 2 * n_tiles]
        nxt = 1 + 2 * n_tiles
    b_ref = refs[nxt]
    o_ref = refs[nxt + 1]
    lse_ref = refs[nxt + 2] if with_lse else None
    tq = q_ref.shape[0]
    tk = k_refs[0].shape[0]
    lk = n_tiles * tk
    if n_tiles > 1:
        qi = pl.program_id(lead_axis)
        col = lax.broadcasted_iota(jnp.int32, (tq, lk), 1)
        pad_mask = jnp.where(col < (n_tiles - 1 - qi) * tk, NEG, 0.0).astype(F32)
    else:
        pad_mask = None
    if with_lse:
        lane = lax.broadcasted_iota(jnp.int32, (tq, LANES), 1)
        lse_t = jnp.zeros((tq, LANES), F32)
    for j in range(nh // rep):
        kc = k_col + j * HEAD_DIM
        vc = v_col + j * HEAD_DIM
        heads = range(j * rep, (j + 1) * rep)
        q = jnp.concatenate([(q_ref[:, h * HEAD_DIM:(h + 1) * HEAD_DIM] * SCALE).astype(BF16) for h in heads], axis=0)
        if n_tiles > 1:
            k = jnp.concatenate([r[:, kc:kc + HEAD_DIM] for r in k_refs], axis=0).astype(BF16)
            v = jnp.concatenate([r[:, vc:vc + HEAD_DIM] for r in v_refs], axis=0).astype(BF16)
        else:
            k = k_refs[0][:, kc:kc + HEAD_DIM].astype(BF16)
            v = v_refs[0][:, vc:vc + HEAD_DIM].astype(BF16)
        s = lax.dot_general(q, k, (((1,), (1,)), ((), ())), preferred_element_type=F32)
        s = s.reshape(rep, tq, lk) + b_ref[j * rep:(j + 1) * rep]
        if pad_mask is not None:
            s = s + pad_mask[None]
        s = s.reshape(rep * tq, lk)
        m = jnp.max(s, axis=-1, keepdims=True)
        e = jnp.exp(s - m)
        den = jnp.sum(e, axis=-1, keepdims=True)
        o = jnp.dot(e.astype(BF16), v, preferred_element_type=F32) / den
        for r, h in enumerate(heads):
            o_ref[:, h * HEAD_DIM:(h + 1) * HEAD_DIM] = o[r * tq:(r + 1) * tq]
        if with_lse:
            lse = m + jnp.log(den)
            for r, h in enumerate(heads):
                lse_t = jnp.where(lane == h, lse[r * tq:(r + 1) * tq], lse_t)
    if with_lse:
        lse_ref[...] = lse_t


A_TU = 128


def _a_rev_table(rel_bias):
    j = 2 * A_TU - np.arange(3 * A_TU)
    rows = []
    for g, (win, dil) in enumerate(A_GROUPS):
        valid = (j >= 0) & (j <= win // dil)
        tab = rel_bias[:, g * A_HPG:(g + 1) * A_HPG][_t5_bucket_np(np.clip(j, 0, None) * dil)]
        rows.append(jnp.where(jnp.asarray(valid)[:, None], tab, NEG).T)
    return jnp.stack(rows, axis=1).astype(F32)


def _a_prompt_kernel(q0_ref, q1_ref, q2_ref, k0_ref, k1_ref, k2_ref, v0_ref, v1_ref, v2_ref, ga_ref, rev_ref,
                     mix_ref, o_scr, l_scr, bias_scr):
    t = q0_ref.shape[0]
    tu = A_TU
    q_refs, k_refs, v_refs = (q0_ref, q1_ref, q2_ref), (k0_ref, k1_ref, k2_ref), (v0_ref, v1_ref, v2_ref)
    for g in range(A_NG):
        bias_scr[g] = _toeplitz(rev_ref[g:g + 1, :], tu)
    col = lax.broadcasted_iota(jnp.int32, (tu, 2 * tu), 1)
    for g, (win, dil) in enumerate(A_GROUPS):
        n_u = t // (dil * tu)
        q_ref, k_ref, v_ref = q_refs[g], k_refs[g], v_refs[g]

        def rows(r, u, dil=dil):
            start = r + u * (tu * dil)
            if dil == 1:
                return pl.ds(pl.multiple_of(start, tu), tu)
            return pl.ds(start, tu, stride=dil)

        def body(it, carry, g=g, n_u=n_u, q_ref=q_ref, k_ref=k_ref, v_ref=v_ref, rows=rows):
            r = it // n_u
            u = it % n_u
            cur = rows(r, u)
            q = (q_ref[cur, :] * SCALE).astype(BF16)
            if n_u > 1:
                prev = rows(r, jnp.maximum(u - 1, 0))
                k = jnp.concatenate([k_ref[prev, :], k_ref[cur, :]], axis=0).astype(BF16)
                v = jnp.concatenate([v_ref[prev, :], v_ref[cur, :]], axis=0).astype(BF16)
                bias = bias_scr[g] + jnp.where((col < tu) & (u == 0), NEG, 0.0)
            else:
                k = k_ref[cur, :].astype(BF16)
                v = v_ref[cur, :].astype(BF16)
                bias = bias_scr[g][:, tu:]
            s = lax.dot_general(q, k, (((1,), (1,)), ((), ())), preferred_element_type=F32) + bias
            m = jnp.max(s, axis=-1, keepdims=True)
            e = jnp.exp(s - m)
            den = jnp.sum(e, axis=-1, keepdims=True)
            o_scr[g, cur, :] = jnp.dot(e.astype(BF16), v, preferred_element_type=F32) / den
            l_scr[g, cur, :] = jnp.broadcast_to(m + jnp.log(den), (tu, HEAD_DIM))
            return carry

        lax.fori_loop(0, t // tu, body, 0, unroll=8)

    def combine(c, carry):
        cur = pl.ds(pl.multiple_of(c * tu, tu), tu)
        ls = [l_scr[g, cur, :] for g in range(A_NG)]
        m = jnp.maximum(jnp.maximum(ls[0], ls[1]), ls[2])
        es = [jnp.exp(l - m) for l in ls]
        acc = es[0] * o_scr[0, cur, :] + es[1] * o_scr[1, cur, :] + es[2] * o_scr[2, cur, :]
        out = acc / (es[0] + es[1] + es[2])
        mix_ref[cur, :] = (out * _silu(ga_ref[cur, :])).astype(BF16)
        return carry

    lax.fori_loop(0, t // tu, combine, 0)


def _a_attn_prompt(z, rel_bias):
    b, t, n = z.shape
    for win, dil in A_GROUPS:
        assert win // dil == A_TU and t % (dil * A_TU) == 0
    rev = _a_rev_table(rel_bias)
    nq = A_QKV // HEAD_DIM
    col = lambda base: pl.BlockSpec((None, t, HEAD_DIM), lambda bi, h: (bi, 0, base + h))
    return pl.pallas_call(
        _a_prompt_kernel,
        grid=(b, A_HPG),
        in_specs=[col(g * A_HPG) for g in range(A_NG)]
                 + [col(nq + g * A_HPG) for g in range(A_NG)]
                 + [col(2 * nq + g * A_HPG) for g in range(A_NG)]
                 + [col(3 * nq), pl.BlockSpec((None, A_NG, 3 * A_TU), lambda bi, h: (h, 0, 0))],
        out_specs=pl.BlockSpec((None, t, HEAD_DIM), lambda bi, h: (bi, 0, h)),
        out_shape=jax.ShapeDtypeStruct((b, t, A_WIDTH), BF16),
        scratch_shapes=[pltpu.VMEM((A_NG, t, HEAD_DIM), F32), pltpu.VMEM((A_NG, t, HEAD_DIM), F32),
                        pltpu.VMEM((A_NG, A_TU, 2 * A_TU), F32)],
        compiler_params=_cparams("parallel", "parallel"),
        name="a_attn_prompt",
    )(*([z] * 10), rev)


def _a_bias_sample(rel_bias, g, hist_len, tq):
    win, dil = A_GROUPS[g]
    lk = hist_len + tq
    dist = hist_len + tq - 1 - np.arange(lk + tq - 1)
    valid = (dist >= 0) & (dist <= win) & (dist % dil == 0)
    tab = rel_bias[:, g * A_HPG:(g + 1) * A_HPG][_t5_bucket_np(np.clip(dist, 0, None))]
    rev = jnp.where(jnp.asarray(valid)[:, None], tab, NEG).T.astype(F32)
    return jnp.stack([rev[:, tq - 1 - t:tq - 1 - t + lk] for t in range(tq)], axis=1)


A_ROWS = 2 * A_HPG
A_CHUNK = 512


def _a_sample_kernel(*refs, aliased, layer):
    q_ref, kn_ref, vn_ref, old_ref, head_ref, bias_ref, bnew_ref = refs[:7]
    o_ref, lse_ref, new_ref, m_ref, l_ref, acc_ref = refs[7 + int(aliased):]
    if not aliased:
        for other in range(new_ref.shape[0]):
            if other != layer:
                new_ref[other] = jnp.zeros(new_ref.shape[1:], F32)
        new_ref = new_ref.at[layer]
    c = pl.program_id(1)
    last = pl.num_programs(1) - 1
    tq = q_ref.shape[0]
    n_tok = old_ref.shape[0] // A_ROWS
    shift = tq * A_ROWS

    @pl.when(c == 0)
    def _():
        m_ref[...] = jnp.full(m_ref.shape, NEG, F32)
        l_ref[...] = jnp.zeros(l_ref.shape, F32)
        acc_ref[...] = jnp.zeros(acc_ref.shape, F32)

    def update(h, q, k, v, bias):
        s = lax.dot_general(q, k, (((1,), (1,)), ((), ())), preferred_element_type=F32) + bias
        m_old = m_ref[h]
        m_new = jnp.maximum(m_old, jnp.max(s, axis=-1, keepdims=True))
        alpha = jnp.exp(m_old - m_new)
        e = jnp.exp(s - m_new)
        m_ref[h] = m_new
        l_ref[h] = alpha * l_ref[h] + jnp.sum(e, axis=-1, keepdims=True)
        acc_ref[h] = alpha * acc_ref[h] + jnp.dot(e.astype(BF16), v, preferred_element_type=F32)

    qs = [(q_ref[:, h * HEAD_DIM:(h + 1) * HEAD_DIM] * SCALE).astype(BF16) for h in range(A_HPG)]
    for h in range(A_HPG):
        k = old_ref[pl.ds(h, n_tok, stride=A_ROWS), :].astype(BF16)
        v = old_ref[pl.ds(A_HPG + h, n_tok, stride=A_ROWS), :].astype(BF16)
        update(h, qs[h], k, v, bias_ref[h])

    new_ref[0:n_tok * A_ROWS - shift, :] = old_ref[shift:, :]

    @pl.when(c < last)
    def _():
        new_ref[n_tok * A_ROWS - shift:, :] = head_ref[...]

    @pl.when(c == last)
    def _():
        lane = lax.broadcasted_iota(jnp.int32, (tq, LANES), 1)
        lse_t = jnp.zeros((tq, LANES), F32)
        for h in range(A_HPG):
            cs = slice(h * HEAD_DIM, (h + 1) * HEAD_DIM)
            update(h, qs[h], kn_ref[:, cs].astype(BF16), vn_ref[:, cs].astype(BF16), bnew_ref[h])
            o_ref[:, cs] = acc_ref[h] / l_ref[h]
            lse_t = jnp.where(lane == h, m_ref[h] + jnp.log(l_ref[h]), lse_t)
            base = n_tok * A_ROWS - shift
            new_ref[pl.ds(base + h, tq, stride=A_ROWS), :] = kn_ref[:, cs]
            new_ref[pl.ds(base + A_HPG + h, tq, stride=A_ROWS), :] = vn_ref[:, cs]
        lse_ref[...] = lse_t


def _a_attn_sample(z, cache, layer, new_cache, rel_bias, g):
    b, tq, n = z.shape
    n_layers, _, w = cache.shape[:3]
    assert cache.shape[3:] == (2, A_HPG, HEAD_DIM) and w % LANES == 0 and tq * A_ROWS == LANES
    ch = min(w, A_CHUNK)
    n_ch = w // ch
    old = cache.reshape(n_layers, b, w * A_ROWS, HEAD_DIM)
    bias_full = _a_bias_sample(rel_bias, g, w, tq)
    bias = jnp.stack([bias_full[:, :, i * ch:(i + 1) * ch] for i in range(n_ch)])
    bnew = bias_full[:, :, w:]
    nq = A_QKV // A_WIDTH
    zcol = lambda cb: pl.BlockSpec((None, tq, A_WIDTH), lambda bi, c: (bi, 0, cb))
    heads_per_chunk = ch * A_ROWS // LANES
    in_specs = [zcol(g), zcol(nq + g), zcol(2 * nq + g),
                pl.BlockSpec((None, None, ch * A_ROWS, HEAD_DIM), lambda bi, c: (layer, bi, c, 0)),
                pl.BlockSpec((None, None, LANES, HEAD_DIM),
                             lambda bi, c: (layer, bi, jnp.minimum(c + 1, n_ch - 1) * heads_per_chunk, 0)),
                pl.BlockSpec((None, A_HPG, tq, ch), lambda bi, c: (c, 0, 0, 0)),
                pl.BlockSpec((A_HPG, tq, tq), lambda bi, c: (0, 0, 0))]
    args = [z, z, z, old, old, bias, bnew]
    aliases = {}
    if new_cache is not None:
        in_specs.append(pl.BlockSpec(memory_space=pl.ANY))
        args.append(new_cache)
        aliases = {len(args) - 1: 2}
        new_spec = pl.BlockSpec((None, None, ch * A_ROWS, HEAD_DIM), lambda bi, c: (layer, bi, c, 0))
    else:
        new_spec = pl.BlockSpec((n_layers, None, ch * A_ROWS, HEAD_DIM), lambda bi, c: (0, bi, c, 0))
    o, lse, new = pl.pallas_call(
        functools.partial(_a_sample_kernel, aliased=new_cache is not None, layer=layer),
        grid=(b, n_ch),
        in_specs=in_specs,
        out_specs=[pl.BlockSpec((None, tq, A_WIDTH), lambda bi, c: (bi, 0, 0)),
                   pl.BlockSpec((None, tq, LANES), lambda bi, c: (bi, 0, 0)),
                   new_spec],
        out_shape=[jax.ShapeDtypeStruct((b, tq, A_WIDTH), F32),
                   jax.ShapeDtypeStruct((b, tq, LANES), F32),
                   jax.ShapeDtypeStruct(old.shape, F32)],
        scratch_shapes=[pltpu.VMEM((A_HPG, tq, 1), F32), pltpu.VMEM((A_HPG, tq, 1), F32),
                        pltpu.VMEM((A_HPG, tq, HEAD_DIM), F32)],
        input_output_aliases=aliases,
        compiler_params=_cparams("parallel", "arbitrary"),
        name=f"a_attn_sample_g{g}",
    )(*args)
    return o, lse, new


def _conv_kernel(a_ref, gt_ref, gb_ref, hist_ref, cw_ref, cb_ref, lg_ref, lb_ref, pw_ref, pb_ref,
                 o_ref, nc_ref, ubuf, ybuf, pwb):
    ti = pl.program_id(1)
    nt = pl.num_programs(1)
    tt = a_ref.shape[0]
    nhist = CONV_W - 1
    pad = CONV_HALO - nhist

    @pl.when(ti == 0)
    def _():
        pwb[...] = pw_ref[...].astype(BF16)
        ubuf[0:pad, :] = jnp.zeros((pad, B_WIDTH), F32)
        ubuf[pad:CONV_HALO, :] = hist_ref[...]

    ubuf[CONV_HALO:CONV_HALO + tt, :] = a_ref[...] * jax.nn.sigmoid(gt_ref[...])
    for c in range(B_WIDTH // LANES):
        cs = slice(c * LANES, (c + 1) * LANES)
        acc = jnp.zeros((tt, LANES), F32) + cb_ref[:, cs]
        for r in range(SUBLANES):
            taps = [a for a in range(CONV_HALO // SUBLANES + 1) if 0 <= SUBLANES * a + r - pad < CONV_W]
            span = tt if r == 0 else tt + SUBLANES
            z = None
            for a in taps:
                k = SUBLANES * a + r - pad
                term = ubuf[SUBLANES * a:SUBLANES * a + span, cs] * cw_ref[k:k + 1, cs]
                z = term if z is None else z + term
            acc = acc + (z if r == 0 else z[r:r + tt])
        ybuf[:, cs] = acc
    y = ybuf[...]
    mu = jnp.mean(y, axis=-1, keepdims=True)
    yc = y - mu
    var = jnp.mean(yc * yc, axis=-1, keepdims=True)
    yn = yc * lax.rsqrt(var + EPS) * lg_ref[...] + lb_ref[...]
    act = _silu(yn).astype(BF16)
    ob = jnp.dot(act, pwb[...], preferred_element_type=F32) + pb_ref[...]
    o_ref[...] = (ob * _silu(gb_ref[...])).astype(BF16)

    @pl.when(ti == nt - 1)
    def _():
        nc_ref[...] = ubuf[tt + pad:tt + CONV_HALO, :]

    @pl.when(ti < nt - 1)
    def _():
        ubuf[0:CONV_HALO, :] = ubuf[tt:tt + CONV_HALO, :]


def _conv_module(z, hist, layer, conv_w, conv_b, ln_g, ln_b, pw_w, pw_b, *, tt):
    b, t, n = z.shape
    assert t % tt == 0 and (tt >= CONV_HALO or t == tt)
    glu0 = (3 * A_QKV + A_WIDTH) // B_WIDTH
    row = lambda a: a.reshape(a.shape[0], 1, B_WIDTH)
    blk = (None, tt, B_WIDTH)
    lyr = lambda shape: pl.BlockSpec((None,) + shape, lambda bi, ti: (layer,) + (0,) * len(shape))
    return pl.pallas_call(
        _conv_kernel,
        grid=(b, t // tt),
        in_specs=[pl.BlockSpec(blk, lambda bi, ti: (bi, ti, glu0)),
                  pl.BlockSpec(blk, lambda bi, ti: (bi, ti, glu0 + 1)),
                  pl.BlockSpec(blk, lambda bi, ti: (bi, ti, glu0 + 2)),
                  pl.BlockSpec((None, CONV_W - 1, B_WIDTH), lambda bi, ti: (bi, 0, 0)),
                  lyr((CONV_W, B_WIDTH)), lyr((1, B_WIDTH)), lyr((1, B_WIDTH)), lyr((1, B_WIDTH)),
                  lyr((B_WIDTH, B_WIDTH)), lyr((1, B_WIDTH))],
        out_specs=[pl.BlockSpec(blk, lambda bi, ti: (bi, ti, 0)),
                   pl.BlockSpec((None, CONV_W - 1, B_WIDTH), lambda bi, ti: (bi, 0, 0))],
        out_shape=[jax.ShapeDtypeStruct((b, t, B_WIDTH), BF16),
                   jax.ShapeDtypeStruct((b, CONV_W - 1, B_WIDTH), F32)],
        scratch_shapes=[pltpu.VMEM((CONV_HALO + tt, B_WIDTH), F32),
                        pltpu.VMEM((tt, B_WIDTH), F32),
                        pltpu.VMEM((B_WIDTH, B_WIDTH), BF16)],
        compiler_params=_cparams("parallel", "arbitrary"),
        name="conv_module",
    )(z, z, z, hist, conv_w, row(conv_b), row(ln_g), row(ln_b), pw_w, row(pw_b))


def _even_mix_kernel(o0_ref, o1_ref, o2_ref, l0_ref, l1_ref, l2_ref, ga_ref, mix_ref):
    ls = [l0_ref[...], l1_ref[...], l2_ref[...]]
    m = jnp.maximum(jnp.maximum(ls[0], ls[1]), ls[2])
    es = [jnp.exp(l - m) for l in ls]
    inv = 1.0 / (es[0] + es[1] + es[2])
    ws = [e * inv for e in es]
    o_refs = (o0_ref, o1_ref, o2_ref)
    for h in range(A_HPG):
        cs = slice(h * HEAD_DIM, (h + 1) * HEAD_DIM)
        acc = ws[0][:, h:h + 1] * o_refs[0][:, cs]
        for g in range(1, A_NG):
            acc = acc + ws[g][:, h:h + 1] * o_refs[g][:, cs]
        mix_ref[:, cs] = (acc * _silu(ga_ref[:, cs])).astype(BF16)


def _even_mix(oas, lses, z2, *, tm):
    m = z2.shape[0]
    ga_blk = 3 * A_QKV // A_WIDTH
    wide = lambda c: pl.BlockSpec((tm, A_WIDTH), lambda i: (i, c))
    narrow = pl.BlockSpec((tm, LANES), lambda i: (i, 0))
    return pl.pallas_call(
        _even_mix_kernel,
        grid=(m // tm,),
        in_specs=[wide(0), wide(0), wide(0), narrow, narrow, narrow, wide(ga_blk)],
        out_specs=wide(0),
        out_shape=jax.ShapeDtypeStruct((m, A_WIDTH), BF16),
        compiler_params=_cparams("parallel"),
        name="even_mix",
    )(*oas, *lses, z2)


def _even_layer(x, a_caches, a_new, conv_hist, rel_bias, layer, norm_g, w_in, conv_w, conv_b, ln_g, ln_b, pw_w,
                pw_b, w_out, *, tm, tn_in, tn_out, tt):
    b, t, d = x.shape
    m = b * t
    x2 = x.reshape(m, d)
    z2 = _norm_matmul(x2, norm_g[layer], w_in, layer, tm=tm, tn=tn_in)
    z = z2.reshape(b, t, EVEN_IN)
    oas, lses, new_hists = [], [], []
    for g, (win, _) in enumerate(A_GROUPS):
        if a_caches is None:
            kcol = A_QKV + g * A_WIDTH
            vcol = 2 * A_QKV + g * A_WIDTH
            new_kv = jnp.stack([z[:, :, kcol:kcol + A_WIDTH], z[:, :, vcol:vcol + A_WIDTH]], axis=2)
            new_hists.append(new_kv[:, t - min(win, t):].reshape(b, min(win, t), 2, A_HPG, HEAD_DIM))
        else:
            assert a_caches[g].shape[2] == win
            o, lse, new = _a_attn_sample(z, a_caches[g], layer, None if a_new is None else a_new[g], rel_bias, g)
            new_hists.append(new)
            oas.append(o.reshape(m, A_WIDTH))
            lses.append(lse.reshape(m, LANES))
    if a_caches is None:
        mix_a = _a_attn_prompt(z, rel_bias).reshape(m, A_WIDTH)
    else:
        mix_a = _even_mix(oas, lses, z2, tm=min(tm, 256))
    mix_b, new_conv = _conv_module(z, conv_hist, layer, conv_w, conv_b, ln_g, ln_b, pw_w, pw_b, tt=tt)
    y = _matmul_res([mix_a, mix_b.reshape(m, B_WIDTH)], w_out, layer, x2, tm=tm, tn=tn_out)
    return y.reshape(b, t, d), new_hists, new_conv


ODD_KV0 = C_WIDTH
ODD_GL0 = C_WIDTH + 6 * C_KV
ODD_GP0 = ODD_GL0 + 3 * C_HEADS
ODD_Z = ODD_GL0 + 512
CMP_HALF = CMP_BLK // 2
CMP_K = CMP_HALF * HEAD_DIM
PAGES_PER_STEP = 16
CACHE_KINDS = 2 * C_KV_HEADS
CACHE_ROW_KINDS = 2 * CACHE_KINDS


def _cmp_partials_kernel(*refs, n_src, n_prefetch, interleaved):
    refs = refs[n_prefetch:]
    src = refs[:n_src]
    pos_ref, w1_ref, a_ref, b_ref, rows_ref, w1b_ref = refs[n_src:n_src + 6]
    n = rows_ref.shape[1] // CMP_HALF

    if n_prefetch:
        @pl.when(pl.program_id(1) == 0)
        def _():
            w1b_ref[...] = w1_ref[...].astype(BF16)
    else:
        w1b_ref[...] = w1_ref[...].astype(BF16)

    r0 = 0
    for r in src:
        nr = r.shape[0] // CACHE_ROW_KINDS if interleaved else r.shape[0]
        for c in range(CACHE_KINDS):
            if interleaved:
                rows_ref[c, r0:r0 + nr, :] = r[pl.ds(c, nr, stride=CACHE_ROW_KINDS), :]
            else:
                rows_ref[c, r0:r0 + nr, :] = r[:, c * HEAD_DIM:(c + 1) * HEAD_DIM]
        r0 += nr
    for kv in range(2):
        x = jnp.concatenate(
            [jnp.concatenate([rows_ref[kv * C_KV_HEADS + g, pl.ds(l, n, stride=CMP_HALF), :]
                              for l in range(CMP_HALF)], axis=1) for g in range(C_KV_HEADS)], axis=0)
        for half, out in ((0, a_ref), (1, b_ref)):
            y = jnp.dot((x + pos_ref[kv, half]).astype(BF16), w1b_ref[kv, half], preferred_element_type=F32)
            for g in range(C_KV_HEADS):
                out[kv * C_KV_HEADS + g] = y[g * n:(g + 1) * n]


def _cmp_weights(cmp_pos, cmp_w1):
    n = cmp_pos.shape[0]
    return cmp_pos.reshape(n, 2, 2, 1, CMP_K), cmp_w1.reshape(n, 2, 2, CMP_K, CMP_HID)


def _cmp_partials_prompt(z, layer, cmp_pos, cmp_w1):
    b, t, _ = z.shape
    nch = t // CMP_STRIDE
    pos, w1 = _cmp_weights(cmp_pos, cmp_w1)
    kern = functools.partial(_cmp_partials_kernel, n_src=1, n_prefetch=0, interleaved=False)
    out = jax.ShapeDtypeStruct((b, 4, nch, CMP_HID), F32)
    ospec = pl.BlockSpec((None, 4, nch, CMP_HID), lambda bi: (bi, 0, 0, 0))
    return pl.pallas_call(
        kern,
        grid=(b,),
        in_specs=[pl.BlockSpec((None, t, 2 * C_KV), lambda bi: (bi, 0, ODD_KV0 // (2 * C_KV))),
                  pl.BlockSpec((None,) + pos.shape[1:], lambda bi: (layer, 0, 0, 0, 0)),
                  pl.BlockSpec((None,) + w1.shape[1:], lambda bi: (layer, 0, 0, 0, 0))],
        out_specs=[ospec, ospec],
        out_shape=[out, out],
        scratch_shapes=[pltpu.VMEM((2 * C_KV_HEADS, t, HEAD_DIM), F32), pltpu.VMEM(w1.shape[1:], BF16)],
        compiler_params=_cparams("parallel"),
        name="cmp_partials_prompt",
    )(z, pos, w1)


def _cmp_partials_paged(cache, layer, page_table, cmp_pos, cmp_w1):
    b, n_pages = page_table.shape
    page = cache.shape[2] // CACHE_ROW_KINDS
    pps = PAGES_PER_STEP
    assert n_pages % pps == 0 and page % CMP_STRIDE == 0
    nch_step = pps * page // CMP_STRIDE
    nch = n_pages * page // CMP_STRIDE
    pos, w1 = _cmp_weights(cmp_pos, cmp_w1)
    kern = functools.partial(_cmp_partials_kernel, n_src=pps, n_prefetch=1, interleaved=True)
    out = jax.ShapeDtypeStruct((b, 4, nch, CMP_HID), F32)
    ospec = pl.BlockSpec((None, 4, nch_step, CMP_HID), lambda bi, pg, pt: (bi, 0, pg, 0))

    def page_spec(j):
        return pl.BlockSpec((None, None, page * CACHE_ROW_KINDS, HEAD_DIM),
                            lambda bi, pg, pt: (layer, pt[bi, pg * pps + j], 0, 0))

    grid_spec = pltpu.PrefetchScalarGridSpec(
        num_scalar_prefetch=1,
        grid=(b, n_pages // pps),
        in_specs=[page_spec(j) for j in range(pps)] + [
            pl.BlockSpec((None,) + pos.shape[1:], lambda bi, pg, pt: (layer, 0, 0, 0, 0)),
            pl.BlockSpec((None,) + w1.shape[1:], lambda bi, pg, pt: (layer, 0, 0, 0, 0))],
        out_specs=[ospec, ospec],
        scratch_shapes=[pltpu.VMEM((2 * C_KV_HEADS, pps * page, HEAD_DIM), F32), pltpu.VMEM(w1.shape[1:], BF16)])
    return pl.pallas_call(
        kern, grid_spec=grid_spec, out_shape=[out, out],
        compiler_params=_cparams("parallel", "arbitrary"),
        name="cmp_partials_paged",
    )(page_table, *([cache] * pps), pos, w1)


def _overlap_np(n_ch, n_sel, ns_pad):
    ci = np.arange(n_ch)[:, None]
    si = np.arange(ns_pad)[None, :]
    ov = (ci * CMP_STRIDE < (si + 1) * SEL_BLK) & (ci * CMP_STRIDE + CMP_BLK > si * SEL_BLK)
    ov &= (ci < n_ch - 1) & (si < n_sel)
    return ov.astype(np.float32)


def _cmp_select_kernel(q_ref, ak_ref, bk_ref, av_ref, bv_ref, b1_ref, w2_ref, ov_ref, oc_ref, sel_ref,
                       *, qpos0, n_sel):
    qi = pl.program_id(2)
    tq = q_ref.shape[0]
    nch = ak_ref.shape[0]
    ns_pad = ov_ref.shape[1]

    def finish(a_ref, b_ref, kv):
        hid = a_ref[...] + pltpu.roll(b_ref[...], nch - 1, 0) + b1_ref[kv:kv + 1, :]
        return jnp.dot(_silu(hid).astype(BF16), w2_ref[kv].astype(BF16),
                       preferred_element_type=F32).astype(BF16)

    k_cmp = finish(ak_ref, bk_ref, 0)
    v_cmp = finish(av_ref, bv_ref, 1)
    if tq % LANES == 0:
        cols = C_HPG * tq
        assert tq & (tq - 1) == 0
        qpos_t = qpos0 + qi * tq + (lax.broadcasted_iota(jnp.int32, (nch, cols), 1) & (tq - 1))
        end_t = lax.broadcasted_iota(jnp.int32, (nch, cols), 0) * CMP_STRIDE + (CMP_BLK - 1)
        ok = (end_t <= qpos_t) & (end_t < (nch - 1) * CMP_STRIDE + CMP_BLK - 1)
        q = jnp.concatenate([(q_ref[:, h * HEAD_DIM:(h + 1) * HEAD_DIM] * SCALE).astype(BF16)
                             for h in range(C_HPG)], axis=0)
        s = lax.dot_general(k_cmp, q, (((1,), (1,)), ((), ())), preferred_element_type=F32)
        s = jnp.where(ok, s, NEG)
        m = jnp.max(s, axis=0, keepdims=True)
        e = jnp.where(ok, jnp.exp(s - m), 0.0)
        den = jnp.sum(e, axis=0, keepdims=True)
        p = e / jnp.where(den > 0.0, den, 1.0)
        psum = p[:, 0:tq]
        for h in range(1, C_HPG):
            psum = psum + p[:, h * tq:(h + 1) * tq]
        v_t = v_cmp.astype(F32).T.astype(BF16)
        oc = jnp.dot(v_t, p.astype(BF16), preferred_element_type=F32)
        for h in range(C_HPG):
            oc_ref[:, h * HEAD_DIM:(h + 1) * HEAD_DIM] = oc[:, h * tq:(h + 1) * tq].T
        imp = jnp.dot(ov_ref[...].T, psum, preferred_element_type=F32, precision=lax.Precision.HIGHEST)
        blk = lax.broadcasted_iota(jnp.int32, (ns_pad, tq), 0)
        qblk = (qpos0 + qi * tq + lax.broadcasted_iota(jnp.int32, (ns_pad, tq), 1)) // SEL_BLK
        forced = (blk == 0) | (blk == qblk) | (blk == qblk - 1)
        allowed = (blk <= qblk) & (blk < n_sel)
        score = jnp.where(allowed, jnp.where(forced, -NEG, imp), -1.0)
        rank = jnp.zeros((ns_pad, tq), jnp.int32)
        for j in range(n_sel):
            rj = score[j:j + 1, :]
            before = (rj > score) | ((rj == score) & (j < blk))
            rank = rank + before.astype(jnp.int32)
        sel_ref[...] = (allowed & (rank < SEL_N)).astype(F32).T
        return
    qpos = qpos0 + qi * tq + lax.broadcasted_iota(jnp.int32, (tq, nch), 0)
    cmp_end = lax.broadcasted_iota(jnp.int32, (tq, nch), 1) * CMP_STRIDE + (CMP_BLK - 1)
    ok = (cmp_end <= qpos) & (cmp_end < (nch - 1) * CMP_STRIDE + CMP_BLK - 1)
    q = jnp.concatenate([(q_ref[:, h * HEAD_DIM:(h + 1) * HEAD_DIM] * SCALE).astype(BF16) for h in range(C_HPG)],
                        axis=0)
    s = lax.dot_general(q, k_cmp, (((1,), (1,)), ((), ())), preferred_element_type=F32)
    s = jnp.where(ok[None], s.reshape(C_HPG, tq, nch), NEG)
    m = jnp.max(s, axis=-1, keepdims=True)
    e = jnp.where(ok[None], jnp.exp(s - m), 0.0)
    den = jnp.sum(e, axis=-1, keepdims=True)
    p = e / jnp.where(den > 0.0, den, 1.0)
    psum = jnp.sum(p, axis=0)
    oc = jnp.dot(p.reshape(C_HPG * tq, nch).astype(BF16), v_cmp, preferred_element_type=F32)
    for h in range(C_HPG):
        oc_ref[:, h * HEAD_DIM:(h + 1) * HEAD_DIM] = oc[h * tq:(h + 1) * tq]
    imp = jnp.dot(psum, ov_ref[...], preferred_element_type=F32, precision=lax.Precision.HIGHEST)
    blk = lax.broadcasted_iota(jnp.int32, (tq, ns_pad), 1)
    qblk = (qpos0 + qi * tq + lax.broadcasted_iota(jnp.int32, (tq, ns_pad), 0)) // SEL_BLK
    forced = (blk == 0) | (blk == qblk) | (blk == qblk - 1)
    allowed = (blk <= qblk) & (blk < n_sel)
    score = jnp.where(allowed, jnp.where(forced, -NEG, imp), -1.0)
    rank = jnp.zeros((tq, ns_pad), jnp.int32)
    for j in range(n_sel):
        cj = score[:, j:j + 1]
        before = (cj > score) | ((cj == score) & (j < blk))
        rank = rank + before.astype(jnp.int32)
    sel_ref[...] = (allowed & (rank < SEL_N)).astype(F32)


def _cmp_select(z, a, bm, layer, cmp_b1, cmp_w2, *, tq, qpos0, n_sel):
    b, t, _ = z.shape
    nch = a.shape[2]
    ns_pad = -(-n_sel // LANES) * LANES
    ov = jnp.asarray(_overlap_np(nch, n_sel, ns_pad))
    kern = functools.partial(_cmp_select_kernel, qpos0=qpos0, n_sel=n_sel)
    part = lambda kv: pl.BlockSpec((None, None, nch, CMP_HID), lambda bi, g, qi: (bi, kv * C_KV_HEADS + g, 0, 0))
    return pl.pallas_call(
        kern,
        grid=(b, C_KV_HEADS, t // tq),
        in_specs=[pl.BlockSpec((None, tq, C_HPG * HEAD_DIM), lambda bi, g, qi: (bi, qi, g)),
                  part(0), part(0), part(1), part(1),
                  pl.BlockSpec((None, 2, CMP_HID), lambda bi, g, qi: (layer, 0, 0)),
                  pl.BlockSpec((None, 2, CMP_HID, HEAD_DIM), lambda bi, g, qi: (layer, 0, 0, 0)),
                  pl.BlockSpec((nch, ns_pad), lambda bi, g, qi: (0, 0))],
        out_specs=[pl.BlockSpec((None, tq, C_HPG * HEAD_DIM), lambda bi, g, qi: (bi, qi, g)),
                   pl.BlockSpec((None, None, tq, ns_pad), lambda bi, g, qi: (bi, g, qi, 0))],
        out_shape=[jax.ShapeDtypeStruct((b, t, C_WIDTH), F32),
                   jax.ShapeDtypeStruct((b, C_KV_HEADS, t, ns_pad), F32)],
        compiler_params=_cparams("parallel", "parallel", "arbitrary"),
        name="cmp_select",
    )(z, a, bm, a, bm, cmp_b1, cmp_w2, ov)


def _sel_prompt_kernel(q_ref, k_ref, v_ref, rev_ref, sel_ref, o_ref,
                       bias_ref, msk_ref, qb_ref, m_ref, l_ref, acc_ref):
    qi = pl.program_id(2)
    tq = q_ref.shape[0]
    nt = bias_ref.shape[0] - 1
    tk = msk_ref.shape[1]
    ratio = tk // tq
    ns_pad = sel_ref.shape[1]

    @pl.when(qi == 0)
    def _():
        bias_ref[0] = jnp.full(bias_ref.shape[1:], NEG, F32)
        for h in range(C_HPG):
            for delta in range(nt):
                off = (nt - 1 - delta) * tq
                bias_ref[delta + 1, :, h * tq:(h + 1) * tq] = (
                    _toeplitz(rev_ref[h, :, off:off + 2 * tq], tq).T * LOG2E)

    sel_t = sel_ref[...].T.astype(BF16)
    erow = lax.broadcasted_iota(jnp.int32, (tk, ns_pad), 0) // SEL_BLK
    ecol = lax.broadcasted_iota(jnp.int32, (tk, ns_pad), 1)
    for t in range(msk_ref.shape[0]):
        expand = (ecol == erow + t * (tk // SEL_BLK)).astype(BF16)
        hit = jnp.dot(expand, sel_t, preferred_element_type=F32)
        msk_ref[t] = jnp.where(hit > 0.5, 0.0, NEG)
    for h in range(C_HPG):
        qb_ref[h * tq:(h + 1) * tq, :] = (q_ref[:, h * HEAD_DIM:(h + 1) * HEAD_DIM] * (SCALE * LOG2E)).astype(BF16)
    m_ref[...] = jnp.full(m_ref.shape, NEG, F32)
    l_ref[...] = jnp.zeros(l_ref.shape, F32)
    acc_ref[...] = jnp.zeros(acc_ref.shape, F32)

    def body(kj, carry):
        keys = pl.ds(pl.multiple_of(kj * tk, tk), tk)
        k = k_ref[keys, :].astype(BF16)
        v_t = v_ref[keys, :].T.astype(BF16)
        d0 = qi - ratio * kj
        parts = [bias_ref[d0 + 1 - i] for i in range(ratio)]
        bias = parts[0] if ratio == 1 else jnp.concatenate(parts, axis=0)
        s = lax.dot_general(k, qb_ref[...], (((1,), (1,)), ((), ())), preferred_element_type=F32)
        s = s + bias + jnp.concatenate([msk_ref[kj]] * C_HPG, axis=1)
        m_old = m_ref[...]
        m_new = jnp.maximum(m_old, jnp.max(s, axis=0, keepdims=True))
        alpha = jnp.exp2(m_old - m_new)
        e = jnp.exp2(s - m_new)
        m_ref[...] = m_new
        l_ref[...] = alpha * l_ref[...] + jnp.sum(e, axis=0, keepdims=True)
        acc_ref[...] = alpha * acc_ref[...] + jnp.dot(v_t, e.astype(BF16), preferred_element_type=F32)
        return carry

    lax.fori_loop(0, (qi + ratio) // ratio, body, 0)
    o = acc_ref[...] / l_ref[...]
    for h in range(C_HPG):
        o_ref[:, h * HEAD_DIM:(h + 1) * HEAD_DIM] = o[:, h * tq:(h + 1) * tq].T


SEL_TQ = 128
SEL_KEY_RATIO = 2


def _sel_attn_prompt(z, sel, rel_bias):
    b, t, _ = z.shape
    tq = SEL_TQ
    nt = t // tq
    ratio = SEL_KEY_RATIO
    assert nt % ratio == 0
    ns_pad = sel.shape[3]
    rev = _c_rev_table(rel_bias, t, tq, t)
    kcol = (ODD_KV0 + 2 * C_KV) // HEAD_DIM
    vcol = (ODD_KV0 + 3 * C_KV) // HEAD_DIM
    return pl.pallas_call(
        _sel_prompt_kernel,
        grid=(b, C_KV_HEADS, t // tq),
        in_specs=[pl.BlockSpec((None, tq, C_HPG * HEAD_DIM), lambda bi, g, qi: (bi, qi, g)),
                  pl.BlockSpec((None, t, HEAD_DIM), lambda bi, g, qi: (bi, 0, kcol + g)),
                  pl.BlockSpec((None, t, HEAD_DIM), lambda bi, g, qi: (bi, 0, vcol + g)),
                  pl.BlockSpec((C_HPG, 1, t + tq), lambda bi, g, qi: (g, 0, 0)),
                  pl.BlockSpec((None, None, tq, ns_pad), lambda bi, g, qi: (bi, g, qi, 0))],
        out_specs=pl.BlockSpec((None, tq, C_HPG * HEAD_DIM), lambda bi, g, qi: (bi, qi, g)),
        out_shape=jax.ShapeDtypeStruct((b, t, C_WIDTH), F32),
        scratch_shapes=[pltpu.VMEM((nt + 1, tq, C_HPG * tq), F32),
                        pltpu.VMEM((nt // ratio, ratio * tq, tq), F32),
                        pltpu.VMEM((C_HPG * tq, HEAD_DIM), BF16),
                        pltpu.VMEM((1, C_HPG * tq), F32),
                        pltpu.VMEM((1, C_HPG * tq), F32),
                        pltpu.VMEM((HEAD_DIM, C_HPG * tq), F32)],
        compiler_params=_cparams("parallel", "parallel", "arbitrary"),
        name="sel_attn_prompt",
    )(z, z, z, rev, sel)


def _sel_paged_kernel(pt_ref, *refs, n_src, past):
    del pt_ref
    pages = refs[:n_src]
    q_ref, new_ref, sel_ref, bias_ref, bnew_ref, o_ref, m_ref, l_ref, acc_ref = refs[n_src:]
    pg = pl.program_id(1)
    tq = q_ref.shape[0]
    rows = C_HPG * tq
    page = pages[0].shape[0] // CACHE_ROW_KINDS
    keys = n_src * page
    ns_pad = sel_ref.shape[2]

    def page_rows(r, kind):
        return r[pl.ds(CACHE_KINDS + kind, page, stride=CACHE_ROW_KINDS), :]

    def q_rows(g):
        return jnp.concatenate(
            [(q_ref[:, (g * C_HPG + h) * HEAD_DIM:(g * C_HPG + h + 1) * HEAD_DIM] * SCALE).astype(BF16)
             for h in range(C_HPG)], axis=0)

    @pl.when(pg == 0)
    def _():
        for g in range(C_KV_HEADS):
            k = new_ref[:, g * HEAD_DIM:(g + 1) * HEAD_DIM].astype(BF16)
            v = new_ref[:, C_KV + g * HEAD_DIM:C_KV + (g + 1) * HEAD_DIM].astype(BF16)
            s = lax.dot_general(q_rows(g), k, (((1,), (1,)), ((), ())), preferred_element_type=F32)
            s = s + bnew_ref[g * C_HPG:(g + 1) * C_HPG].reshape(rows, tq)
            m = jnp.max(s, axis=-1, keepdims=True)
            e = jnp.exp(s - m)
            m_ref[g] = m
            l_ref[g] = jnp.sum(e, axis=-1, keepdims=True)
            acc_ref[g] = jnp.dot(e.astype(BF16), v, preferred_element_type=F32)

    srow = lax.broadcasted_iota(jnp.int32, (ns_pad, keys), 0)
    scol = lax.broadcasted_iota(jnp.int32, (ns_pad, keys), 1) // SEL_BLK + pg * (keys // SEL_BLK)
    expand = (srow == scol).astype(BF16)
    for g in range(C_KV_HEADS):
        k = jnp.concatenate([page_rows(r, g) for r in pages], axis=0).astype(BF16)
        v = jnp.concatenate([page_rows(r, C_KV_HEADS + g) for r in pages], axis=0).astype(BF16)
        hit = jnp.dot(sel_ref[g].astype(BF16), expand, preferred_element_type=F32)
        mask = jnp.where(hit > 0.5, 0.0, NEG)
        s = lax.dot_general(q_rows(g), k, (((1,), (1,)), ((), ())), preferred_element_type=F32)
        s = (s.reshape(C_HPG, tq, keys) + bias_ref[g * C_HPG:(g + 1) * C_HPG] + mask[None]).reshape(rows, keys)
        m_old = m_ref[g]
        m_new = jnp.maximum(m_old, jnp.max(s, axis=-1, keepdims=True))
        alpha = jnp.exp(m_old - m_new)
        e = jnp.exp(s - m_new)
        m_ref[g] = m_new
        l_ref[g] = alpha * l_ref[g] + jnp.sum(e, axis=-1, keepdims=True)
        acc_ref[g] = alpha * acc_ref[g] + jnp.dot(e.astype(BF16), v, preferred_element_type=F32)

    @pl.when(pg == pl.num_programs(1) - 1)
    def _():
        for g in range(C_KV_HEADS):
            o = acc_ref[g] / l_ref[g]
            for h in range(C_HPG):
                o_ref[:, (g * C_HPG + h) * HEAD_DIM:(g * C_HPG + h + 1) * HEAD_DIM] = o[h * tq:(h + 1) * tq]


def _sel_attn_paged(z, sel, cache, layer, page_table, rel_bias):
    b, tq, _ = z.shape
    n_pages = page_table.shape[1]
    page = cache.shape[2] // CACHE_ROW_KINDS
    past = n_pages * page
    pps = PAGES_PER_STEP
    keys = pps * page
    ns_pad = sel.shape[3]
    tab = _c_tab(rel_bias)
    rev = tab[_t5_bucket_np(past + tq - 1 - np.arange(past + tq - 1))].T.astype(F32)
    bias = jnp.stack([rev[:, tq - 1 - t:tq - 1 - t + past] for t in range(tq)], axis=1)
    dnew = np.arange(tq)[:, None] - np.arange(tq)[None, :]
    bnew = jnp.where(jnp.asarray(dnew >= 0)[None],
                     jnp.transpose(tab[_t5_bucket_np(np.clip(dnew, 0, None))], (2, 0, 1)), NEG).astype(F32)
    kern = functools.partial(_sel_paged_kernel, n_src=pps, past=past)

    def page_spec(j):
        return pl.BlockSpec((None, None, page * CACHE_ROW_KINDS, HEAD_DIM),
                            lambda bi, pg, pt: (layer, pt[bi, pg * pps + j], 0, 0))

    rows = C_HPG * tq
    grid_spec = pltpu.PrefetchScalarGridSpec(
        num_scalar_prefetch=1,
        grid=(b, n_pages // pps),
        in_specs=[page_spec(j) for j in range(pps)] + [
            pl.BlockSpec((None, tq, C_WIDTH), lambda bi, pg, pt: (bi, 0, 0)),
            pl.BlockSpec((None, tq, 2 * C_KV), lambda bi, pg, pt: (bi, 0, (ODD_KV0 + 2 * C_KV) // (2 * C_KV))),
            pl.BlockSpec((None, C_KV_HEADS, tq, ns_pad), lambda bi, pg, pt: (bi, 0, 0, 0)),
            pl.BlockSpec((C_HEADS, tq, keys), lambda bi, pg, pt: (0, 0, pg)),
            pl.BlockSpec((C_HEADS, tq, tq), lambda bi, pg, pt: (0, 0, 0))],
        out_specs=pl.BlockSpec((None, tq, C_WIDTH), lambda bi, pg, pt: (bi, 0, 0)),
        scratch_shapes=[pltpu.VMEM((C_KV_HEADS, rows, 1), F32), pltpu.VMEM((C_KV_HEADS, rows, 1), F32),
                        pltpu.VMEM((C_KV_HEADS, rows, HEAD_DIM), F32)])
    return pl.pallas_call(
        kern, grid_spec=grid_spec, out_shape=jax.ShapeDtypeStruct((b, tq, C_WIDTH), F32),
        compiler_params=_cparams("parallel", "arbitrary"),
        name="sel_attn_paged",
    )(page_table, *([cache] * pps), z, z, sel, bias, bnew)


def _win_prompt_kernel(*refs, n_tiles):
    q_ref = refs[0]
    kv_refs = refs[1:1 + n_tiles]
    rev_ref, o_ref, bias_scr = refs[1 + n_tiles:]
    qi = pl.program_id(1)
    tq = q_ref.shape[0]
    lk = n_tiles * tq
    cols = C_HPG * tq

    @pl.when(qi == 0)
    def _():
        for h in range(C_HEADS):
            g, hh = divmod(h, C_HPG)
            bias_scr[g, :, hh * tq:(hh + 1) * tq] = _toeplitz(rev_ref[h], tq).T * LOG2E

    clamped = lax.broadcasted_iota(jnp.int32, (lk, cols), 0) < (n_tiles - 1 - qi) * tq
    for g in range(C_KV_HEADS):
        q = jnp.concatenate(
            [(q_ref[:, (g * C_HPG + h) * HEAD_DIM:(g * C_HPG + h + 1) * HEAD_DIM] * (SCALE * LOG2E)).astype(BF16)
             for h in range(C_HPG)], axis=0)
        k = jnp.concatenate([r[:, g * HEAD_DIM:(g + 1) * HEAD_DIM] for r in kv_refs], axis=0).astype(BF16)
        v_t = jnp.concatenate([r[:, C_KV + g * HEAD_DIM:C_KV + (g + 1) * HEAD_DIM] for r in kv_refs],
                              axis=0).T.astype(BF16)
        s = lax.dot_general(k, q, (((1,), (1,)), ((), ())), preferred_element_type=F32) + bias_scr[g]
        s = jnp.where(clamped, NEG, s)
        m = jnp.max(s, axis=0, keepdims=True)
        e = jnp.exp2(s - m)
        den = jnp.sum(e, axis=0, keepdims=True)
        o = jnp.dot(v_t, e.astype(BF16), preferred_element_type=F32) / den
        for h in range(C_HPG):
            o_ref[:, (g * C_HPG + h) * HEAD_DIM:(g * C_HPG + h + 1) * HEAD_DIM] = o[:, h * tq:(h + 1) * tq].T


def _win_attn_prompt(z, rel_bias, *, tq):
    b, t, _ = z.shape
    n_tiles = WIN // tq + 1
    rev = _c_rev_table(rel_bias, n_tiles * tq, tq, WIN)
    kv_blk = (ODD_KV0 + 4 * C_KV) // (2 * C_KV)
    kern = functools.partial(_win_prompt_kernel, n_tiles=n_tiles)

    def kv_spec(p):
        return pl.BlockSpec((None, tq, 2 * C_KV),
                            lambda bi, qi: (bi, jnp.maximum(qi - (n_tiles - 1 - p), 0), kv_blk))

    return pl.pallas_call(
        kern,
        grid=(b, t // tq),
        in_specs=[pl.BlockSpec((None, tq, C_WIDTH), lambda bi, qi: (bi, qi, 0))]
                 + [kv_spec(p) for p in range(n_tiles)]
                 + [pl.BlockSpec((C_HEADS, 1, (n_tiles + 1) * tq), lambda bi, qi: (0, 0, 0))],
        out_specs=pl.BlockSpec((None, tq, C_WIDTH), lambda bi, qi: (bi, qi, 0)),
        out_shape=jax.ShapeDtypeStruct((b, t, C_WIDTH), F32),
        scratch_shapes=[pltpu.VMEM((C_KV_HEADS, n_tiles * tq, C_HPG * tq), F32)],
        compiler_params=_cparams("parallel", "arbitrary"),
        name="win_attn_prompt",
    )(z, *([z] * n_tiles), rev)


def _win_attn_sample(z, win_full, rel_bias):
    b, tq, _ = z.shape
    lk = win_full.shape[1]
    dist = (lk - tq) + np.arange(tq)[:, None] - np.arange(lk)[None, :]
    valid = (dist >= 0) & (dist <= WIN)
    bias = jnp.transpose(_c_tab(rel_bias)[_t5_bucket_np(np.clip(dist, 0, None))], (2, 0, 1))
    bias = jnp.where(jnp.asarray(valid)[None], bias, NEG).astype(F32)
    kern = functools.partial(_tile_attn_kernel, nh=C_HEADS, rep=C_HPG, n_tiles=1, k_col=0, v_col=C_KV,
                             shared_kv=True, with_lse=False, lead_axis=0)
    return pl.pallas_call(
        kern,
        grid=(b,),
        in_specs=[pl.BlockSpec((None, tq, C_WIDTH), lambda bi: (bi, 0, 0)),
                  pl.BlockSpec((None, lk, 2 * C_KV), lambda bi: (bi, 0, 0)),
                  pl.BlockSpec((C_HEADS, tq, lk), lambda bi: (0, 0, 0))],
        out_specs=pl.BlockSpec((None, tq, C_WIDTH), lambda bi: (bi, 0, 0)),
        out_shape=jax.ShapeDtypeStruct((b, tq, C_WIDTH), F32),
        compiler_params=_cparams("parallel"),
        name="win_attn_sample",
    )(z, win_full, bias)


def _odd_mix_kernel(oc_ref, os_ref, ow_ref, gl_ref, gp_ref, mix_ref):
    gates = jax.nn.sigmoid(gl_ref[...])
    branches = (oc_ref, os_ref, ow_ref)
    for h in range(C_HEADS):
        cs = slice(h * HEAD_DIM, (h + 1) * HEAD_DIM)
        acc = gates[:, h:h + 1] * branches[0][:, cs]
        for br in range(1, 3):
            lane = br * C_HEADS + h
            acc = acc + gates[:, lane:lane + 1] * branches[br][:, cs]
        mix_ref[:, cs] = (acc * _silu(gp_ref[:, cs])).astype(BF16)


def _odd_mix(oc, osel, ow, z2, zgp, *, tm):
    m = z2.shape[0]
    wide = pl.BlockSpec((tm, C_WIDTH), lambda i: (i, 0))
    return pl.pallas_call(
        _odd_mix_kernel,
        grid=(m // tm,),
        in_specs=[wide, wide, wide, pl.BlockSpec((tm, LANES), lambda i: (i, ODD_GL0 // LANES)), wide],
        out_specs=wide,
        out_shape=jax.ShapeDtypeStruct((m, C_WIDTH), BF16),
        compiler_params=_cparams("parallel"),
        name="odd_mix",
    )(oc, osel, ow, z2, zgp)


def _odd_layer(x, past, win_hist, rel_bias, layer, norm_g, w_in, w_in_gp, cmp_pos, cmp_w1, cmp_b1, cmp_w2, w_out,
               *, tm, tn_in, tn_out, tq):
    b, t, d = x.shape
    m = b * t
    x2 = x.reshape(m, d)
    z2 = _norm_matmul(x2, norm_g[layer], w_in, layer, tm=tm, tn=tn_in, n_out=ODD_Z)
    zgp = _norm_matmul(x2, norm_g[layer], w_in_gp, layer, tm=tm, tn=tn_in)
    z = z2.reshape(b, t, ODD_Z)
    rows = z[:, :, ODD_KV0:ODD_KV0 + 4 * C_KV].reshape(b, t, 4, C_KV_HEADS, HEAD_DIM)
    win_new = z[:, :, ODD_KV0 + 4 * C_KV:ODD_KV0 + 6 * C_KV]
    if past is None:
        length = t
        a, bm = _cmp_partials_prompt(z, layer, cmp_pos, cmp_w1)
        qpos0 = 0
    else:
        cache, page_table = past
        p_len = page_table.shape[1] * (cache.shape[2] // CACHE_ROW_KINDS)
        length = p_len + t
        assert length // CMP_STRIDE == p_len // CMP_STRIDE and p_len % SEL_BLK == 0
        a, bm = _cmp_partials_paged(cache, layer, page_table, cmp_pos, cmp_w1)
        qpos0 = p_len
    n_sel = -(-length // SEL_BLK)
    oc, sel = _cmp_select(z, a, bm, layer, cmp_b1, cmp_w2, tq=tq, qpos0=qpos0, n_sel=n_sel)
    if past is None:
        osel = _sel_attn_prompt(z, sel, rel_bias)
        ow = _win_attn_prompt(z, rel_bias, tq=tq)
        new_win = win_new[:, t - min(WIN, t):].reshape(b, min(WIN, t), 2, C_KV_HEADS, HEAD_DIM)
    else:
        osel = _sel_attn_paged(z, sel, cache, layer, page_table, rel_bias)
        win_full = jnp.concatenate([win_hist.reshape(b, -1, 2 * C_KV), win_new], axis=1)
        ow = _win_attn_sample(z, win_full, rel_bias)
        lk = win_full.shape[1]
        keep = min(WIN, length)
        new_win = win_full[:, lk - keep:].reshape(b, keep, 2, C_KV_HEADS, HEAD_DIM)
    mix = _odd_mix(oc.reshape(m, C_WIDTH), osel.reshape(m, C_WIDTH), ow.reshape(m, C_WIDTH), z2, zgp,
                   tm=min(tm, 256))
    y = _matmul_res([mix], w_out, layer, x2, tm=tm, tn=tn_out)
    return y.reshape(b, t, d), rows, new_win


def _run_trunk(x, a_caches, conv_state, c_cache, c_win, page_table, rel_bias, norm_even, w_in_even, conv_w,
               conv_b, conv_ln_g, conv_ln_b, conv_pw_w, conv_pw_b, w_out_even, norm_odd, w_in_odd, w_in_odd_gp,
               cmp_pos, cmp_w1, cmp_b1, cmp_w2, w_out_odd, final_norm, *, tm, tt, tq):
    b, t, d = x.shape
    prompt = a_caches is None
    depth = norm_even.shape[0] + norm_odd.shape[0]
    new_a = [[] for _ in A_GROUPS]
    a_new = None
    new_conv, new_rows, new_win = [], [], []
    for depth_i in range(depth):
        i = depth_i // 2
        if depth_i % 2 == 0:
            conv_hist = jnp.zeros((b, CONV_W - 1, B_WIDTH), F32) if prompt else conv_state[i]
            x, hists, conv = _even_layer(x, a_caches, a_new, conv_hist, rel_bias, i, norm_even, w_in_even, conv_w,
                                         conv_b, conv_ln_g, conv_ln_b, conv_pw_w, conv_pw_b,
                                         w_out_even, tm=tm, tn_in=512, tn_out=512, tt=tt)
            if prompt:
                for g in range(A_NG):
                    new_a[g].append(hists[g])
            else:
                a_new = hists
            new_conv.append(conv)
        else:
            past = None if prompt else (c_cache, page_table)
            x, rows, win = _odd_layer(x, past, None if prompt else c_win[i], rel_bias, i, norm_odd,
                                      w_in_odd, w_in_odd_gp, cmp_pos, cmp_w1, cmp_b1, cmp_w2,
                                      w_out_odd, tm=tm, tn_in=512, tn_out=512, tq=tq)
            new_rows.append(rows)
            new_win.append(win)
    y = _rms_norm(x.reshape(b * t, d), final_norm, tm=min(tm, 256)).reshape(b, t, d)
    if prompt:
        a_out = [jnp.stack(a) for a in new_a]
    else:
        a_out = [n.reshape(c.shape) for n, c in zip(a_new, a_caches)]
    return (y, a_out, jnp.stack(new_conv), jnp.stack(new_rows), jnp.stack(new_win))


def kernel(x_prompt, x_sample, cache_a_kv0, cache_a_kv1, cache_a_kv2, state_b_conv, cache_c_kv, cache_c_win, page_table, rel_bias, norm_even, w_in_even, conv_w, conv_b, conv_ln_g, conv_ln_b, conv_pw_w, conv_pw_b, w_out_even, norm_odd, w_in_odd, cmp_pos, cmp_w1, cmp_b1, cmp_w2, w_out_odd, final_norm):
    w_in_odd_gp = w_in_odd[:, :, ODD_GP0:]
    weights = (rel_bias, norm_even, w_in_even, conv_w, conv_b, conv_ln_g, conv_ln_b, conv_pw_w, conv_pw_b,
               w_out_even, norm_odd, w_in_odd, w_in_odd_gp, cmp_pos, cmp_w1, cmp_b1, cmp_w2, w_out_odd, final_norm)
    y_p, a_p, conv_p, rows_p, win_p = _run_trunk(x_prompt, None, None, None, None, None, *weights,
                                                 tm=2048, tt=256, tq=128)
    n_odd, n_pool, page = cache_c_kv.shape[:3]
    c_cache = cache_c_kv.reshape(n_odd, n_pool, page * CACHE_ROW_KINDS, HEAD_DIM)
    db, dt = x_sample.shape[:2]
    y_s, a_s, conv_s, rows_s, win_s = _run_trunk(x_sample, (cache_a_kv0, cache_a_kv1, cache_a_kv2), state_b_conv,
                                                 c_cache, cache_c_win, page_table, *weights,
                                                 tm=db * dt, tt=dt, tq=dt)
    return (y_p, y_s, a_p[0], a_p[1], a_p[2], conv_p, rows_p, win_p, a_s[0], a_s[1], a_s[2], conv_s, rows_s, win_s)
```

```python
import functools
import math

import numpy as np
import jax
import jax.numpy as jnp
from jax import lax
from jax.experimental import pallas as pl
from jax.experimental.pallas import tpu as pltpu

F32 = jnp.float32
BF16 = jnp.bfloat16

D_MODEL = 2048
HEAD_DIM = 128
LANES = 128
SUBLANES = 8
A_GROUPS = ((128, 1), (512, 4), (2048, 16))
A_NG = 3
A_HPG = 8
A_QKV = A_NG * A_HPG * HEAD_DIM
A_WIDTH = A_HPG * HEAD_DIM
B_WIDTH = 1024
CONV_W = 31
CONV_HALO = 32
EVEN_IN = 3 * A_QKV + A_WIDTH + 3 * B_WIDTH
C_HEADS = 16
C_KV_HEADS = 2
C_HPG = C_HEADS // C_KV_HEADS
C_WIDTH = C_HEADS * HEAD_DIM
C_KV = C_KV_HEADS * HEAD_DIM
CMP_BLK = 32
CMP_STRIDE = 16
CMP_HID = 128
SEL_BLK = 64
SEL_N = 16
WIN = 512
NUM_BUCKETS = 32
MAX_DIST = 2048
EPS = 1e-6
NEG = -1e30
SCALE = HEAD_DIM ** -0.5
LOG2E = math.log2(math.e)
VMEM_LIMIT = 56 * 1024 * 1024


def _cparams(*sem):
    return pltpu.CompilerParams(dimension_semantics=sem, vmem_limit_bytes=VMEM_LIMIT)


def _t5_bucket_np(dist):
    max_exact = NUM_BUCKETS // 2
    d = np.maximum(np.asarray(dist, np.int64), 0)
    ratio = np.log(np.maximum(d, 1).astype(np.float64) / max_exact) / math.log(MAX_DIST / max_exact)
    large = np.minimum(max_exact + (ratio * (NUM_BUCKETS - max_exact)).astype(np.int64), NUM_BUCKETS - 1)
    return np.where(d < max_exact, d, large).astype(np.int32)


def _silu(x):
    return x * jax.nn.sigmoid(x)


NORM_ROWS = 256


def _norm_matmul_kernel(x_ref, g_ref, w_ref, o_ref, xn_ref):
    @pl.when(pl.program_id(1) == 0)
    def _():
        rows = min(NORM_ROWS, x_ref.shape[0])

        def norm(c, carry):
            cur = pl.ds(pl.multiple_of(c * rows, rows), rows)
            x = x_ref[cur, :]
            ms = jnp.mean(x * x, axis=-1, keepdims=True)
            xn_ref[cur, :] = (x * lax.rsqrt(ms + EPS) * g_ref[...]).astype(BF16)
            return carry

        lax.fori_loop(0, x_ref.shape[0] // rows, norm, 0)

    o_ref[...] = jnp.dot(xn_ref[...], w_ref[...].astype(BF16), preferred_element_type=F32)


def _norm_matmul(x, g, w, layer, *, tm, tn, n_out=None):
    m, k = x.shape
    n = w.shape[2] if n_out is None else n_out
    assert n % tn == 0 and n <= w.shape[2]
    return pl.pallas_call(
        _norm_matmul_kernel,
        grid=(m // tm, n // tn),
        in_specs=[pl.BlockSpec((tm, k), lambda i, j: (i, 0), pipeline_mode=pl.Buffered(1)),
                  pl.BlockSpec((1, k), lambda i, j: (0, 0)),
                  pl.BlockSpec((None, k, tn), lambda i, j: (layer, 0, j))],
        out_specs=pl.BlockSpec((tm, tn), lambda i, j: (i, j)),
        out_shape=jax.ShapeDtypeStruct((m, n), F32),
        scratch_shapes=[pltpu.VMEM((tm, k), BF16)],
        compiler_params=_cparams("parallel", "arbitrary"),
        name="norm_matmul",
    )(x, g.reshape(1, k), w)


def _matmul_res_kernel(*refs, n_parts):
    a_refs = refs[:n_parts]
    w_ref, r_ref, o_ref = refs[n_parts:]
    acc = r_ref[...]
    k0 = 0
    for a_ref in a_refs:
        kp = a_ref.shape[1]
        acc = acc + jnp.dot(a_ref[...], w_ref[k0:k0 + kp, :].astype(BF16), preferred_element_type=F32)
        k0 += kp
    o_ref[...] = acc


def _matmul_res(parts, w, layer, res, *, tm, tn):
    m = res.shape[0]
    k, n = w.shape[1:]
    assert sum(p.shape[1] for p in parts) == k
    return pl.pallas_call(
        functools.partial(_matmul_res_kernel, n_parts=len(parts)),
        grid=(m // tm, n // tn),
        in_specs=[pl.BlockSpec((tm, p.shape[1]), lambda i, j: (i, 0), pipeline_mode=pl.Buffered(1))
                  for p in parts]
                 + [pl.BlockSpec((None, k, tn), lambda i, j: (layer, 0, j)),
                    pl.BlockSpec((tm, tn), lambda i, j: (i, j))],
        out_specs=pl.BlockSpec((tm, tn), lambda i, j: (i, j)),
        out_shape=jax.ShapeDtypeStruct((m, n), F32),
        compiler_params=_cparams("parallel", "arbitrary"),
        name="matmul_res",
    )(*parts, w, res)


def _rms_kernel(x_ref, g_ref, o_ref):
    x = x_ref[...]
    ms = jnp.mean(x * x, axis=-1, keepdims=True)
    o_ref[...] = x * lax.rsqrt(ms + EPS) * g_ref[...]


def _rms_norm(x, g, *, tm):
    m, k = x.shape
    return pl.pallas_call(
        _rms_kernel,
        grid=(m // tm,),
        in_specs=[pl.BlockSpec((tm, k), lambda i: (i, 0)), pl.BlockSpec((1, k), lambda i: (0, 0))],
        out_specs=pl.BlockSpec((tm, k), lambda i: (i, 0)),
        out_shape=jax.ShapeDtypeStruct((m, k), F32),
        compiler_params=_cparams("parallel"),
        name="rms_norm",
    )(x, g.reshape(1, k))


def _c_tab(rel_bias):
    return rel_bias[:, :C_HEADS]


def _c_rev_table(rel_bias, length, tq, max_dist):
    dist = length - np.arange(length + tq)
    valid = (dist >= 0) & (dist <= max_dist)
    tab = _c_tab(rel_bias)[_t5_bucket_np(np.clip(dist, 0, None))]
    tab = jnp.where(jnp.asarray(valid)[:, None], tab, NEG).astype(F32)
    return tab.T.reshape(C_HEADS, 1, length + tq)


def _toeplitz(vec, tq):
    w = vec.shape[1]
    return pltpu.roll(jnp.broadcast_to(vec, (tq, w)), 0, 1, stride=1, stride_axis=0)[:, tq:]


def _tile_attn_kernel(*refs, nh, rep, n_tiles, k_col, v_col, shared_kv, with_lse, lead_axis):
    q_ref = refs[0]
    if shared_kv:
        k_refs = v_refs = refs[1:1 + n_tiles]
        nxt = 1 + n_tiles
    else:
        k_refs = refs[1:1 + n_tiles]
        v_refs = refs[1 + n_tiles:1 + 2 * n_tiles]
        nxt = 1 + 2 * n_tiles
    b_ref = refs[nxt]
    o_ref = refs[nxt + 1]
    lse_ref = refs[nxt + 2] if with_lse else None
    tq = q_ref.shape[0]
    tk = k_refs[0].shape[0]
    lk = n_tiles * tk
    if n_tiles > 1:
        qi = pl.program_id(lead_axis)
        col = lax.broadcasted_iota(jnp.int32, (tq, lk), 1)
        pad_mask = jnp.where(col < (n_tiles - 1 - qi) * tk, NEG, 0.0).astype(F32)
    else:
        pad_mask = None
    if with_lse:
        lane = lax.broadcasted_iota(jnp.int32, (tq, LANES), 1)
        lse_t = jnp.zeros((tq, LANES), F32)
    for j in range(nh // rep):
        kc = k_col + j * HEAD_DIM
        vc = v_col + j * HEAD_DIM
        heads = range(j * rep, (j + 1) * rep)
        q = jnp.concatenate([(q_ref[:, h * HEAD_DIM:(h + 1) * HEAD_DIM] * SCALE).astype(BF16) for h in heads], axis=0)
        if n_tiles > 1:
            k = jnp.concatenate([r[:, kc:kc + HEAD_DIM] for r in k_refs], axis=0).astype(BF16)
            v = jnp.concatenate([r[:, vc:vc + HEAD_DIM] for r in v_refs], axis=0).astype(BF16)
        else:
            k = k_refs[0][:, kc:kc + HEAD_DIM].astype(BF16)
            v = v_refs[0][:, vc:vc + HEAD_DIM].astype(BF16)
        s = lax.dot_general(q, k, (((1,), (1,)), ((), ())), preferred_element_type=F32)
        s = s.reshape(rep, tq, lk) + b_ref[j * rep:(j + 1) * rep]
        if pad_mask is not None:
            s = s + pad_mask[None]
        s = s.reshape(rep * tq, lk)
        m = jnp.max(s, axis=-1, keepdims=True)
        e = jnp.exp(s - m)
        den = jnp.sum(e, axis=-1, keepdims=True)
        o = jnp.dot(e.astype(BF16), v, preferred_element_type=F32) / den
        for r, h in enumerate(heads):
            o_ref[:, h * HEAD_DIM:(h + 1) * HEAD_DIM] = o[r * tq:(r + 1) * tq]
        if with_lse:
            lse = m + jnp.log(den)
            for r, h in enumerate(heads):
                lse_t = jnp.where(lane == h, lse[r * tq:(r + 1) * tq], lse_t)
    if with_lse:
        lse_ref[...] = lse_t


A_TU = 128


def _a_rev_table(rel_bias):
    j = 2 * A_TU - np.arange(3 * A_TU)
    rows = []
    for g, (win, dil) in enumerate(A_GROUPS):
        valid = (j >= 0) & (j <= win // dil)
        tab = rel_bias[:, g * A_HPG:(g + 1) * A_HPG][_t5_bucket_np(np.clip(j, 0, None) * dil)]
        rows.append(jnp.where(jnp.asarray(valid)[:, None], tab, NEG).T)
    return jnp.stack(rows, axis=1).astype(F32)


def _a_prompt_kernel(q0_ref, q1_ref, q2_ref, k0_ref, k1_ref, k2_ref, v0_ref, v1_ref, v2_ref, ga_ref, rev_ref,
                     mix_ref, o_scr, l_scr, bias_scr):
    t = q0_ref.shape[0]
    tu = A_TU
    q_refs, k_refs, v_refs = (q0_ref, q1_ref, q2_ref), (k0_ref, k1_ref, k2_ref), (v0_ref, v1_ref, v2_ref)
    for g in range(A_NG):
        bias_scr[g] = _toeplitz(rev_ref[g:g + 1, :], tu)
    col = lax.broadcasted_iota(jnp.int32, (tu, 2 * tu), 1)
    for g, (win, dil) in enumerate(A_GROUPS):
        n_u = t // (dil * tu)
        q_ref, k_ref, v_ref = q_refs[g], k_refs[g], v_refs[g]

        def rows(r, u, dil=dil):
            start = r + u * (tu * dil)
            if dil == 1:
                return pl.ds(pl.multiple_of(start, tu), tu)
            return pl.ds(start, tu, stride=dil)

        def body(it, carry, g=g, n_u=n_u, q_ref=q_ref, k_ref=k_ref, v_ref=v_ref, rows=rows):
            r = it // n_u
            u = it % n_u
            cur = rows(r, u)
            q = (q_ref[cur, :] * SCALE).astype(BF16)
            prev = rows(r, jnp.maximum(u - 1, 0)) if n_u > 1 else cur
            k = jnp.concatenate([k_ref[prev, :], k_ref[cur, :]], axis=0).astype(BF16)
            v = jnp.concatenate([v_ref[prev, :], v_ref[cur, :]], axis=0).astype(BF16)
            bias = bias_scr[g] + jnp.where((col < tu) & (u == 0), NEG, 0.0)
            s = lax.dot_general(q, k, (((1,), (1,)), ((), ())), preferred_element_type=F32) + bias
            m = jnp.max(s, axis=-1, keepdims=True)
            e = jnp.exp(s - m)
            den = jnp.sum(e, axis=-1, keepdims=True)
            o_scr[g, cur, :] = jnp.dot(e.astype(BF16), v, preferred_element_type=F32) / den
            l_scr[g, cur, :] = jnp.broadcast_to(m + jnp.log(den), (tu, HEAD_DIM))
            return carry

        lax.fori_loop(0, t // tu, body, 0, unroll=8)

    def combine(c, carry):
        cur = pl.ds(pl.multiple_of(c * tu, tu), tu)
        ls = [l_scr[g, cur, :] for g in range(A_NG)]
        m = jnp.maximum(jnp.maximum(ls[0], ls[1]), ls[2])
        es = [jnp.exp(l - m) for l in ls]
        acc = es[0] * o_scr[0, cur, :] + es[1] * o_scr[1, cur, :] + es[2] * o_scr[2, cur, :]
        out = acc / (es[0] + es[1] + es[2])
        mix_ref[cur, :] = (out * _silu(ga_ref[cur, :])).astype(BF16)
        return carry

    lax.fori_loop(0, t // tu, combine, 0)


def _a_attn_prompt(z, rel_bias):
    b, t, n = z.shape
    for win, dil in A_GROUPS:
        assert win // dil == A_TU and t % (dil * A_TU) == 0
    rev = _a_rev_table(rel_bias)
    nq = A_QKV // HEAD_DIM
    col = lambda base: pl.BlockSpec((None, t, HEAD_DIM), lambda bi, h: (bi, 0, base + h))
    return pl.pallas_call(
        _a_prompt_kernel,
        grid=(b, A_HPG),
        in_specs=[col(g * A_HPG) for g in range(A_NG)]
                 + [col(nq + g * A_HPG) for g in range(A_NG)]
                 + [col(2 * nq + g * A_HPG) for g in range(A_NG)]
                 + [col(3 * nq), pl.BlockSpec((None, A_NG, 3 * A_TU), lambda bi, h: (h, 0, 0))],
        out_specs=pl.BlockSpec((None, t, HEAD_DIM), lambda bi, h: (bi, 0, h)),
        out_shape=jax.ShapeDtypeStruct((b, t, A_WIDTH), BF16),
        scratch_shapes=[pltpu.VMEM((A_NG, t, HEAD_DIM), F32), pltpu.VMEM((A_NG, t, HEAD_DIM), F32),
                        pltpu.VMEM((A_NG, A_TU, 2 * A_TU), F32)],
        compiler_params=_cparams("parallel", "parallel"),
        name="a_attn_prompt",
    )(*([z] * 10), rev)


def _a_bias_sample(rel_bias, g, hist_len, tq):
    win, dil = A_GROUPS[g]
    lk = hist_len + tq
    dist = hist_len + tq - 1 - np.arange(lk + tq - 1)
    valid = (dist >= 0) & (dist <= win) & (dist % dil == 0)
    tab = rel_bias[:, g * A_HPG:(g + 1) * A_HPG][_t5_bucket_np(np.clip(dist, 0, None))]
    rev = jnp.where(jnp.asarray(valid)[:, None], tab, NEG).T.astype(F32)
    return jnp.stack([rev[:, tq - 1 - t:tq - 1 - t + lk] for t in range(tq)], axis=1)


A_ROWS = 2 * A_HPG
A_CHUNK = 512


def _a_sample_kernel(*refs, aliased, layer):
    q_ref, kn_ref, vn_ref, old_ref, head_ref, bias_ref, bnew_ref = refs[:7]
    o_ref, lse_ref, new_ref, m_ref, l_ref, acc_ref = refs[7 + int(aliased):]
    if not aliased:
        for other in range(new_ref.shape[0]):
            if other != layer:
                new_ref[other] = jnp.zeros(new_ref.shape[1:], F32)
        new_ref = new_ref.at[layer]
    c = pl.program_id(1)
    last = pl.num_programs(1) - 1
    tq = q_ref.shape[0]
    n_tok = old_ref.shape[0] // A_ROWS
    shift = tq * A_ROWS

    @pl.when(c == 0)
    def _():
        m_ref[...] = jnp.full(m_ref.shape, NEG, F32)
        l_ref[...] = jnp.zeros(l_ref.shape, F32)
        acc_ref[...] = jnp.zeros(acc_ref.shape, F32)

    def update(h, q, k, v, bias):
        s = lax.dot_general(q, k, (((1,), (1,)), ((), ())), preferred_element_type=F32) + bias
        m_old = m_ref[h]
        m_new = jnp.maximum(m_old, jnp.max(s, axis=-1, keepdims=True))
        alpha = jnp.exp(m_old - m_new)
        e = jnp.exp(s - m_new)
        m_ref[h] = m_new
        l_ref[h] = alpha * l_ref[h] + jnp.sum(e, axis=-1, keepdims=True)
        acc_ref[h] = alpha * acc_ref[h] + jnp.dot(e.astype(BF16), v, preferred_element_type=F32)

    qs = [(q_ref[:, h * HEAD_DIM:(h + 1) * HEAD_DIM] * SCALE).astype(BF16) for h in range(A_HPG)]
    for h in range(A_HPG):
        k = old_ref[pl.ds(h, n_tok, stride=A_ROWS), :].astype(BF16)
        v = old_ref[pl.ds(A_HPG + h, n_tok, stride=A_ROWS), :].astype(BF16)
        update(h, qs[h], k, v, bias_ref[h])

    new_ref[0:n_tok * A_ROWS - shift, :] = old_ref[shift:, :]

    @pl.when(c < last)
    def _():
        new_ref[n_tok * A_ROWS - shift:, :] = head_ref[...]

    @pl.when(c == last)
    def _():
        lane = lax.broadcasted_iota(jnp.int32, (tq, LANES), 1)
        lse_t = jnp.zeros((tq, LANES), F32)
        for h in range(A_HPG):
            cs = slice(h * HEAD_DIM, (h + 1) * HEAD_DIM)
            update(h, qs[h], kn_ref[:, cs].astype(BF16), vn_ref[:, cs].astype(BF16), bnew_ref[h])
            o_ref[:, cs] = acc_ref[h] / l_ref[h]
            lse_t = jnp.where(lane == h, m_ref[h] + jnp.log(l_ref[h]), lse_t)
            base = n_tok * A_ROWS - shift
            new_ref[pl.ds(base + h, tq, stride=A_ROWS), :] = kn_ref[:, cs]
            new_ref[pl.ds(base + A_HPG + h, tq, stride=A_ROWS), :] = vn_ref[:, cs]
        lse_ref[...] = lse_t


def _a_attn_sample(z, cache, layer, new_cache, rel_bias, g):
    b, tq, n = z.shape
    n_layers, _, w = cache.shape[:3]
    assert cache.shape[3:] == (2, A_HPG, HEAD_DIM) and w % LANES == 0 and tq * A_ROWS == LANES
    ch = min(w, A_CHUNK)
    n_ch = w // ch
    old = cache.reshape(n_layers, b, w * A_ROWS, HEAD_DIM)
    bias_full = _a_bias_sample(rel_bias, g, w, tq)
    bias = jnp.stack([bias_full[:, :, i * ch:(i + 1) * ch] for i in range(n_ch)])
    bnew = bias_full[:, :, w:]
    nq = A_QKV // A_WIDTH
    zcol = lambda cb: pl.BlockSpec((None, tq, A_WIDTH), lambda bi, c: (bi, 0, cb))
    heads_per_chunk = ch * A_ROWS // LANES
    in_specs = [zcol(g), zcol(nq + g), zcol(2 * nq + g),
                pl.BlockSpec((None, None, ch * A_ROWS, HEAD_DIM), lambda bi, c: (layer, bi, c, 0)),
                pl.BlockSpec((None, None, LANES, HEAD_DIM),
                             lambda bi, c: (layer, bi, jnp.minimum(c + 1, n_ch - 1) * heads_per_chunk, 0)),
                pl.BlockSpec((None, A_HPG, tq, ch), lambda bi, c: (c, 0, 0, 0)),
                pl.BlockSpec((A_HPG, tq, tq), lambda bi, c: (0, 0, 0))]
    args = [z, z, z, old, old, bias, bnew]
    aliases = {}
    if new_cache is not None:
        in_specs.append(pl.BlockSpec(memory_space=pl.ANY))
        args.append(new_cache)
        aliases = {len(args) - 1: 2}
        new_spec = pl.BlockSpec((None, None, ch * A_ROWS, HEAD_DIM), lambda bi, c: (layer, bi, c, 0))
    else:
        new_spec = pl.BlockSpec((n_layers, None, ch * A_ROWS, HEAD_DIM), lambda bi, c: (0, bi, c, 0))
    o, lse, new = pl.pallas_call(
        functools.partial(_a_sample_kernel, aliased=new_cache is not None, layer=layer),
        grid=(b, n_ch),
        in_specs=in_specs,
        out_specs=[pl.BlockSpec((None, tq, A_WIDTH), lambda bi, c: (bi, 0, 0)),
                   pl.BlockSpec((None, tq, LANES), lambda bi, c: (bi, 0, 0)),
                   new_spec],
        out_shape=[jax.ShapeDtypeStruct((b, tq, A_WIDTH), F32),
                   jax.ShapeDtypeStruct((b, tq, LANES), F32),
                   jax.ShapeDtypeStruct(old.shape, F32)],
        scratch_shapes=[pltpu.VMEM((A_HPG, tq, 1), F32), pltpu.VMEM((A_HPG, tq, 1), F32),
                        pltpu.VMEM((A_HPG, tq, HEAD_DIM), F32)],
        input_output_aliases=aliases,
        compiler_params=_cparams("parallel", "arbitrary"),
        name=f"a_attn_sample_g{g}",
    )(*args)
    return o, lse, new


def _conv_kernel(a_ref, gt_ref, gb_ref, hist_ref, cw_ref, cb_ref, lg_ref, lb_ref, pw_ref, pb_ref,
                 o_ref, nc_ref, ubuf, ybuf, pwb):
    ti = pl.program_id(1)
    nt = pl.num_programs(1)
    tt = a_ref.shape[0]
    nhist = CONV_W - 1
    pad = CONV_HALO - nhist

    @pl.when(ti == 0)
    def _():
        pwb[...] = pw_ref[...].astype(BF16)
        ubuf[0:pad, :] = jnp.zeros((pad, B_WIDTH), F32)
        ubuf[pad:CONV_HALO, :] = hist_ref[...]

    ubuf[CONV_HALO:CONV_HALO + tt, :] = a_ref[...] * jax.nn.sigmoid(gt_ref[...])
    for c in range(B_WIDTH // LANES):
        cs = slice(c * LANES, (c + 1) * LANES)
        acc = jnp.zeros((tt, LANES), F32) + cb_ref[:, cs]
        for r in range(SUBLANES):
            taps = [a for a in range(CONV_HALO // SUBLANES + 1) if 0 <= SUBLANES * a + r - pad < CONV_W]
            span = tt if r == 0 else tt + SUBLANES
            z = None
            for a in taps:
                k = SUBLANES * a + r - pad
                term = ubuf[SUBLANES * a:SUBLANES * a + span, cs] * cw_ref[k:k + 1, cs]
                z = term if z is None else z + term
            acc = acc + (z if r == 0 else z[r:r + tt])
        ybuf[:, cs] = acc
    y = ybuf[...]
    mu = jnp.mean(y, axis=-1, keepdims=True)
    yc = y - mu
    var = jnp.mean(yc * yc, axis=-1, keepdims=True)
    yn = yc * lax.rsqrt(var + EPS) * lg_ref[...] + lb_ref[...]
    act = _silu(yn).astype(BF16)
    ob = jnp.dot(act, pwb[...], preferred_element_type=F32) + pb_ref[...]
    o_ref[...] = (ob * _silu(gb_ref[...])).astype(BF16)

    @pl.when(ti == nt - 1)
    def _():
        nc_ref[...] = ubuf[tt + pad:tt + CONV_HALO, :]

    @pl.when(ti < nt - 1)
    def _():
        ubuf[0:CONV_HALO, :] = ubuf[tt:tt + CONV_HALO, :]


def _conv_module(z, hist, layer, conv_w, conv_b, ln_g, ln_b, pw_w, pw_b, *, tt):
    b, t, n = z.shape
    assert t % tt == 0 and (tt >= CONV_HALO or t == tt)
    glu0 = (3 * A_QKV + A_WIDTH) // B_WIDTH
    row = lambda a: a.reshape(a.shape[0], 1, B_WIDTH)
    blk = (None, tt, B_WIDTH)
    lyr = lambda shape: pl.BlockSpec((None,) + shape, lambda bi, ti: (layer,) + (0,) * len(shape))
    return pl.pallas_call(
        _conv_kernel,
        grid=(b, t // tt),
        in_specs=[pl.BlockSpec(blk, lambda bi, ti: (bi, ti, glu0)),
                  pl.BlockSpec(blk, lambda bi, ti: (bi, ti, glu0 + 1)),
                  pl.BlockSpec(blk, lambda bi, ti: (bi, ti, glu0 + 2)),
                  pl.BlockSpec((None, CONV_W - 1, B_WIDTH), lambda bi, ti: (bi, 0, 0)),
                  lyr((CONV_W, B_WIDTH)), lyr((1, B_WIDTH)), lyr((1, B_WIDTH)), lyr((1, B_WIDTH)),
                  lyr((B_WIDTH, B_WIDTH)), lyr((1, B_WIDTH))],
        out_specs=[pl.BlockSpec(blk, lambda bi, ti: (bi, ti, 0)),
                   pl.BlockSpec((None, CONV_W - 1, B_WIDTH), lambda bi, ti: (bi, 0, 0))],
        out_shape=[jax.ShapeDtypeStruct((b, t, B_WIDTH), BF16),
                   jax.ShapeDtypeStruct((b, CONV_W - 1, B_WIDTH), F32)],
        scratch_shapes=[pltpu.VMEM((CONV_HALO + tt, B_WIDTH), F32),
                        pltpu.VMEM((tt, B_WIDTH), F32),
                        pltpu.VMEM((B_WIDTH, B_WIDTH), BF16)],
        compiler_params=_cparams("parallel", "arbitrary"),
        name="conv_module",
    )(z, z, z, hist, conv_w, row(conv_b), row(ln_g), row(ln_b), pw_w, row(pw_b))


def _even_mix_kernel(o0_ref, o1_ref, o2_ref, l0_ref, l1_ref, l2_ref, ga_ref, mix_ref):
    ls = [l0_ref[...], l1_ref[...], l2_ref[...]]
    m = jnp.maximum(jnp.maximum(ls[0], ls[1]), ls[2])
    es = [jnp.exp(l - m) for l in ls]
    inv = 1.0 / (es[0] + es[1] + es[2])
    ws = [e * inv for e in es]
    o_refs = (o0_ref, o1_ref, o2_ref)
    for h in range(A_HPG):
        cs = slice(h * HEAD_DIM, (h + 1) * HEAD_DIM)
        acc = ws[0][:, h:h + 1] * o_refs[0][:, cs]
        for g in range(1, A_NG):
            acc = acc + ws[g][:, h:h + 1] * o_refs[g][:, cs]
        mix_ref[:, cs] = (acc * _silu(ga_ref[:, cs])).astype(BF16)


def _even_mix(oas, lses, z2, *, tm):
    m = z2.shape[0]
    ga_blk = 3 * A_QKV // A_WIDTH
    wide = lambda c: pl.BlockSpec((tm, A_WIDTH), lambda i: (i, c))
    narrow = pl.BlockSpec((tm, LANES), lambda i: (i, 0))
    return pl.pallas_call(
        _even_mix_kernel,
        grid=(m // tm,),
        in_specs=[wide(0), wide(0), wide(0), narrow, narrow, narrow, wide(ga_blk)],
        out_specs=wide(0),
        out_shape=jax.ShapeDtypeStruct((m, A_WIDTH), BF16),
        compiler_params=_cparams("parallel"),
        name="even_mix",
    )(*oas, *lses, z2)


def _even_layer(x, a_caches, a_new, conv_hist, rel_bias, layer, norm_g, w_in, conv_w, conv_b, ln_g, ln_b, pw_w,
                pw_b, w_out, *, tm, tn_in, tn_out, tt):
    b, t, d = x.shape
    m = b * t
    x2 = x.reshape(m, d)
    z2 = _norm_matmul(x2, norm_g[layer], w_in, layer, tm=tm, tn=tn_in)
    z = z2.reshape(b, t, EVEN_IN)
    oas, lses, new_hists = [], [], []
    for g, (win, _) in enumerate(A_GROUPS):
        if a_caches is None:
            kcol = A_QKV + g * A_WIDTH
            vcol = 2 * A_QKV + g * A_WIDTH
            new_kv = jnp.stack([z[:, :, kcol:kcol + A_WIDTH], z[:, :, vcol:vcol + A_WIDTH]], axis=2)
            new_hists.append(new_kv[:, t - min(win, t):].reshape(b, min(win, t), 2, A_HPG, HEAD_DIM))
        else:
            assert a_caches[g].shape[2] == win
            o, lse, new = _a_attn_sample(z, a_caches[g], layer, None if a_new is None else a_new[g], rel_bias, g)
            new_hists.append(new)
            oas.append(o.reshape(m, A_WIDTH))
            lses.append(lse.reshape(m, LANES))
    if a_caches is None:
        mix_a = _a_attn_prompt(z, rel_bias).reshape(m, A_WIDTH)
    else:
        mix_a = _even_mix(oas, lses, z2, tm=min(tm, 256))
    mix_b, new_conv = _conv_module(z, conv_hist, layer, conv_w, conv_b, ln_g, ln_b, pw_w, pw_b, tt=tt)
    y = _matmul_res([mix_a, mix_b.reshape(m, B_WIDTH)], w_out, layer, x2, tm=tm, tn=tn_out)
    return y.reshape(b, t, d), new_hists, new_conv


ODD_KV0 = C_WIDTH
ODD_GL0 = C_WIDTH + 6 * C_KV
ODD_GP0 = ODD_GL0 + 3 * C_HEADS
ODD_Z = ODD_GL0 + 512
CMP_HALF = CMP_BLK // 2
CMP_K = CMP_HALF * HEAD_DIM
PAGES_PER_STEP = 16
CACHE_KINDS = 2 * C_KV_HEADS
CACHE_ROW_KINDS = 2 * CACHE_KINDS


def _cmp_partials_kernel(*refs, n_src, n_prefetch, interleaved):
    refs = refs[n_prefetch:]
    src = refs[:n_src]
    pos_ref, w1_ref, a_ref, b_ref, rows_ref, w1b_ref = refs[n_src:n_src + 6]
    n = rows_ref.shape[1] // CMP_HALF

    if n_prefetch:
        @pl.when(pl.program_id(1) == 0)
        def _():
            w1b_ref[...] = w1_ref[...].astype(BF16)
    else:
        w1b_ref[...] = w1_ref[...].astype(BF16)

    r0 = 0
    for r in src:
        nr = r.shape[0] // CACHE_ROW_KINDS if interleaved else r.shape[0]
        for c in range(CACHE_KINDS):
            if interleaved:
                rows_ref[c, r0:r0 + nr, :] = r[pl.ds(c, nr, stride=CACHE_ROW_KINDS), :]
            else:
                rows_ref[c, r0:r0 + nr, :] = r[:, c * HEAD_DIM:(c + 1) * HEAD_DIM]
        r0 += nr
    for kv in range(2):
        x = jnp.concatenate(
            [jnp.concatenate([rows_ref[kv * C_KV_HEADS + g, pl.ds(l, n, stride=CMP_HALF), :]
                              for l in range(CMP_HALF)], axis=1) for g in range(C_KV_HEADS)], axis=0)
        for half, out in ((0, a_ref), (1, b_ref)):
            y = jnp.dot((x + pos_ref[kv, half]).astype(BF16), w1b_ref[kv, half], preferred_element_type=F32)
            for g in range(C_KV_HEADS):
                out[kv * C_KV_HEADS + g] = y[g * n:(g + 1) * n]


def _cmp_weights(cmp_pos, cmp_w1):
    n = cmp_pos.shape[0]
    return cmp_pos.reshape(n, 2, 2, 1, CMP_K), cmp_w1.reshape(n, 2, 2, CMP_K, CMP_HID)


def _cmp_partials_prompt(z, layer, cmp_pos, cmp_w1):
    b, t, _ = z.shape
    nch = t // CMP_STRIDE
    pos, w1 = _cmp_weights(cmp_pos, cmp_w1)
    kern = functools.partial(_cmp_partials_kernel, n_src=1, n_prefetch=0, interleaved=False)
    out = jax.ShapeDtypeStruct((b, 4, nch, CMP_HID), F32)
    ospec = pl.BlockSpec((None, 4, nch, CMP_HID), lambda bi: (bi, 0, 0, 0))
    return pl.pallas_call(
        kern,
        grid=(b,),
        in_specs=[pl.BlockSpec((None, t, 2 * C_KV), lambda bi: (bi, 0, ODD_KV0 // (2 * C_KV))),
                  pl.BlockSpec((None,) + pos.shape[1:], lambda bi: (layer, 0, 0, 0, 0)),
                  pl.BlockSpec((None,) + w1.shape[1:], lambda bi: (layer, 0, 0, 0, 0))],
        out_specs=[ospec, ospec],
        out_shape=[out, out],
        scratch_shapes=[pltpu.VMEM((2 * C_KV_HEADS, t, HEAD_DIM), F32), pltpu.VMEM(w1.shape[1:], BF16)],
        compiler_params=_cparams("parallel"),
        name="cmp_partials_prompt",
    )(z, pos, w1)


def _cmp_partials_paged(cache, layer, page_table, cmp_pos, cmp_w1):
    b, n_pages = page_table.shape
    page = cache.shape[2] // CACHE_ROW_KINDS
    pps = PAGES_PER_STEP
    assert n_pages % pps == 0 and page % CMP_STRIDE == 0
    nch_step = pps * page // CMP_STRIDE
    nch = n_pages * page // CMP_STRIDE
    pos, w1 = _cmp_weights(cmp_pos, cmp_w1)
    kern = functools.partial(_cmp_partials_kernel, n_src=pps, n_prefetch=1, interleaved=True)
    out = jax.ShapeDtypeStruct((b, 4, nch, CMP_HID), F32)
    ospec = pl.BlockSpec((None, 4, nch_step, CMP_HID), lambda bi, pg, pt: (bi, 0, pg, 0))

    def page_spec(j):
        return pl.BlockSpec((None, None, page * CACHE_ROW_KINDS, HEAD_DIM),
                            lambda bi, pg, pt: (layer, pt[bi, pg * pps + j], 0, 0))

    grid_spec = pltpu.PrefetchScalarGridSpec(
        num_scalar_prefetch=1,
        grid=(b, n_pages // pps),
        in_specs=[page_spec(j) for j in range(pps)] + [
            pl.BlockSpec((None,) + pos.shape[1:], lambda bi, pg, pt: (layer, 0, 0, 0, 0)),
            pl.BlockSpec((None,) + w1.shape[1:], lambda bi, pg, pt: (layer, 0, 0, 0, 0))],
        out_specs=[ospec, ospec],
        scratch_shapes=[pltpu.VMEM((2 * C_KV_HEADS, pps * page, HEAD_DIM), F32), pltpu.VMEM(w1.shape[1:], BF16)])
    return pl.pallas_call(
        kern, grid_spec=grid_spec, out_shape=[out, out],
        compiler_params=_cparams("parallel", "arbitrary"),
        name="cmp_partials_paged",
    )(page_table, *([cache] * pps), pos, w1)


def _overlap_np(n_ch, n_sel, ns_pad):
    ci = np.arange(n_ch)[:, None]
    si = np.arange(ns_pad)[None, :]
    ov = (ci * CMP_STRIDE < (si + 1) * SEL_BLK) & (ci * CMP_STRIDE + CMP_BLK > si * SEL_BLK)
    ov &= (ci < n_ch - 1) & (si < n_sel)
    return ov.astype(np.float32)


def _cmp_select_kernel(q_ref, ak_ref, bk_ref, av_ref, bv_ref, b1_ref, w2_ref, ov_ref, oc_ref, sel_ref,
                       *, qpos0, n_sel):
    qi = pl.program_id(2)
    tq = q_ref.shape[0]
    nch = ak_ref.shape[0]
    ns_pad = ov_ref.shape[1]

    def finish(a_ref, b_ref, kv):
        hid = a_ref[...] + pltpu.roll(b_ref[...], nch - 1, 0) + b1_ref[kv:kv + 1, :]
        return jnp.dot(_silu(hid).astype(BF16), w2_ref[kv].astype(BF16),
                       preferred_element_type=F32).astype(BF16)

    k_cmp = finish(ak_ref, bk_ref, 0)
    v_cmp = finish(av_ref, bv_ref, 1)
    if tq % LANES == 0:
        cols = C_HPG * tq
        assert tq & (tq - 1) == 0
        qpos_t = qpos0 + qi * tq + (lax.broadcasted_iota(jnp.int32, (nch, cols), 1) & (tq - 1))
        end_t = lax.broadcasted_iota(jnp.int32, (nch, cols), 0) * CMP_STRIDE + (CMP_BLK - 1)
        ok = (end_t <= qpos_t) & (end_t < (nch - 1) * CMP_STRIDE + CMP_BLK - 1)
        q = jnp.concatenate([(q_ref[:, h * HEAD_DIM:(h + 1) * HEAD_DIM] * SCALE).astype(BF16)
                             for h in range(C_HPG)], axis=0)
        s = lax.dot_general(k_cmp, q, (((1,), (1,)), ((), ())), preferred_element_type=F32)
        s = jnp.where(ok, s, NEG)
        m = jnp.max(s, axis=0, keepdims=True)
        e = jnp.where(ok, jnp.exp(s - m), 0.0)
        den = jnp.sum(e, axis=0, keepdims=True)
        p = e / jnp.where(den > 0.0, den, 1.0)
        psum = p[:, 0:tq]
        for h in range(1, C_HPG):
            psum = psum + p[:, h * tq:(h + 1) * tq]
        v_t = v_cmp.astype(F32).T.astype(BF16)
        oc = jnp.dot(v_t, p.astype(BF16), preferred_element_type=F32)
        for h in range(C_HPG):
            oc_ref[:, h * HEAD_DIM:(h + 1) * HEAD_DIM] = oc[:, h * tq:(h + 1) * tq].T
        imp = jnp.dot(ov_ref[...].T, psum, preferred_element_type=F32, precision=lax.Precision.HIGHEST)
        blk = lax.broadcasted_iota(jnp.int32, (ns_pad, tq), 0)
        qblk = (qpos0 + qi * tq + lax.broadcasted_iota(jnp.int32, (ns_pad, tq), 1)) // SEL_BLK
        forced = (blk == 0) | (blk == qblk) | (blk == qblk - 1)
        allowed = (blk <= qblk) & (blk < n_sel)
        score = jnp.where(allowed, jnp.where(forced, -NEG, imp), -1.0)
        rank = jnp.zeros((ns_pad, tq), jnp.int32)
        for j in range(n_sel):
            rj = score[j:j + 1, :]
            before = (rj > score) | ((rj == score) & (j < blk))
            rank = rank + before.astype(jnp.int32)
        sel_ref[...] = (allowed & (rank < SEL_N)).astype(F32).T
        return
    qpos = qpos0 + qi * tq + lax.broadcasted_iota(jnp.int32, (tq, nch), 0)
    cmp_end = lax.broadcasted_iota(jnp.int32, (tq, nch), 1) * CMP_STRIDE + (CMP_BLK - 1)
    ok = (cmp_end <= qpos) & (cmp_end < (nch - 1) * CMP_STRIDE + CMP_BLK - 1)
    q = jnp.concatenate([(q_ref[:, h * HEAD_DIM:(h + 1) * HEAD_DIM] * SCALE).astype(BF16) for h in range(C_HPG)],
                        axis=0)
    s = lax.dot_general(q, k_cmp, (((1,), (1,)), ((), ())), preferred_element_type=F32)
    s = jnp.where(ok[None], s.reshape(C_HPG, tq, nch), NEG)
    m = jnp.max(s, axis=-1, keepdims=True)
    e = jnp.where(ok[None], jnp.exp(s - m), 0.0)
    den = jnp.sum(e, axis=-1, keepdims=True)
    p = e / jnp.where(den > 0.0, den, 1.0)
    psum = jnp.sum(p, axis=0)
    oc = jnp.dot(p.reshape(C_HPG * tq, nch).astype(BF16), v_cmp, preferred_element_type=F32)
    for h in range(C_HPG):
        oc_ref[:, h * HEAD_DIM:(h + 1) * HEAD_DIM] = oc[h * tq:(h + 1) * tq]
    imp = jnp.dot(psum, ov_ref[...], preferred_element_type=F32, precision=lax.Precision.HIGHEST)
    blk = lax.broadcasted_iota(jnp.int32, (tq, ns_pad), 1)
    qblk = (qpos0 + qi * tq + lax.broadcasted_iota(jnp.int32, (tq, ns_pad), 0)) // SEL_BLK
    forced = (blk == 0) | (blk == qblk) | (blk == qblk - 1)
    allowed = (blk <= qblk) & (blk < n_sel)
    score = jnp.where(allowed, jnp.where(forced, -NEG, imp), -1.0)
    rank = jnp.zeros((tq, ns_pad), jnp.int32)
    for j in range(n_sel):
        cj = score[:, j:j + 1]
        before = (cj > score) | ((cj == score) & (j < blk))
        rank = rank + before.astype(jnp.int32)
    sel_ref[...] = (allowed & (rank < SEL_N)).astype(F32)


def _cmp_select(z, a, bm, layer, cmp_b1, cmp_w2, *, tq, qpos0, n_sel):
    b, t, _ = z.shape
    nch = a.shape[2]
    ns_pad = -(-n_sel // LANES) * LANES
    ov = jnp.asarray(_overlap_np(nch, n_sel, ns_pad))
    kern = functools.partial(_cmp_select_kernel, qpos0=qpos0, n_sel=n_sel)
    part = lambda kv: pl.BlockSpec((None, None, nch, CMP_HID), lambda bi, g, qi: (bi, kv * C_KV_HEADS + g, 0, 0))
    return pl.pallas_call(
        kern,
        grid=(b, C_KV_HEADS, t // tq),
        in_specs=[pl.BlockSpec((None, tq, C_HPG * HEAD_DIM), lambda bi, g, qi: (bi, qi, g)),
                  part(0), part(0), part(1), part(1),
                  pl.BlockSpec((None, 2, CMP_HID), lambda bi, g, qi: (layer, 0, 0)),
                  pl.BlockSpec((None, 2, CMP_HID, HEAD_DIM), lambda bi, g, qi: (layer, 0, 0, 0)),
                  pl.BlockSpec((nch, ns_pad), lambda bi, g, qi: (0, 0))],
        out_specs=[pl.BlockSpec((None, tq, C_HPG * HEAD_DIM), lambda bi, g, qi: (bi, qi, g)),
                   pl.BlockSpec((None, None, tq, ns_pad), lambda bi, g, qi: (bi, g, qi, 0))],
        out_shape=[jax.ShapeDtypeStruct((b, t, C_WIDTH), F32),
                   jax.ShapeDtypeStruct((b, C_KV_HEADS, t, ns_pad), F32)],
        compiler_params=_cparams("parallel", "parallel", "arbitrary"),
        name="cmp_select",
    )(z, a, bm, a, bm, cmp_b1, cmp_w2, ov)


def _sel_prompt_kernel(q_ref, k_ref, v_ref, rev_ref, sel_ref, o_ref,
                       bias_ref, msk_ref, qb_ref, m_ref, l_ref, acc_ref):
    qi = pl.program_id(2)
    tq = q_ref.shape[0]
    nt = bias_ref.shape[0] - 1
    tk = msk_ref.shape[1]
    ratio = tk // tq
    ns_pad = sel_ref.shape[1]

    @pl.when(qi == 0)
    def _():
        bias_ref[0] = jnp.full(bias_ref.shape[1:], NEG, F32)
        for h in range(C_HPG):
            for delta in range(nt):
                off = (nt - 1 - delta) * tq
                bias_ref[delta + 1, :, h * tq:(h + 1) * tq] = (
                    _toeplitz(rev_ref[h, :, off:off + 2 * tq], tq).T * LOG2E)

    sel_t = sel_ref[...].T.astype(BF16)
    erow = lax.broadcasted_iota(jnp.int32, (tk, ns_pad), 0) // SEL_BLK
    ecol = lax.broadcasted_iota(jnp.int32, (tk, ns_pad), 1)
    for t in range(msk_ref.shape[0]):
        expand = (ecol == erow + t * (tk // SEL_BLK)).astype(BF16)
        hit = jnp.dot(expand, sel_t, preferred_element_type=F32)
        msk_ref[t] = jnp.where(hit > 0.5, 0.0, NEG)
    for h in range(C_HPG):
        qb_ref[h * tq:(h + 1) * tq, :] = (q_ref[:, h * HEAD_DIM:(h + 1) * HEAD_DIM] * (SCALE * LOG2E)).astype(BF16)
    m_ref[...] = jnp.full(m_ref.shape, NEG, F32)
    l_ref[...] = jnp.zeros(l_ref.shape, F32)
    acc_ref[...] = jnp.zeros(acc_ref.shape, F32)

    def body(kj, carry):
        keys = pl.ds(pl.multiple_of(kj * tk, tk), tk)
        k = k_ref[keys, :].astype(BF16)
        v_t = v_ref[keys, :].T.astype(BF16)
        d0 = qi - ratio * kj
        parts = [bias_ref[d0 + 1 - i] for i in range(ratio)]
        bias = parts[0] if ratio == 1 else jnp.concatenate(parts, axis=0)
        s = lax.dot_general(k, qb_ref[...], (((1,), (1,)), ((), ())), preferred_element_type=F32)
        s = s + bias + jnp.concatenate([msk_ref[kj]] * C_HPG, axis=1)
        m_old = m_ref[...]
        m_new = jnp.maximum(m_old, jnp.max(s, axis=0, keepdims=True))
        alpha = jnp.exp2(m_old - m_new)
        e = jnp.exp2(s - m_new)
        m_ref[...] = m_new
        l_ref[...] = alpha * l_ref[...] + jnp.sum(e, axis=0, keepdims=True)
        acc_ref[...] = alpha * acc_ref[...] + jnp.dot(v_t, e.astype(BF16), preferred_element_type=F32)
        return carry

    lax.fori_loop(0, (qi + ratio) // ratio, body, 0)
    o = acc_ref[...] / l_ref[...]
    for h in range(C_HPG):
        o_ref[:, h * HEAD_DIM:(h + 1) * HEAD_DIM] = o[:, h * tq:(h + 1) * tq].T


SEL_TQ = 128
SEL_KEY_RATIO = 2


def _sel_attn_prompt(z, sel, rel_bias):
    b, t, _ = z.shape
    tq = SEL_TQ
    nt = t // tq
    ratio = SEL_KEY_RATIO
    assert nt % ratio == 0
    ns_pad = sel.shape[3]
    rev = _c_rev_table(rel_bias, t, tq, t)
    kcol = (ODD_KV0 + 2 * C_KV) // HEAD_DIM
    vcol = (ODD_KV0 + 3 * C_KV) // HEAD_DIM
    return pl.pallas_call(
        _sel_prompt_kernel,
        grid=(b, C_KV_HEADS, t // tq),
        in_specs=[pl.BlockSpec((None, tq, C_HPG * HEAD_DIM), lambda bi, g, qi: (bi, qi, g)),
                  pl.BlockSpec((None, t, HEAD_DIM), lambda bi, g, qi: (bi, 0, kcol + g)),
                  pl.BlockSpec((None, t, HEAD_DIM), lambda bi, g, qi: (bi, 0, vcol + g)),
                  pl.BlockSpec((C_HPG, 1, t + tq), lambda bi, g, qi: (g, 0, 0)),
                  pl.BlockSpec((None, None, tq, ns_pad), lambda bi, g, qi: (bi, g, qi, 0))],
        out_specs=pl.BlockSpec((None, tq, C_HPG * HEAD_DIM), lambda bi, g, qi: (bi, qi, g)),
        out_shape=jax.ShapeDtypeStruct((b, t, C_WIDTH), F32),
        scratch_shapes=[pltpu.VMEM((nt + 1, tq, C_HPG * tq), F32),
                        pltpu.VMEM((nt // ratio, ratio * tq, tq), F32),
                        pltpu.VMEM((C_HPG * tq, HEAD_DIM), BF16),
                        pltpu.VMEM((1, C_HPG * tq), F32),
                        pltpu.VMEM((1, C_HPG * tq), F32),
                        pltpu.VMEM((HEAD_DIM, C_HPG * tq), F32)],
        compiler_params=_cparams("parallel", "parallel", "arbitrary"),
        name="sel_attn_prompt",
    )(z, z, z, rev, sel)


def _sel_paged_kernel(pt_ref, *refs, n_src, past):
    del pt_ref
    pages = refs[:n_src]
    q_ref, new_ref, sel_ref, bias_ref, bnew_ref, o_ref, m_ref, l_ref, acc_ref = refs[n_src:]
    pg = pl.program_id(1)
    tq = q_ref.shape[0]
    rows = C_HPG * tq
    page = pages[0].shape[0] // CACHE_ROW_KINDS
    keys = n_src * page
    ns_pad = sel_ref.shape[2]

    def page_rows(r, kind):
        return r[pl.ds(CACHE_KINDS + kind, page, stride=CACHE_ROW_KINDS), :]

    def q_rows(g):
        return jnp.concatenate(
            [(q_ref[:, (g * C_HPG + h) * HEAD_DIM:(g * C_HPG + h + 1) * HEAD_DIM] * SCALE).astype(BF16)
             for h in range(C_HPG)], axis=0)

    @pl.when(pg == 0)
    def _():
        for g in range(C_KV_HEADS):
            k = new_ref[:, g * HEAD_DIM:(g + 1) * HEAD_DIM].astype(BF16)
            v = new_ref[:, C_KV + g * HEAD_DIM:C_KV + (g + 1) * HEAD_DIM].astype(BF16)
            s = lax.dot_general(q_rows(g), k, (((1,), (1,)), ((), ())), preferred_element_type=F32)
            s = s + bnew_ref[g * C_HPG:(g + 1) * C_HPG].reshape(rows, tq)
            m = jnp.max(s, axis=-1, keepdims=True)
            e = jnp.exp(s - m)
            m_ref[g] = m
            l_ref[g] = jnp.sum(e, axis=-1, keepdims=True)
            acc_ref[g] = jnp.dot(e.astype(BF16), v, preferred_element_type=F32)

    srow = lax.broadcasted_iota(jnp.int32, (ns_pad, keys), 0)
    scol = lax.broadcasted_iota(jnp.int32, (ns_pad, keys), 1) // SEL_BLK + pg * (keys // SEL_BLK)
    expand = (srow == scol).astype(BF16)
    for g in range(C_KV_HEADS):
        k = jnp.concatenate([page_rows(r, g) for r in pages], axis=0).astype(BF16)
        v = jnp.concatenate([page_rows(r, C_KV_HEADS + g) for r in pages], axis=0).astype(BF16)
        hit = jnp.dot(sel_ref[g].astype(BF16), expand, preferred_element_type=F32)
        mask = jnp.where(hit > 0.5, 0.0, NEG)
        s = lax.dot_general(q_rows(g), k, (((1,), (1,)), ((), ())), preferred_element_type=F32)
        s = (s.reshape(C_HPG, tq, keys) + bias_ref[g * C_HPG:(g + 1) * C_HPG] + mask[None]).reshape(rows, keys)
        m_old = m_ref[g]
        m_new = jnp.maximum(m_old, jnp.max(s, axis=-1, keepdims=True))
        alpha = jnp.exp(m_old - m_new)
        e = jnp.exp(s - m_new)
        m_ref[g] = m_new
        l_ref[g] = alpha * l_ref[g] + jnp.sum(e, axis=-1, keepdims=True)
        acc_ref[g] = alpha * acc_ref[g] + jnp.dot(e.astype(BF16), v, preferred_element_type=F32)

    @pl.when(pg == pl.num_programs(1) - 1)
    def _():
        for g in range(C_KV_HEADS):
            o = acc_ref[g] / l_ref[g]
            for h in range(C_HPG):
                o_ref[:, (g * C_HPG + h) * HEAD_DIM:(g * C_HPG + h + 1) * HEAD_DIM] = o[h * tq:(h + 1) * tq]


def _sel_attn_paged(z, sel, cache, layer, page_table, rel_bias):
    b, tq, _ = z.shape
    n_pages = page_table.shape[1]
    page = cache.shape[2] // CACHE_ROW_KINDS
    past = n_pages * page
    pps = PAGES_PER_STEP
    keys = pps * page
    ns_pad = sel.shape[3]
    tab = _c_tab(rel_bias)
    rev = tab[_t5_bucket_np(past + tq - 1 - np.arange(past + tq - 1))].T.astype(F32)
    bias = jnp.stack([rev[:, tq - 1 - t:tq - 1 - t + past] for t in range(tq)], axis=1)
    dnew = np.arange(tq)[:, None] - np.arange(tq)[None, :]
    bnew = jnp.where(jnp.asarray(dnew >= 0)[None],
                     jnp.transpose(tab[_t5_bucket_np(np.clip(dnew, 0, None))], (2, 0, 1)), NEG).astype(F32)
    kern = functools.partial(_sel_paged_kernel, n_src=pps, past=past)

    def page_spec(j):
        return pl.BlockSpec((None, None, page * CACHE_ROW_KINDS, HEAD_DIM),
                            lambda bi, pg, pt: (layer, pt[bi, pg * pps + j], 0, 0))

    rows = C_HPG * tq
    grid_spec = pltpu.PrefetchScalarGridSpec(
        num_scalar_prefetch=1,
        grid=(b, n_pages // pps),
        in_specs=[page_spec(j) for j in range(pps)] + [
            pl.BlockSpec((None, tq, C_WIDTH), lambda bi, pg, pt: (bi, 0, 0)),
            pl.BlockSpec((None, tq, 2 * C_KV), lambda bi, pg, pt: (bi, 0, (ODD_KV0 + 2 * C_KV) // (2 * C_KV))),
            pl.BlockSpec((None, C_KV_HEADS, tq, ns_pad), lambda bi, pg, pt: (bi, 0, 0, 0)),
            pl.BlockSpec((C_HEADS, tq, keys), lambda bi, pg, pt: (0, 0, pg)),
            pl.BlockSpec((C_HEADS, tq, tq), lambda bi, pg, pt: (0, 0, 0))],
        out_specs=pl.BlockSpec((None, tq, C_WIDTH), lambda bi, pg, pt: (bi, 0, 0)),
        scratch_shapes=[pltpu.VMEM((C_KV_HEADS, rows, 1), F32), pltpu.VMEM((C_KV_HEADS, rows, 1), F32),
                        pltpu.VMEM((C_KV_HEADS, rows, HEAD_DIM), F32)])
    return pl.pallas_call(
        kern, grid_spec=grid_spec, out_shape=jax.ShapeDtypeStruct((b, tq, C_WIDTH), F32),
        compiler_params=_cparams("parallel", "arbitrary"),
        name="sel_attn_paged",
    )(page_table, *([cache] * pps), z, z, sel, bias, bnew)


def _win_prompt_kernel(*refs, n_tiles):
    q_ref = refs[0]
    kv_refs = refs[1:1 + n_tiles]
    rev_ref, o_ref, bias_scr = refs[1 + n_tiles:]
    qi = pl.program_id(1)
    tq = q_ref.shape[0]
    lk = n_tiles * tq
    cols = C_HPG * tq

    @pl.when(qi == 0)
    def _():
        for h in range(C_HEADS):
            g, hh = divmod(h, C_HPG)
            bias_scr[g, :, hh * tq:(hh + 1) * tq] = _toeplitz(rev_ref[h], tq).T * LOG2E

    clamped = lax.broadcasted_iota(jnp.int32, (lk, cols), 0) < (n_tiles - 1 - qi) * tq
    for g in range(C_KV_HEADS):
        q = jnp.concatenate(
            [(q_ref[:, (g * C_HPG + h) * HEAD_DIM:(g * C_HPG + h + 1) * HEAD_DIM] * (SCALE * LOG2E)).astype(BF16)
             for h in range(C_HPG)], axis=0)
        k = jnp.concatenate([r[:, g * HEAD_DIM:(g + 1) * HEAD_DIM] for r in kv_refs], axis=0).astype(BF16)
        v_t = jnp.concatenate([r[:, C_KV + g * HEAD_DIM:C_KV + (g + 1) * HEAD_DIM] for r in kv_refs],
                              axis=0).T.astype(BF16)
        s = lax.dot_general(k, q, (((1,), (1,)), ((), ())), preferred_element_type=F32) + bias_scr[g]
        s = jnp.where(clamped, NEG, s)
        m = jnp.max(s, axis=0, keepdims=True)
        e = jnp.exp2(s - m)
        den = jnp.sum(e, axis=0, keepdims=True)
        o = jnp.dot(v_t, e.astype(BF16), preferred_element_type=F32) / den
        for h in range(C_HPG):
            o_ref[:, (g * C_HPG + h) * HEAD_DIM:(g * C_HPG + h + 1) * HEAD_DIM] = o[:, h * tq:(h + 1) * tq].T


def _win_attn_prompt(z, rel_bias, *, tq):
    b, t, _ = z.shape
    n_tiles = WIN // tq + 1
    rev = _c_rev_table(rel_bias, n_tiles * tq, tq, WIN)
    kv_blk = (ODD_KV0 + 4 * C_KV) // (2 * C_KV)
    kern = functools.partial(_win_prompt_kernel, n_tiles=n_tiles)

    def kv_spec(p):
        return pl.BlockSpec((None, tq, 2 * C_KV),
                            lambda bi, qi: (bi, jnp.maximum(qi - (n_tiles - 1 - p), 0), kv_blk))

    return pl.pallas_call(
        kern,
        grid=(b, t // tq),
        in_specs=[pl.BlockSpec((None, tq, C_WIDTH), lambda bi, qi: (bi, qi, 0))]
                 + [kv_spec(p) for p in range(n_tiles)]
                 + [pl.BlockSpec((C_HEADS, 1, (n_tiles + 1) * tq), lambda bi, qi: (0, 0, 0))],
        out_specs=pl.BlockSpec((None, tq, C_WIDTH), lambda bi, qi: (bi, qi, 0)),
        out_shape=jax.ShapeDtypeStruct((b, t, C_WIDTH), F32),
        scratch_shapes=[pltpu.VMEM((C_KV_HEADS, n_tiles * tq, C_HPG * tq), F32)],
        compiler_params=_cparams("parallel", "arbitrary"),
        name="win_attn_prompt",
    )(z, *([z] * n_tiles), rev)


def _win_attn_sample(z, win_full, rel_bias):
    b, tq, _ = z.shape
    lk = win_full.shape[1]
    dist = (lk - tq) + np.arange(tq)[:, None] - np.arange(lk)[None, :]
    valid = (dist >= 0) & (dist <= WIN)
    bias = jnp.transpose(_c_tab(rel_bias)[_t5_bucket_np(np.clip(dist, 0, None))], (2, 0, 1))
    bias = jnp.where(jnp.asarray(valid)[None], bias, NEG).astype(F32)
    kern = functools.partial(_tile_attn_kernel, nh=C_HEADS, rep=C_HPG, n_tiles=1, k_col=0, v_col=C_KV,
                             shared_kv=True, with_lse=False, lead_axis=0)
    return pl.pallas_call(
        kern,
        grid=(b,),
        in_specs=[pl.BlockSpec((None, tq, C_WIDTH), lambda bi: (bi, 0, 0)),
                  pl.BlockSpec((None, lk, 2 * C_KV), lambda bi: (bi, 0, 0)),
                  pl.BlockSpec((C_HEADS, tq, lk), lambda bi: (0, 0, 0))],
        out_specs=pl.BlockSpec((None, tq, C_WIDTH), lambda bi: (bi, 0, 0)),
        out_shape=jax.ShapeDtypeStruct((b, tq, C_WIDTH), F32),
        compiler_params=_cparams("parallel"),
        name="win_attn_sample",
    )(z, win_full, bias)


def _odd_mix_kernel(oc_ref, os_ref, ow_ref, gl_ref, gp_ref, mix_ref):
    gates = jax.nn.sigmoid(gl_ref[...])
    branches = (oc_ref, os_ref, ow_ref)
    for h in range(C_HEADS):
        cs = slice(h * HEAD_DIM, (h + 1) * HEAD_DIM)
        acc = gates[:, h:h + 1] * branches[0][:, cs]
        for br in range(1, 3):
            lane = br * C_HEADS + h
            acc = acc + gates[:, lane:lane + 1] * branches[br][:, cs]
        mix_ref[:, cs] = (acc * _silu(gp_ref[:, cs])).astype(BF16)


def _odd_mix(oc, osel, ow, z2, zgp, *, tm):
    m = z2.shape[0]
    wide = pl.BlockSpec((tm, C_WIDTH), lambda i: (i, 0))
    return pl.pallas_call(
        _odd_mix_kernel,
        grid=(m // tm,),
        in_specs=[wide, wide, wide, pl.BlockSpec((tm, LANES), lambda i: (i, ODD_GL0 // LANES)), wide],
        out_specs=wide,
        out_shape=jax.ShapeDtypeStruct((m, C_WIDTH), BF16),
        compiler_params=_cparams("parallel"),
        name="odd_mix",
    )(oc, osel, ow, z2, zgp)


def _odd_layer(x, past, win_hist, rel_bias, layer, norm_g, w_in, w_in_gp, cmp_pos, cmp_w1, cmp_b1, cmp_w2, w_out,
               *, tm, tn_in, tn_out, tq):
    b, t, d = x.shape
    m = b * t
    x2 = x.reshape(m, d)
    z2 = _norm_matmul(x2, norm_g[layer], w_in, layer, tm=tm, tn=tn_in, n_out=ODD_Z)
    zgp = _norm_matmul(x2, norm_g[layer], w_in_gp, layer, tm=tm, tn=tn_in)
    z = z2.reshape(b, t, ODD_Z)
    rows = z[:, :, ODD_KV0:ODD_KV0 + 4 * C_KV].reshape(b, t, 4, C_KV_HEADS, HEAD_DIM)
    win_new = z[:, :, ODD_KV0 + 4 * C_KV:ODD_KV0 + 6 * C_KV]
    if past is None:
        length = t
        a, bm = _cmp_partials_prompt(z, layer, cmp_pos, cmp_w1)
        qpos0 = 0
    else:
        cache, page_table = past
        p_len = page_table.shape[1] * (cache.shape[2] // CACHE_ROW_KINDS)
        length = p_len + t
        assert length // CMP_STRIDE == p_len // CMP_STRIDE and p_len % SEL_BLK == 0
        a, bm = _cmp_partials_paged(cache, layer, page_table, cmp_pos, cmp_w1)
        qpos0 = p_len
    n_sel = -(-length // SEL_BLK)
    oc, sel = _cmp_select(z, a, bm, layer, cmp_b1, cmp_w2, tq=tq, qpos0=qpos0, n_sel=n_sel)
    if past is None:
        osel = _sel_attn_prompt(z, sel, rel_bias)
        ow = _win_attn_prompt(z, rel_bias, tq=tq)
        new_win = win_new[:, t - min(WIN, t):].reshape(b, min(WIN, t), 2, C_KV_HEADS, HEAD_DIM)
    else:
        osel = _sel_attn_paged(z, sel, cache, layer, page_table, rel_bias)
        win_full = jnp.concatenate([win_hist.reshape(b, -1, 2 * C_KV), win_new], axis=1)
        ow = _win_attn_sample(z, win_full, rel_bias)
        lk = win_full.shape[1]
        keep = min(WIN, length)
        new_win = win_full[:, lk - keep:].reshape(b, keep, 2, C_KV_HEADS, HEAD_DIM)
    mix = _odd_mix(oc.reshape(m, C_WIDTH), osel.reshape(m, C_WIDTH), ow.reshape(m, C_WIDTH), z2, zgp,
                   tm=min(tm, 256))
    y = _matmul_res([mix], w_out, layer, x2, tm=tm, tn=tn_out)
    return y.reshape(b, t, d), rows, new_win


def _run_trunk(x, a_caches, conv_state, c_cache, c_win, page_table, rel_bias, norm_even, w_in_even, conv_w,
               conv_b, conv_ln_g, conv_ln_b, conv_pw_w, conv_pw_b, w_out_even, norm_odd, w_in_odd, w_in_odd_gp,
               cmp_pos, cmp_w1, cmp_b1, cmp_w2, w_out_odd, final_norm, *, tm, tt, tq):
    b, t, d = x.shape
    prompt = a_caches is None
    depth = norm_even.shape[0] + norm_odd.shape[0]
    new_a = [[] for _ in A_GROUPS]
    a_new = None
    new_conv, new_rows, new_win = [], [], []
    for depth_i in range(depth):
        i = depth_i // 2
        if depth_i % 2 == 0:
            conv_hist = jnp.zeros((b, CONV_W - 1, B_WIDTH), F32) if prompt else conv_state[i]
            x, hists, conv = _even_layer(x, a_caches, a_new, conv_hist, rel_bias, i, norm_even, w_in_even, conv_w,
                                         conv_b, conv_ln_g, conv_ln_b, conv_pw_w, conv_pw_b,
                                         w_out_even, tm=tm, tn_in=512, tn_out=512, tt=tt)
            if prompt:
                for g in range(A_NG):
                    new_a[g].append(hists[g])
            else:
                a_new = hists
            new_conv.append(conv)
        else:
            past = None if prompt else (c_cache, page_table)
            x, rows, win = _odd_layer(x, past, None if prompt else c_win[i], rel_bias, i, norm_odd,
                                      w_in_odd, w_in_odd_gp, cmp_pos, cmp_w1, cmp_b1, cmp_w2,
                                      w_out_odd, tm=tm, tn_in=512, tn_out=512, tq=tq)
            new_rows.append(rows)
            new_win.append(win)
    y = _rms_norm(x.reshape(b * t, d), final_norm, tm=min(tm, 256)).reshape(b, t, d)
    if prompt:
        a_out = [jnp.stack(a) for a in new_a]
    else:
        a_out = [n.reshape(c.shape) for n, c in zip(a_new, a_caches)]
    return (y, a_out, jnp.stack(new_conv), jnp.stack(new_rows), jnp.stack(new_win))


def kernel(x_prompt, x_sample, cache_a_kv0, cache_a_kv1, cache_a_kv2, state_b_conv, cache_c_kv, cache_c_win, page_table, rel_bias, norm_even, w_in_even, conv_w, conv_b, conv_ln_g, conv_ln_b, conv_pw_w, conv_pw_b, w_out_even, norm_odd, w_in_odd, cmp_pos, cmp_w1, cmp_b1, cmp_w2, w_out_odd, final_norm):
    w_in_odd_gp = w_in_odd[:, :, ODD_GP0:]
    weights = (rel_bias, norm_even, w_in_even, conv_w, conv_b, conv_ln_g, conv_ln_b, conv_pw_w, conv_pw_b,
               w_out_even, norm_odd, w_in_odd, w_in_odd_gp, cmp_pos, cmp_w1, cmp_b1, cmp_w2, w_out_odd, final_norm)
    y_p, a_p, conv_p, rows_p, win_p = _run_trunk(x_prompt, None, None, None, None, None, *weights,
                                                 tm=2048, tt=256, tq=128)
    n_odd, n_pool, page = cache_c_kv.shape[:3]
    c_cache = cache_c_kv.reshape(n_odd, n_pool, page * CACHE_ROW_KINDS, HEAD_DIM)
    db, dt = x_sample.shape[:2]
    y_s, a_s, conv_s, rows_s, win_s = _run_trunk(x_sample, (cache_a_kv0, cache_a_kv1, cache_a_kv2), state_b_conv,
                                                 c_cache, cache_c_win, page_table, *weights,
                                                 tm=db * dt, tt=dt, tq=dt)
    return (y_p, y_s, a_p[0], a_p[1], a_p[2], conv_p, rows_p, win_p, a_s[0], a_s[1], a_s[2], conv_s, rows_s, win_s)
```

```python
import functools
import math

import numpy as np
import jax
import jax.numpy as jnp
from jax import lax
from jax.experimental import pallas as pl
from jax.experimental.pallas import tpu as pltpu

F32 = jnp.float32
BF16 = jnp.bfloat16

D_MODEL = 2048
HEAD_DIM = 128
LANES = 128
SUBLANES = 8
A_GROUPS = ((128, 1), (512, 4), (2048, 16))
A_NG = 3
A_HPG = 8
A_QKV = A_NG * A_HPG * HEAD_DIM
A_WIDTH = A_HPG * HEAD_DIM
B_WIDTH = 1024
CONV_W = 31
CONV_HALO = 32
EVEN_IN = 3 * A_QKV + A_WIDTH + 3 * B_WIDTH
C_HEADS = 16
C_KV_HEADS = 2
C_HPG = C_HEADS // C_KV_HEADS
C_WIDTH = C_HEADS * HEAD_DIM
C_KV = C_KV_HEADS * HEAD_DIM
CMP_BLK = 32
CMP_STRIDE = 16
CMP_HID = 128
SEL_BLK = 64
SEL_N = 16
WIN = 512
NUM_BUCKETS = 32
MAX_DIST = 2048
EPS = 1e-6
NEG = -1e30
SCALE = HEAD_DIM ** -0.5
LOG2E = math.log2(math.e)
VMEM_LIMIT = 56 * 1024 * 1024


def _cparams(*sem):
    return pltpu.CompilerParams(dimension_semantics=sem, vmem_limit_bytes=VMEM_LIMIT)


def _t5_bucket_np(dist):
    max_exact = NUM_BUCKETS // 2
    d = np.maximum(np.asarray(dist, np.int64), 0)
    ratio = np.log(np.maximum(d, 1).astype(np.float64) / max_exact) / math.log(MAX_DIST / max_exact)
    large = np.minimum(max_exact + (ratio * (NUM_BUCKETS - max_exact)).astype(np.int64), NUM_BUCKETS - 1)
    return np.where(d < max_exact, d, large).astype(np.int32)


def _silu(x):
    return x * jax.nn.sigmoid(x)


NORM_ROWS = 256


def _norm_matmul_kernel(x_ref, g_ref, w_ref, o_ref, xn_ref):
    @pl.when(pl.program_id(1) == 0)
    def _():
        rows = min(NORM_ROWS, x_ref.shape[0])

        def norm(c, carry):
            cur = pl.ds(pl.multiple_of(c * rows, rows), rows)
            x = x_ref[cur, :]
            ms = jnp.mean(x * x, axis=-1, keepdims=True)
            xn_ref[cur, :] = (x * lax.rsqrt(ms + EPS) * g_ref[...]).astype(BF16)
            return carry

        lax.fori_loop(0, x_ref.shape[0] // rows, norm, 0)

    o_ref[...] = jnp.dot(xn_ref[...], w_ref[...].astype(BF16), preferred_element_type=F32)


def _norm_matmul(x, g, w, layer, *, tm, tn, n_out=None):
    m, k = x.shape
    n = w.shape[2] if n_out is None else n_out
    assert n % tn == 0 and n <= w.shape[2]
    return pl.pallas_call(
        _norm_matmul_kernel,
        grid=(m // tm, n // tn),
        in_specs=[pl.BlockSpec((tm, k), lambda i, j: (i, 0), pipeline_mode=pl.Buffered(1)),
                  pl.BlockSpec((1, k), lambda i, j: (0, 0)),
                  pl.BlockSpec((None, k, tn), lambda i, j: (layer, 0, j))],
        out_specs=pl.BlockSpec((tm, tn), lambda i, j: (i, j)),
        out_shape=jax.ShapeDtypeStruct((m, n), F32),
        scratch_shapes=[pltpu.VMEM((tm, k), BF16)],
        compiler_params=_cparams("parallel", "arbitrary"),
        name="norm_matmul",
    )(x, g.reshape(1, k), w)


def _matmul_res_kernel(*refs, n_parts):
    a_refs = refs[:n_parts]
    w_ref, r_ref, o_ref = refs[n_parts:]
    acc = r_ref[...]
    k0 = 0
    for a_ref in a_refs:
        kp = a_ref.shape[1]
        acc = acc + jnp.dot(a_ref[...], w_ref[k0:k0 + kp, :].astype(BF16), preferred_element_type=F32)
        k0 += kp
    o_ref[...] = acc


def _matmul_res(parts, w, layer, res, *, tm, tn):
    m = res.shape[0]
    k, n = w.shape[1:]
    assert sum(p.shape[1] for p in parts) == k
    return pl.pallas_call(
        functools.partial(_matmul_res_kernel, n_parts=len(parts)),
        grid=(m // tm, n // tn),
        in_specs=[pl.BlockSpec((tm, p.shape[1]), lambda i, j: (i, 0), pipeline_mode=pl.Buffered(1))
                  for p in parts]
                 + [pl.BlockSpec((None, k, tn), lambda i, j: (layer, 0, j)),
                    pl.BlockSpec((tm, tn), lambda i, j: (i, j))],
        out_specs=pl.BlockSpec((tm, tn), lambda i, j: (i, j)),
        out_shape=jax.ShapeDtypeStruct((m, n), F32),
        compiler_params=_cparams("parallel", "arbitrary"),
        name="matmul_res",
    )(*parts, w, res)


def _rms_kernel(x_ref, g_ref, o_ref):
    x = x_ref[...]
    ms = jnp.mean(x * x, axis=-1, keepdims=True)
    o_ref[...] = x * lax.rsqrt(ms + EPS) * g_ref[...]


def _rms_norm(x, g, *, tm):
    m, k = x.shape
    return pl.pallas_call(
        _rms_kernel,
        grid=(m // tm,),
        in_specs=[pl.BlockSpec((tm, k), lambda i: (i, 0)), pl.BlockSpec((1, k), lambda i: (0, 0))],
        out_specs=pl.BlockSpec((tm, k), lambda i: (i, 0)),
        out_shape=jax.ShapeDtypeStruct((m, k), F32),
        compiler_params=_cparams("parallel"),
        name="rms_norm",
    )(x, g.reshape(1, k))


def _c_tab(rel_bias):
    return rel_bias[:, :C_HEADS]


def _c_rev_table(rel_bias, length, tq, max_dist):
    dist = length - np.arange(length + tq)
    valid = (dist >= 0) & (dist <= max_dist)
    tab = _c_tab(rel_bias)[_t5_bucket_np(np.clip(dist, 0, None))]
    tab = jnp.where(jnp.asarray(valid)[:, None], tab, NEG).astype(F32)
    return tab.T.reshape(C_HEADS, 1, length + tq)


def _toeplitz(vec, tq):
    w = vec.shape[1]
    return pltpu.roll(jnp.broadcast_to(vec, (tq, w)), 0, 1, stride=1, stride_axis=0)[:, tq:]


def _tile_attn_kernel(*refs, nh, rep, n_tiles, k_col, v_col, shared_kv, with_lse, lead_axis):
    q_ref = refs[0]
    if shared_kv:
        k_refs = v_refs = refs[1:1 + n_tiles]
        nxt = 1 + n_tiles
    else:
        k_refs = refs[1:1 + n_tiles]
        v_refs = refs[1 + n_tiles:1 + 2 * n_tiles]
        nxt = 1 + 2 * n_tiles
    b_ref = refs[nxt]
    o_ref = refs[nxt + 1]
    lse_ref = refs[nxt + 2] if with_lse else None
    tq = q_ref.shape[0]
    tk = k_refs[0].shape[0]
    lk = n_tiles * tk
    if n_tiles > 1:
        qi = pl.program_id(lead_axis)
        col = lax.broadcasted_iota(jnp.int32, (tq, lk), 1)
        pad_mask = jnp.where(col < (n_tiles - 1 - qi) * tk, NEG, 0.0).astype(F32)
    else:
        pad_mask = None
    if with_lse:
        lane = lax.broadcasted_iota(jnp.int32, (tq, LANES), 1)
        lse_t = jnp.zeros((tq, LANES), F32)
    for j in range(nh // rep):
        kc = k_col + j * HEAD_DIM
        vc = v_col + j * HEAD_DIM
        heads = range(j * rep, (j + 1) * rep)
        q = jnp.concatenate([(q_ref[:, h * HEAD_DIM:(h + 1) * HEAD_DIM] * SCALE).astype(BF16) for h in heads], axis=0)
        if n_tiles > 1:
            k = jnp.concatenate([r[:, kc:kc + HEAD_DIM] for r in k_refs], axis=0).astype(BF16)
            v = jnp.concatenate([r[:, vc:vc + HEAD_DIM] for r in v_refs], axis=0).astype(BF16)
        else:
            k = k_refs[0][:, kc:kc + HEAD_DIM].astype(BF16)
            v = v_refs[0][:, vc:vc + HEAD_DIM].astype(BF16)
        s = lax.dot_general(q, k, (((1,), (1,)), ((), ())), preferred_element_type=F32)
        s = s.reshape(rep, tq, lk) + b_ref[j * rep:(j + 1) * rep]
        if pad_mask is not None:
            s = s + pad_mask[None]
        s = s.reshape(rep * tq, lk)
        m = jnp.max(s, axis=-1, keepdims=True)
        e = jnp.exp(s - m)
        den = jnp.sum(e, axis=-1, keepdims=True)
        o = jnp.dot(e.astype(BF16), v, preferred_element_type=F32) / den
        for r, h in enumerate(heads):
            o_ref[:, h * HEAD_DIM:(h + 1) * HEAD_DIM] = o[r * tq:(r + 1) * tq]
        if with_lse:
            lse = m + jnp.log(den)
            for r, h in enumerate(heads):
                lse_t = jnp.where(lane == h, lse[r * tq:(r + 1) * tq], lse_t)
    if with_lse:
        lse_ref[...] = lse_t


A_TU = 128


def _a_rev_table(rel_bias):
    j = 2 * A_TU - np.arange(3 * A_TU)
    rows = []
    for g, (win, dil) in enumerate(A_GROUPS):
        valid = (j >= 0) & (j <= win // dil)
        tab = rel_bias[:, g * A_HPG:(g + 1) * A_HPG][_t5_bucket_np(np.clip(j, 0, None) * dil)]
        rows.append(jnp.where(jnp.asarray(valid)[:, None], tab, NEG).T)
    return jnp.stack(rows, axis=1).astype(F32)


def _a_prompt_kernel(q0_ref, q1_ref, q2_ref, k0_ref, k1_ref, k2_ref, v0_ref, v1_ref, v2_ref, ga_ref, rev_ref,
                     mix_ref, o_scr, l_scr, bias_scr):
    t = q0_ref.shape[0]
    tu = A_TU
    q_refs, k_refs, v_refs = (q0_ref, q1_ref, q2_ref), (k0_ref, k1_ref, k2_ref), (v0_ref, v1_ref, v2_ref)
    for g in range(A_NG):
        bias_scr[g] = _toeplitz(rev_ref[g:g + 1, :], tu)
    col = lax.broadcasted_iota(jnp.int32, (tu, 2 * tu), 1)
    for g, (win, dil) in enumerate(A_GROUPS):
        n_u = t // (dil * tu)
        q_ref, k_ref, v_ref = q_refs[g], k_refs[g], v_refs[g]

        def rows(r, u, dil=dil):
            start = r + u * (tu * dil)
            if dil == 1:
                return pl.ds(pl.multiple_of(start, tu), tu)
            return pl.ds(start, tu, stride=dil)

        def body(it, carry, g=g, n_u=n_u, q_ref=q_ref, k_ref=k_ref, v_ref=v_ref, rows=rows):
            r = it // n_u
            u = it % n_u
            cur = rows(r, u)
            q = (q_ref[cur, :] * SCALE).astype(BF16)
            prev = rows(r, jnp.maximum(u - 1, 0)) if n_u > 1 else cur
            k = jnp.concatenate([k_ref[prev, :], k_ref[cur, :]], axis=0).astype(BF16)
            v = jnp.concatenate([v_ref[prev, :], v_ref[cur, :]], axis=0).astype(BF16)
            bias = bias_scr[g] + jnp.where((col < tu) & (u == 0), NEG, 0.0)
            s = lax.dot_general(q, k, (((1,), (1,)), ((), ())), preferred_element_type=F32) + bias
            m = jnp.max(s, axis=-1, keepdims=True)
            e = jnp.exp(s - m)
            den = jnp.sum(e, axis=-1, keepdims=True)
            o_scr[g, cur, :] = jnp.dot(e.astype(BF16), v, preferred_element_type=F32) / den
            l_scr[g, cur, :] = jnp.broadcast_to(m + jnp.log(den), (tu, HEAD_DIM))
            return carry

        lax.fori_loop(0, t // tu, body, 0, unroll=8)

    def combine(c, carry):
        cur = pl.ds(pl.multiple_of(c * tu, tu), tu)
        ls = [l_scr[g, cur, :] for g in range(A_NG)]
        m = jnp.maximum(jnp.maximum(ls[0], ls[1]), ls[2])
        es = [jnp.exp(l - m) for l in ls]
        acc = es[0] * o_scr[0, cur, :] + es[1] * o_scr[1, cur, :] + es[2] * o_scr[2, cur, :]
        out = acc / (es[0] + es[1] + es[2])
        mix_ref[cur, :] = (out * _silu(ga_ref[cur, :])).astype(BF16)
        return carry

    lax.fori_loop(0, t // tu, combine, 0)


def _a_attn_prompt(z, rel_bias):
    b, t, n = z.shape
    for win, dil in A_GROUPS:
        assert win // dil == A_TU and t % (dil * A_TU) == 0
    rev = _a_rev_table(rel_bias)
    nq = A_QKV // HEAD_DIM
    col = lambda base: pl.BlockSpec((None, t, HEAD_DIM), lambda bi, h: (bi, 0, base + h))
    return pl.pallas_call(
        _a_prompt_kernel,
        grid=(b, A_HPG),
        in_specs=[col(g * A_HPG) for g in range(A_NG)]
                 + [col(nq + g * A_HPG) for g in range(A_NG)]
                 + [col(2 * nq + g * A_HPG) for g in range(A_NG)]
                 + [col(3 * nq), pl.BlockSpec((None, A_NG, 3 * A_TU), lambda bi, h: (h, 0, 0))],
        out_specs=pl.BlockSpec((None, t, HEAD_DIM), lambda bi, h: (bi, 0, h)),
        out_shape=jax.ShapeDtypeStruct((b, t, A_WIDTH), BF16),
        scratch_shapes=[pltpu.VMEM((A_NG, t, HEAD_DIM), F32), pltpu.VMEM((A_NG, t, HEAD_DIM), F32),
                        pltpu.VMEM((A_NG, A_TU, 2 * A_TU), F32)],
        compiler_params=_cparams("parallel", "parallel"),
        name="a_attn_prompt",
    )(*([z] * 10), rev)


def _a_bias_sample(rel_bias, g, hist_len, tq):
    win, dil = A_GROUPS[g]
    lk = hist_len + tq
    dist = hist_len + tq - 1 - np.arange(lk + tq - 1)
    valid = (dist >= 0) & (dist <= win) & (dist % dil == 0)
    tab = rel_bias[:, g * A_HPG:(g + 1) * A_HPG][_t5_bucket_np(np.clip(dist, 0, None))]
    rev = jnp.where(jnp.asarray(valid)[:, None], tab, NEG).T.astype(F32)
    return jnp.stack([rev[:, tq - 1 - t:tq - 1 - t + lk] for t in range(tq)], axis=1)


A_ROWS = 2 * A_HPG
A_CHUNK = 512


def _a_sample_kernel(*refs, aliased, layer):
    q_ref, kn_ref, vn_ref, old_ref, head_ref, bias_ref, bnew_ref = refs[:7]
    o_ref, lse_ref, new_ref, m_ref, l_ref, acc_ref = refs[7 + int(aliased):]
    if not aliased:
        for other in range(new_ref.shape[0]):
            if other != layer:
                new_ref[other] = jnp.zeros(new_ref.shape[1:], F32)
        new_ref = new_ref.at[layer]
    c = pl.program_id(1)
    last = pl.num_programs(1) - 1
    tq = q_ref.shape[0]
    n_tok = old_ref.shape[0] // A_ROWS
    shift = tq * A_ROWS

    @pl.when(c == 0)
    def _():
        m_ref[...] = jnp.full(m_ref.shape, NEG, F32)
        l_ref[...] = jnp.zeros(l_ref.shape, F32)
        acc_ref[...] = jnp.zeros(acc_ref.shape, F32)

    def update(h, q, k, v, bias):
        s = lax.dot_general(q, k, (((1,), (1,)), ((), ())), preferred_element_type=F32) + bias
        m_old = m_ref[h]
        m_new = jnp.maximum(m_old, jnp.max(s, axis=-1, keepdims=True))
        alpha = jnp.exp(m_old - m_new)
        e = jnp.exp(s - m_new)
        m_ref[h] = m_new
        l_ref[h] = alpha * l_ref[h] + jnp.sum(e, axis=-1, keepdims=True)
        acc_ref[h] = alpha * acc_ref[h] + jnp.dot(e.astype(BF16), v, preferred_element_type=F32)

    qs = [(q_ref[:, h * HEAD_DIM:(h + 1) * HEAD_DIM] * SCALE).astype(BF16) for h in range(A_HPG)]
    for h in range(A_HPG):
        k = old_ref[pl.ds(h, n_tok, stride=A_ROWS), :].astype(BF16)
        v = old_ref[pl.ds(A_HPG + h, n_tok, stride=A_ROWS), :].astype(BF16)
        update(h, qs[h], k, v, bias_ref[h])

    new_ref[0:n_tok * A_ROWS - shift, :] = old_ref[shift:, :]

    @pl.when(c < last)
    def _():
        new_ref[n_tok * A_ROWS - shift:, :] = head_ref[...]

    @pl.when(c == last)
    def _():
        lane = lax.broadcasted_iota(jnp.int32, (tq, LANES), 1)
        lse_t = jnp.zeros((tq, LANES), F32)
        for h in range(A_HPG):
            cs = slice(h * HEAD_DIM, (h + 1) * HEAD_DIM)
            update(h, qs[h], kn_ref[:, cs].astype(BF16), vn_ref[:, cs].astype(BF16), bnew_ref[h])
            o_ref[:, cs] = acc_ref[h] / l_ref[h]
            lse_t = jnp.where(lane == h, m_ref[h] + jnp.log(l_ref[h]), lse_t)
            base = n_tok * A_ROWS - shift
            new_ref[pl.ds(base + h, tq, stride=A_ROWS), :] = kn_ref[:, cs]
            new_ref[pl.ds(base + A_HPG + h, tq, stride=A_ROWS), :] = vn_ref[:, cs]
        lse_ref[...] = lse_t


def _a_attn_sample(z, cache, layer, new_cache, rel_bias, g):
    b, tq, n = z.shape
    n_layers, _, w = cache.shape[:3]
    assert cache.shape[3:] == (2, A_HPG, HEAD_DIM) and w % LANES == 0 and tq * A_ROWS == LANES
    ch = min(w, A_CHUNK)
    n_ch = w // ch
    old = cache.reshape(n_layers, b, w * A_ROWS, HEAD_DIM)
    bias_full = _a_bias_sample(rel_bias, g, w, tq)
    bias = jnp.stack([bias_full[:, :, i * ch:(i + 1) * ch] for i in range(n_ch)])
    bnew = bias_full[:, :, w:]
    nq = A_QKV // A_WIDTH
    zcol = lambda cb: pl.BlockSpec((None, tq, A_WIDTH), lambda bi, c: (bi, 0, cb))
    heads_per_chunk = ch * A_ROWS // LANES
    in_specs = [zcol(g), zcol(nq + g), zcol(2 * nq + g),
                pl.BlockSpec((None, None, ch * A_ROWS, HEAD_DIM), lambda bi, c: (layer, bi, c, 0)),
                pl.BlockSpec((None, None, LANES, HEAD_DIM),
                             lambda bi, c: (layer, bi, jnp.minimum(c + 1, n_ch - 1) * heads_per_chunk, 0)),
                pl.BlockSpec((None, A_HPG, tq, ch), lambda bi, c: (c, 0, 0, 0)),
                pl.BlockSpec((A_HPG, tq, tq), lambda bi, c: (0, 0, 0))]
    args = [z, z, z, old, old, bias, bnew]
    aliases = {}
    if new_cache is not None:
        in_specs.append(pl.BlockSpec(memory_space=pl.ANY))
        args.append(new_cache)
        aliases = {len(args) - 1: 2}
        new_spec = pl.BlockSpec((None, None, ch * A_ROWS, HEAD_DIM), lambda bi, c: (layer, bi, c, 0))
    else:
        new_spec = pl.BlockSpec((n_layers, None, ch * A_ROWS, HEAD_DIM), lambda bi, c: (0, bi, c, 0))
    o, lse, new = pl.pallas_call(
        functools.partial(_a_sample_kernel, aliased=new_cache is not None, layer=layer),
        grid=(b, n_ch),
        in_specs=in_specs,
        out_specs=[pl.BlockSpec((None, tq, A_WIDTH), lambda bi, c: (bi, 0, 0)),
                   pl.BlockSpec((None, tq, LANES), lambda bi, c: (bi, 0, 0)),
                   new_spec],
        out_shape=[jax.ShapeDtypeStruct((b, tq, A_WIDTH), F32),
                   jax.ShapeDtypeStruct((b, tq, LANES), F32),
                   jax.ShapeDtypeStruct(old.shape, F32)],
        scratch_shapes=[pltpu.VMEM((A_HPG, tq, 1), F32), pltpu.VMEM((A_HPG, tq, 1), F32),
                        pltpu.VMEM((A_HPG, tq, HEAD_DIM), F32)],
        input_output_aliases=aliases,
        compiler_params=_cparams("parallel", "arbitrary"),
        name=f"a_attn_sample_g{g}",
    )(*args)
    return o, lse, new


def _conv_kernel(a_ref, gt_ref, gb_ref, hist_ref, cw_ref, cb_ref, lg_ref, lb_ref, pw_ref, pb_ref,
                 o_ref, nc_ref, ubuf, ybuf, pwb):
    ti = pl.program_id(1)
    nt = pl.num_programs(1)
    tt = a_ref.shape[0]
    nhist = CONV_W - 1
    pad = CONV_HALO - nhist

    @pl.when(ti == 0)
    def _():
        pwb[...] = pw_ref[...].astype(BF16)
        ubuf[0:pad, :] = jnp.zeros((pad, B_WIDTH), F32)
        ubuf[pad:CONV_HALO, :] = hist_ref[...]

    ubuf[CONV_HALO:CONV_HALO + tt, :] = a_ref[...] * jax.nn.sigmoid(gt_ref[...])
    for c in range(B_WIDTH // LANES):
        cs = slice(c * LANES, (c + 1) * LANES)
        acc = jnp.zeros((tt, LANES), F32) + cb_ref[:, cs]
        for r in range(SUBLANES):
            taps = [a for a in range(CONV_HALO // SUBLANES + 1) if 0 <= SUBLANES * a + r - pad < CONV_W]
            span = tt if r == 0 else tt + SUBLANES
            z = None
            for a in taps:
                k = SUBLANES * a + r - pad
                term = ubuf[SUBLANES * a:SUBLANES * a + span, cs] * cw_ref[k:k + 1, cs]
                z = term if z is None else z + term
            acc = acc + (z if r == 0 else z[r:r + tt])
        ybuf[:, cs] = acc
    y = ybuf[...]
    mu = jnp.mean(y, axis=-1, keepdims=True)
    yc = y - mu
    var = jnp.mean(yc * yc, axis=-1, keepdims=True)
    yn = yc * lax.rsqrt(var + EPS) * lg_ref[...] + lb_ref[...]
    act = _silu(yn).astype(BF16)
    ob = jnp.dot(act, pwb[...], preferred_element_type=F32) + pb_ref[...]
    o_ref[...] = (ob * _silu(gb_ref[...])).astype(BF16)

    @pl.when(ti == nt - 1)
    def _():
        nc_ref[...] = ubuf[tt + pad:tt + CONV_HALO, :]

    @pl.when(ti < nt - 1)
    def _():
        ubuf[0:CONV_HALO, :] = ubuf[tt:tt + CONV_HALO, :]


def _conv_module(z, hist, layer, conv_w, conv_b, ln_g, ln_b, pw_w, pw_b, *, tt):
    b, t, n = z.shape
    assert t % tt == 0 and (tt >= CONV_HALO or t == tt)
    glu0 = (3 * A_QKV + A_WIDTH) // B_WIDTH
    row = lambda a: a.reshape(a.shape[0], 1, B_WIDTH)
    blk = (None, tt, B_WIDTH)
    lyr = lambda shape: pl.BlockSpec((None,) + shape, lambda bi, ti: (layer,) + (0,) * len(shape))
    return pl.pallas_call(
        _conv_kernel,
        grid=(b, t // tt),
        in_specs=[pl.BlockSpec(blk, lambda bi, ti: (bi, ti, glu0)),
                  pl.BlockSpec(blk, lambda bi, ti: (bi, ti, glu0 + 1)),
                  pl.BlockSpec(blk, lambda bi, ti: (bi, ti, glu0 + 2)),
                  pl.BlockSpec((None, CONV_W - 1, B_WIDTH), lambda bi, ti: (bi, 0, 0)),
                  lyr((CONV_W, B_WIDTH)), lyr((1, B_WIDTH)), lyr((1, B_WIDTH)), lyr((1, B_WIDTH)),
                  lyr((B_WIDTH, B_WIDTH)), lyr((1, B_WIDTH))],
        out_specs=[pl.BlockSpec(blk, lambda bi, ti: (bi, ti, 0)),
                   pl.BlockSpec((None, CONV_W - 1, B_WIDTH), lambda bi, ti: (bi, 0, 0))],
        out_shape=[jax.ShapeDtypeStruct((b, t, B_WIDTH), BF16),
                   jax.ShapeDtypeStruct((b, CONV_W - 1, B_WIDTH), F32)],
        scratch_shapes=[pltpu.VMEM((CONV_HALO + tt, B_WIDTH), F32),
                        pltpu.VMEM((tt, B_WIDTH), F32),
                        pltpu.VMEM((B_WIDTH, B_WIDTH), BF16)],
        compiler_params=_cparams("parallel", "arbitrary"),
        name="conv_module",
    )(z, z, z, hist, conv_w, row(conv_b), row(ln_g), row(ln_b), pw_w, row(pw_b))


def _even_mix_kernel(o0_ref, o1_ref, o2_ref, l0_ref, l1_ref, l2_ref, ga_ref, mix_ref):
    ls = [l0_ref[...], l1_ref[...], l2_ref[...]]
    m = jnp.maximum(jnp.maximum(ls[0], ls[1]), ls[2])
    es = [jnp.exp(l - m) for l in ls]
    inv = 1.0 / (es[0] + es[1] + es[2])
    ws = [e * inv for e in es]
    o_refs = (o0_ref, o1_ref, o2_ref)
    for h in range(A_HPG):
        cs = slice(h * HEAD_DIM, (h + 1) * HEAD_DIM)
        acc = ws[0][:, h:h + 1] * o_refs[0][:, cs]
        for g in range(1, A_NG):
            acc = acc + ws[g][:, h:h + 1] * o_refs[g][:, cs]
        mix_ref[:, cs] = (acc * _silu(ga_ref[:, cs])).astype(BF16)


def _even_mix(oas, lses, z2, *, tm):
    m = z2.shape[0]
    ga_blk = 3 * A_QKV // A_WIDTH
    wide = lambda c: pl.BlockSpec((tm, A_WIDTH), lambda i: (i, c))
    narrow = pl.BlockSpec((tm, LANES), lambda i: (i, 0))
    return pl.pallas_call(
        _even_mix_kernel,
        grid=(m // tm,),
        in_specs=[wide(0), wide(0), wide(0), narrow, narrow, narrow, wide(ga_blk)],
        out_specs=wide(0),
        out_shape=jax.ShapeDtypeStruct((m, A_WIDTH), BF16),
        compiler_params=_cparams("parallel"),
        name="even_mix",
    )(*oas, *lses, z2)


def _even_layer(x, a_caches, a_new, conv_hist, rel_bias, layer, norm_g, w_in, conv_w, conv_b, ln_g, ln_b, pw_w,
                pw_b, w_out, *, tm, tn_in, tn_out, tt):
    b, t, d = x.shape
    m = b * t
    x2 = x.reshape(m, d)
    z2 = _norm_matmul(x2, norm_g[layer], w_in, layer, tm=tm, tn=tn_in)
    z = z2.reshape(b, t, EVEN_IN)
    oas, lses, new_hists = [], [], []
    for g, (win, _) in enumerate(A_GROUPS):
        if a_caches is None:
            kcol = A_QKV + g * A_WIDTH
            vcol = 2 * A_QKV + g * A_WIDTH
            new_kv = jnp.stack([z[:, :, kcol:kcol + A_WIDTH], z[:, :, vcol:vcol + A_WIDTH]], axis=2)
            new_hists.append(new_kv[:, t - min(win, t):].reshape(b, min(win, t), 2, A_HPG, HEAD_DIM))
        else:
            assert a_caches[g].shape[2] == win
            o, lse, new = _a_attn_sample(z, a_caches[g], layer, None if a_new is None else a_new[g], rel_bias, g)
            new_hists.append(new)
            oas.append(o.reshape(m, A_WIDTH))
            lses.append(lse.reshape(m, LANES))
    if a_caches is None:
        mix_a = _a_attn_prompt(z, rel_bias).reshape(m, A_WIDTH)
    else:
        mix_a = _even_mix(oas, lses, z2, tm=min(tm, 256))
    mix_b, new_conv = _conv_module(z, conv_hist, layer, conv_w, conv_b, ln_g, ln_b, pw_w, pw_b, tt=tt)
    y = _matmul_res([mix_a, mix_b.reshape(m, B_WIDTH)], w_out, layer, x2, tm=tm, tn=tn_out)
    return y.reshape(b, t, d), new_hists, new_conv


ODD_KV0 = C_WIDTH
ODD_GL0 = C_WIDTH + 6 * C_KV
ODD_GP0 = ODD_GL0 + 3 * C_HEADS
ODD_Z = ODD_GL0 + 512
CMP_HALF = CMP_BLK // 2
CMP_K = CMP_HALF * HEAD_DIM
PAGES_PER_STEP = 16
CACHE_KINDS = 2 * C_KV_HEADS
CACHE_ROW_KINDS = 2 * CACHE_KINDS


def _cmp_partials_kernel(*refs, n_src, n_prefetch, interleaved):
    refs = refs[n_prefetch:]
    src = refs[:n_src]
    pos_ref, w1_ref, a_ref, b_ref, rows_ref, w1b_ref = refs[n_src:n_src + 6]
    n = rows_ref.shape[1] // CMP_HALF

    if n_prefetch:
        @pl.when(pl.program_id(1) == 0)
        def _():
            w1b_ref[...] = w1_ref[...].astype(BF16)
    else:
        w1b_ref[...] = w1_ref[...].astype(BF16)

    r0 = 0
    for r in src:
        nr = r.shape[0] // CACHE_ROW_KINDS if interleaved else r.shape[0]
        for c in range(CACHE_KINDS):
            if interleaved:
                rows_ref[c, r0:r0 + nr, :] = r[pl.ds(c, nr, stride=CACHE_ROW_KINDS), :]
            else:
                rows_ref[c, r0:r0 + nr, :] = r[:, c * HEAD_DIM:(c + 1) * HEAD_DIM]
        r0 += nr
    for kv in range(2):
        x = jnp.concatenate(
            [jnp.concatenate([rows_ref[kv * C_KV_HEADS + g, pl.ds(l, n, stride=CMP_HALF), :]
                              for l in range(CMP_HALF)], axis=1) for g in range(C_KV_HEADS)], axis=0)
        for half, out in ((0, a_ref), (1, b_ref)):
            y = jnp.dot((x + pos_ref[kv, half]).astype(BF16), w1b_ref[kv, half], preferred_element_type=F32)
            for g in range(C_KV_HEADS):
                out[kv * C_KV_HEADS + g] = y[g * n:(g + 1) * n]


def _cmp_weights(cmp_pos, cmp_w1):
    n = cmp_pos.shape[0]
    return cmp_pos.reshape(n, 2, 2, 1, CMP_K), cmp_w1.reshape(n, 2, 2, CMP_K, CMP_HID)


def _cmp_partials_prompt(z, layer, cmp_pos, cmp_w1):
    b, t, _ = z.shape
    nch = t // CMP_STRIDE
    pos, w1 = _cmp_weights(cmp_pos, cmp_w1)
    kern = functools.partial(_cmp_partials_kernel, n_src=1, n_prefetch=0, interleaved=False)
    out = jax.ShapeDtypeStruct((b, 4, nch, CMP_HID), F32)
    ospec = pl.BlockSpec((None, 4, nch, CMP_HID), lambda bi: (bi, 0, 0, 0))
    return pl.pallas_call(
        kern,
        grid=(b,),
        in_specs=[pl.BlockSpec((None, t, 2 * C_KV), lambda bi: (bi, 0, ODD_KV0 // (2 * C_KV))),
                  pl.BlockSpec((None,) + pos.shape[1:], lambda bi: (layer, 0, 0, 0, 0)),
                  pl.BlockSpec((None,) + w1.shape[1:], lambda bi: (layer, 0, 0, 0, 0))],
        out_specs=[ospec, ospec],
        out_shape=[out, out],
        scratch_shapes=[pltpu.VMEM((2 * C_KV_HEADS, t, HEAD_DIM), F32), pltpu.VMEM(w1.shape[1:], BF16)],
        compiler_params=_cparams("parallel"),
        name="cmp_partials_prompt",
    )(z, pos, w1)


def _cmp_partials_paged(cache, layer, page_table, cmp_pos, cmp_w1):
    b, n_pages = page_table.shape
    page = cache.shape[2] // CACHE_ROW_KINDS
    pps = PAGES_PER_STEP
    assert n_pages % pps == 0 and page % CMP_STRIDE == 0
    nch_step = pps * page // CMP_STRIDE
    nch = n_pages * page // CMP_STRIDE
    pos, w1 = _cmp_weights(cmp_pos, cmp_w1)
    kern = functools.partial(_cmp_partials_kernel, n_src=pps, n_prefetch=1, interleaved=True)
    out = jax.ShapeDtypeStruct((b, 4, nch, CMP_HID), F32)
    ospec = pl.BlockSpec((None, 4, nch_step, CMP_HID), lambda bi, pg, pt: (bi, 0, pg, 0))

    def page_spec(j):
        return pl.BlockSpec((None, None, page * CACHE_ROW_KINDS, HEAD_DIM),
                            lambda bi, pg, pt: (layer, pt[bi, pg * pps + j], 0, 0))

    grid_spec = pltpu.PrefetchScalarGridSpec(
        num_scalar_prefetch=1,
        grid=(b, n_pages // pps),
        in_specs=[page_spec(j) for j in range(pps)] + [
            pl.BlockSpec((None,) + pos.shape[1:], lambda bi, pg, pt: (layer, 0, 0, 0, 0)),
            pl.BlockSpec((None,) + w1.shape[1:], lambda bi, pg, pt: (layer, 0, 0, 0, 0))],
        out_specs=[ospec, ospec],
        scratch_shapes=[pltpu.VMEM((2 * C_KV_HEADS, pps * page, HEAD_DIM), F32), pltpu.VMEM(w1.shape[1:], BF16)])
    return pl.pallas_call(
        kern, grid_spec=grid_spec, out_shape=[out, out],
        compiler_params=_cparams("parallel", "arbitrary"),
        name="cmp_partials_paged",
    )(page_table, *([cache] * pps), pos, w1)


def _overlap_np(n_ch, n_sel, ns_pad):
    ci = np.arange(n_ch)[:, None]
    si = np.arange(ns_pad)[None, :]
    ov = (ci * CMP_STRIDE < (si + 1) * SEL_BLK) & (ci * CMP_STRIDE + CMP_BLK > si * SEL_BLK)
    ov &= (ci < n_ch - 1) & (si < n_sel)
    return ov.astype(np.float32)


def _cmp_select_kernel(q_ref, ak_ref, bk_ref, av_ref, bv_ref, b1_ref, w2_ref, ov_ref, oc_ref, sel_ref,
                       *, qpos0, n_sel):
    qi = pl.program_id(2)
    tq = q_ref.shape[0]
    nch = ak_ref.shape[0]
    ns_pad = ov_ref.shape[1]

    def finish(a_ref, b_ref, kv):
        hid = a_ref[...] + pltpu.roll(b_ref[...], nch - 1, 0) + b1_ref[kv:kv + 1, :]
        return jnp.dot(_silu(hid).astype(BF16), w2_ref[kv].astype(BF16),
                       preferred_element_type=F32).astype(BF16)

    k_cmp = finish(ak_ref, bk_ref, 0)
    v_cmp = finish(av_ref, bv_ref, 1)
    if tq % LANES == 0:
        cols = C_HPG * tq
        assert tq & (tq - 1) == 0
        qpos_t = qpos0 + qi * tq + (lax.broadcasted_iota(jnp.int32, (nch, cols), 1) & (tq - 1))
        end_t = lax.broadcasted_iota(jnp.int32, (nch, cols), 0) * CMP_STRIDE + (CMP_BLK - 1)
        ok = (end_t <= qpos_t) & (end_t < (nch - 1) * CMP_STRIDE + CMP_BLK - 1)
        q = jnp.concatenate([(q_ref[:, h * HEAD_DIM:(h + 1) * HEAD_DIM] * SCALE).astype(BF16)
                             for h in range(C_HPG)], axis=0)
        s = lax.dot_general(k_cmp, q, (((1,), (1,)), ((), ())), preferred_element_type=F32)
        s = jnp.where(ok, s, NEG)
        m = jnp.max(s, axis=0, keepdims=True)
        e = jnp.where(ok, jnp.exp(s - m), 0.0)
        den = jnp.sum(e, axis=0, keepdims=True)
        p = e / jnp.where(den > 0.0, den, 1.0)
        psum = p[:, 0:tq]
        for h in range(1, C_HPG):
            psum = psum + p[:, h * tq:(h + 1) * tq]
        v_t = v_cmp.astype(F32).T.astype(BF16)
        oc = jnp.dot(v_t, p.astype(BF16), preferred_element_type=F32)
        for h in range(C_HPG):
            oc_ref[:, h * HEAD_DIM:(h + 1) * HEAD_DIM] = oc[:, h * tq:(h + 1) * tq].T
        imp = jnp.dot(ov_ref[...].T, psum, preferred_element_type=F32, precision=lax.Precision.HIGHEST)
        blk = lax.broadcasted_iota(jnp.int32, (ns_pad, tq), 0)
        qblk = (qpos0 + qi * tq + lax.broadcasted_iota(jnp.int32, (ns_pad, tq), 1)) // SEL_BLK
        forced = (blk == 0) | (blk == qblk) | (blk == qblk - 1)
        allowed = (blk <= qblk) & (blk < n_sel)
        score = jnp.where(allowed, jnp.where(forced, -NEG, imp), -1.0)
        rank = jnp.zeros((ns_pad, tq), jnp.int32)
        for j in range(n_sel):
            rj = score[j:j + 1, :]
            before = (rj > score) | ((rj == score) & (j < blk))
            rank = rank + before.astype(jnp.int32)
        sel_ref[...] = (allowed & (rank < SEL_N)).astype(F32).T
        return
    qpos = qpos0 + qi * tq + lax.broadcasted_iota(jnp.int32, (tq, nch), 0)
    cmp_end = lax.broadcasted_iota(jnp.int32, (tq, nch), 1) * CMP_STRIDE + (CMP_BLK - 1)
    ok = (cmp_end <= qpos) & (cmp_end < (nch - 1) * CMP_STRIDE + CMP_BLK - 1)
    q = jnp.concatenate([(q_ref[:, h * HEAD_DIM:(h + 1) * HEAD_DIM] * SCALE).astype(BF16) for h in range(C_HPG)],
                        axis=0)
    s = lax.dot_general(q, k_cmp, (((1,), (1,)), ((), ())), preferred_element_type=F32)
    s = jnp.where(ok[None], s.reshape(C_HPG, tq, nch), NEG)
    m = jnp.max(s, axis=-1, keepdims=True)
    e = jnp.where(ok[None], jnp.exp(s - m), 0.0)
    den = jnp.sum(e, axis=-1, keepdims=True)
    p = e / jnp.where(den > 0.0, den, 1.0)
    psum = jnp.sum(p, axis=0)
    oc = jnp.dot(p.reshape(C_HPG * tq, nch).astype(BF16), v_cmp, preferred_element_type=F32)
    for h in range(C_HPG):
        oc_ref[:, h * HEAD_DIM:(h + 1) * HEAD_DIM] = oc[h * tq:(h + 1) * tq]
    imp = jnp.dot(psum, ov_ref[...], preferred_element_type=F32, precision=lax.Precision.HIGHEST)
    blk = lax.broadcasted_iota(jnp.int32, (tq, ns_pad), 1)
    qblk = (qpos0 + qi * tq + lax.broadcasted_iota(jnp.int32, (tq, ns_pad), 0)) // SEL_BLK
    forced = (blk == 0) | (blk == qblk) | (blk == qblk - 1)
    allowed = (blk <= qblk) & (blk < n_sel)
    score = jnp.where(allowed, jnp.where(forced, -NEG, imp), -1.0)
    rank = jnp.zeros((tq, ns_pad), jnp.int32)
    for j in range(n_sel):
        cj = score[:, j:j + 1]
        before = (cj > score) | ((cj == score) & (j < blk))
        rank = rank + before.astype(jnp.int32)
    sel_ref[...] = (allowed & (rank < SEL_N)).astype(F32)


def _cmp_select(z, a, bm, layer, cmp_b1, cmp_w2, *, tq, qpos0, n_sel):
    b, t, _ = z.shape
    nch = a.shape[2]
    ns_pad = -(-n_sel // LANES) * LANES
    ov = jnp.asarray(_overlap_np(nch, n_sel, ns_pad))
    kern = functools.partial(_cmp_select_kernel, qpos0=qpos0, n_sel=n_sel)
    part = lambda kv: pl.BlockSpec((None, None, nch, CMP_HID), lambda bi, g, qi: (bi, kv * C_KV_HEADS + g, 0, 0))
    return pl.pallas_call(
        kern,
        grid=(b, C_KV_HEADS, t // tq),
        in_specs=[pl.BlockSpec((None, tq, C_HPG * HEAD_DIM), lambda bi, g, qi: (bi, qi, g)),
                  part(0), part(0), part(1), part(1),
                  pl.BlockSpec((None, 2, CMP_HID), lambda bi, g, qi: (layer, 0, 0)),
                  pl.BlockSpec((None, 2, CMP_HID, HEAD_DIM), lambda bi, g, qi: (layer, 0, 0, 0)),
                  pl.BlockSpec((nch, ns_pad), lambda bi, g, qi: (0, 0))],
        out_specs=[pl.BlockSpec((None, tq, C_HPG * HEAD_DIM), lambda bi, g, qi: (bi, qi, g)),
                   pl.BlockSpec((None, None, tq, ns_pad), lambda bi, g, qi: (bi, g, qi, 0))],
        out_shape=[jax.ShapeDtypeStruct((b, t, C_WIDTH), F32),
                   jax.ShapeDtypeStruct((b, C_KV_HEADS, t, ns_pad), F32)],
        compiler_params=_cparams("parallel", "parallel", "arbitrary"),
        name="cmp_select",
    )(z, a, bm, a, bm, cmp_b1, cmp_w2, ov)


def _sel_prompt_kernel(q_ref, k_ref, v_ref, rev_ref, sel_ref, o_ref,
                       bias_ref, msk_ref, qb_ref, m_ref, l_ref, acc_ref):
    qi = pl.program_id(2)
    tq = q_ref.shape[0]
    nt = bias_ref.shape[0] - 1
    tk = msk_ref.shape[1]
    ratio = tk // tq
    ns_pad = sel_ref.shape[1]

    @pl.when((pl.program_id(1) == 0) & (qi == 0))
    def _():
        bias_ref[0] = jnp.full(bias_ref.shape[1:], NEG, F32)
        for h in range(C_HPG):
            for delta in range(nt):
                off = (nt - 1 - delta) * tq
                bias_ref[delta + 1, :, h * tq:(h + 1) * tq] = (
                    _toeplitz(rev_ref[h, :, off:off + 2 * tq], tq).T * LOG2E)

    sel_t = sel_ref[...].T.astype(BF16)
    erow = lax.broadcasted_iota(jnp.int32, (tk, ns_pad), 0) // SEL_BLK
    ecol = lax.broadcasted_iota(jnp.int32, (tk, ns_pad), 1)
    for t in range(msk_ref.shape[0]):
        expand = (ecol == erow + t * (tk // SEL_BLK)).astype(BF16)
        hit = jnp.dot(expand, sel_t, preferred_element_type=F32)
        msk_ref[t] = jnp.where(hit > 0.5, 0.0, NEG)
    for h in range(C_HPG):
        qb_ref[h * tq:(h + 1) * tq, :] = (q_ref[:, h * HEAD_DIM:(h + 1) * HEAD_DIM] * (SCALE * LOG2E)).astype(BF16)
    m_ref[...] = jnp.full(m_ref.shape, NEG, F32)
    l_ref[...] = jnp.zeros(l_ref.shape, F32)
    acc_ref[...] = jnp.zeros(acc_ref.shape, F32)

    def body(kj, carry):
        keys = pl.ds(pl.multiple_of(kj * tk, tk), tk)
        k = k_ref[keys, :].astype(BF16)
        v_t = v_ref[keys, :].T.astype(BF16)
        d0 = qi - ratio * kj
        parts = [bias_ref[jnp.maximum(d0 + 1 - i, 0)] for i in range(ratio)]
        bias = parts[0] if ratio == 1 else jnp.concatenate(parts, axis=0)
        s = lax.dot_general(k, qb_ref[...], (((1,), (1,)), ((), ())), preferred_element_type=F32)
        s = s + bias + jnp.concatenate([msk_ref[kj]] * C_HPG, axis=1)
        m_old = m_ref[...]
        m_new = jnp.maximum(m_old, jnp.max(s, axis=0, keepdims=True))
        alpha = jnp.exp2(m_old - m_new)
        e = jnp.exp2(s - m_new)
        m_ref[...] = m_new
        l_ref[...] = alpha * l_ref[...] + jnp.sum(e, axis=0, keepdims=True)
        acc_ref[...] = alpha * acc_ref[...] + jnp.dot(v_t, e.astype(BF16), preferred_element_type=F32)
        return carry

    lax.fori_loop(0, (qi + ratio) // ratio, body, 0)
    o = acc_ref[...] / l_ref[...]
    for h in range(C_HPG):
        o_ref[:, h * HEAD_DIM:(h + 1) * HEAD_DIM] = o[:, h * tq:(h + 1) * tq].T


SEL_TQ = 128
SEL_KEY_RATIO = 4


def _sel_attn_prompt(z, sel, rel_bias):
    b, t, _ = z.shape
    tq = SEL_TQ
    nt = t // tq
    ratio = SEL_KEY_RATIO
    assert nt % ratio == 0
    ns_pad = sel.shape[3]
    rev = _c_rev_table(rel_bias, t, tq, t)
    kcol = (ODD_KV0 + 2 * C_KV) // HEAD_DIM
    vcol = (ODD_KV0 + 3 * C_KV) // HEAD_DIM
    return pl.pallas_call(
        _sel_prompt_kernel,
        grid=(C_KV_HEADS, b, t // tq),
        in_specs=[pl.BlockSpec((None, tq, C_HPG * HEAD_DIM), lambda g, bi, qi: (bi, qi, g)),
                  pl.BlockSpec((None, t, HEAD_DIM), lambda g, bi, qi: (bi, 0, kcol + g)),
                  pl.BlockSpec((None, t, HEAD_DIM), lambda g, bi, qi: (bi, 0, vcol + g)),
                  pl.BlockSpec((C_HPG, 1, t + tq), lambda g, bi, qi: (g, 0, 0)),
                  pl.BlockSpec((None, None, tq, ns_pad), lambda g, bi, qi: (bi, g, qi, 0))],
        out_specs=pl.BlockSpec((None, tq, C_HPG * HEAD_DIM), lambda g, bi, qi: (bi, qi, g)),
        out_shape=jax.ShapeDtypeStruct((b, t, C_WIDTH), F32),
        scratch_shapes=[pltpu.VMEM((nt + 1, tq, C_HPG * tq), F32),
                        pltpu.VMEM((nt // ratio, ratio * tq, tq), F32),
                        pltpu.VMEM((C_HPG * tq, HEAD_DIM), BF16),
                        pltpu.VMEM((1, C_HPG * tq), F32),
                        pltpu.VMEM((1, C_HPG * tq), F32),
                        pltpu.VMEM((HEAD_DIM, C_HPG * tq), F32)],
        compiler_params=_cparams("parallel", "arbitrary", "arbitrary"),
        name="sel_attn_prompt",
    )(z, z, z, rev, sel)


def _sel_paged_kernel(pt_ref, *refs, n_src, past):
    del pt_ref
    pages = refs[:n_src]
    q_ref, new_ref, sel_ref, bias_ref, bnew_ref, o_ref, m_ref, l_ref, acc_ref = refs[n_src:]
    pg = pl.program_id(1)
    tq = q_ref.shape[0]
    rows = C_HPG * tq
    page = pages[0].shape[0] // CACHE_ROW_KINDS
    keys = n_src * page
    ns_pad = sel_ref.shape[2]

    def page_rows(r, kind):
        return r[pl.ds(CACHE_KINDS + kind, page, stride=CACHE_ROW_KINDS), :]

    def q_rows(g):
        return jnp.concatenate(
            [(q_ref[:, (g * C_HPG + h) * HEAD_DIM:(g * C_HPG + h + 1) * HEAD_DIM] * SCALE).astype(BF16)
             for h in range(C_HPG)], axis=0)

    @pl.when(pg == 0)
    def _():
        for g in range(C_KV_HEADS):
            k = new_ref[:, g * HEAD_DIM:(g + 1) * HEAD_DIM].astype(BF16)
            v = new_ref[:, C_KV + g * HEAD_DIM:C_KV + (g + 1) * HEAD_DIM].astype(BF16)
            s = lax.dot_general(q_rows(g), k, (((1,), (1,)), ((), ())), preferred_element_type=F32)
            s = s + bnew_ref[g * C_HPG:(g + 1) * C_HPG].reshape(rows, tq)
            m = jnp.max(s, axis=-1, keepdims=True)
            e = jnp.exp(s - m)
            m_ref[g] = m
            l_ref[g] = jnp.sum(e, axis=-1, keepdims=True)
            acc_ref[g] = jnp.dot(e.astype(BF16), v, preferred_element_type=F32)

    srow = lax.broadcasted_iota(jnp.int32, (ns_pad, keys), 0)
    scol = lax.broadcasted_iota(jnp.int32, (ns_pad, keys), 1) // SEL_BLK + pg * (keys // SEL_BLK)
    expand = (srow == scol).astype(BF16)
    for g in range(C_KV_HEADS):
        k = jnp.concatenate([page_rows(r, g) for r in pages], axis=0).astype(BF16)
        v = jnp.concatenate([page_rows(r, C_KV_HEADS + g) for r in pages], axis=0).astype(BF16)
        hit = jnp.dot(sel_ref[g].astype(BF16), expand, preferred_element_type=F32)
        mask = jnp.where(hit > 0.5, 0.0, NEG)
        s = lax.dot_general(q_rows(g), k, (((1,), (1,)), ((), ())), preferred_element_type=F32)
        s = (s.reshape(C_HPG, tq, keys) + bias_ref[g * C_HPG:(g + 1) * C_HPG] + mask[None]).reshape(rows, keys)
        m_old = m_ref[g]
        m_new = jnp.maximum(m_old, jnp.max(s, axis=-1, keepdims=True))
        alpha = jnp.exp(m_old - m_new)
        e = jnp.exp(s - m_new)
        m_ref[g] = m_new
        l_ref[g] = alpha * l_ref[g] + jnp.sum(e, axis=-1, keepdims=True)
        acc_ref[g] = alpha * acc_ref[g] + jnp.dot(e.astype(BF16), v, preferred_element_type=F32)

    @pl.when(pg == pl.num_programs(1) - 1)
    def _():
        for g in range(C_KV_HEADS):
            o = acc_ref[g] / l_ref[g]
            for h in range(C_HPG):
                o_ref[:, (g * C_HPG + h) * HEAD_DIM:(g * C_HPG + h + 1) * HEAD_DIM] = o[h * tq:(h + 1) * tq]


def _sel_attn_paged(z, sel, cache, layer, page_table, rel_bias):
    b, tq, _ = z.shape
    n_pages = page_table.shape[1]
    page = cache.shape[2] // CACHE_ROW_KINDS
    past = n_pages * page
    pps = PAGES_PER_STEP
    keys = pps * page
    ns_pad = sel.shape[3]
    tab = _c_tab(rel_bias)
    rev = tab[_t5_bucket_np(past + tq - 1 - np.arange(past + tq - 1))].T.astype(F32)
    bias = jnp.stack([rev[:, tq - 1 - t:tq - 1 - t + past] for t in range(tq)], axis=1)
    dnew = np.arange(tq)[:, None] - np.arange(tq)[None, :]
    bnew = jnp.where(jnp.asarray(dnew >= 0)[None],
                     jnp.transpose(tab[_t5_bucket_np(np.clip(dnew, 0, None))], (2, 0, 1)), NEG).astype(F32)
    kern = functools.partial(_sel_paged_kernel, n_src=pps, past=past)

    def page_spec(j):
        return pl.BlockSpec((None, None, page * CACHE_ROW_KINDS, HEAD_DIM),
                            lambda bi, pg, pt: (layer, pt[bi, pg * pps + j], 0, 0))

    rows = C_HPG * tq
    grid_spec = pltpu.PrefetchScalarGridSpec(
        num_scalar_prefetch=1,
        grid=(b, n_pages // pps),
        in_specs=[page_spec(j) for j in range(pps)] + [
            pl.BlockSpec((None, tq, C_WIDTH), lambda bi, pg, pt: (bi, 0, 0)),
            pl.BlockSpec((None, tq, 2 * C_KV), lambda bi, pg, pt: (bi, 0, (ODD_KV0 + 2 * C_KV) // (2 * C_KV))),
            pl.BlockSpec((None, C_KV_HEADS, tq, ns_pad), lambda bi, pg, pt: (bi, 0, 0, 0)),
            pl.BlockSpec((C_HEADS, tq, keys), lambda bi, pg, pt: (0, 0, pg)),
            pl.BlockSpec((C_HEADS, tq, tq), lambda bi, pg, pt: (0, 0, 0))],
        out_specs=pl.BlockSpec((None, tq, C_WIDTH), lambda bi, pg, pt: (bi, 0, 0)),
        scratch_shapes=[pltpu.VMEM((C_KV_HEADS, rows, 1), F32), pltpu.VMEM((C_KV_HEADS, rows, 1), F32),
                        pltpu.VMEM((C_KV_HEADS, rows, HEAD_DIM), F32)])
    return pl.pallas_call(
        kern, grid_spec=grid_spec, out_shape=jax.ShapeDtypeStruct((b, tq, C_WIDTH), F32),
        compiler_params=_cparams("parallel", "arbitrary"),
        name="sel_attn_paged",
    )(page_table, *([cache] * pps), z, z, sel, bias, bnew)


def _win_prompt_kernel(*refs, n_tiles):
    q_ref = refs[0]
    kv_refs = refs[1:1 + n_tiles]
    rev_ref, o_ref, bias_scr = refs[1 + n_tiles:]
    qi = pl.program_id(1)
    tq = q_ref.shape[0]
    lk = n_tiles * tq
    cols = C_HPG * tq

    @pl.when(qi == 0)
    def _():
        for h in range(C_HEADS):
            g, hh = divmod(h, C_HPG)
            bias_scr[g, :, hh * tq:(hh + 1) * tq] = _toeplitz(rev_ref[h], tq).T * LOG2E

    clamped = lax.broadcasted_iota(jnp.int32, (lk, cols), 0) < (n_tiles - 1 - qi) * tq
    for g in range(C_KV_HEADS):
        q = jnp.concatenate(
            [(q_ref[:, (g * C_HPG + h) * HEAD_DIM:(g * C_HPG + h + 1) * HEAD_DIM] * (SCALE * LOG2E)).astype(BF16)
             for h in range(C_HPG)], axis=0)
        k = jnp.concatenate([r[:, g * HEAD_DIM:(g + 1) * HEAD_DIM] for r in kv_refs], axis=0).astype(BF16)
        v_t = jnp.concatenate([r[:, C_KV + g * HEAD_DIM:C_KV + (g + 1) * HEAD_DIM] for r in kv_refs],
                              axis=0).T.astype(BF16)
        s = lax.dot_general(k, q, (((1,), (1,)), ((), ())), preferred_element_type=F32) + bias_scr[g]
        s = jnp.where(clamped, NEG, s)
        m = jnp.max(s, axis=0, keepdims=True)
        e = jnp.exp2(s - m)
        den = jnp.sum(e, axis=0, keepdims=True)
        o = jnp.dot(v_t, e.astype(BF16), preferred_element_type=F32) / den
        for h in range(C_HPG):
            o_ref[:, (g * C_HPG + h) * HEAD_DIM:(g * C_HPG + h + 1) * HEAD_DIM] = o[:, h * tq:(h + 1) * tq].T


def _win_attn_prompt(z, rel_bias, *, tq):
    b, t, _ = z.shape
    n_tiles = WIN // tq + 1
    rev = _c_rev_table(rel_bias, n_tiles * tq, tq, WIN)
    kv_blk = (ODD_KV0 + 4 * C_KV) // (2 * C_KV)
    kern = functools.partial(_win_prompt_kernel, n_tiles=n_tiles)

    def kv_spec(p):
        return pl.BlockSpec((None, tq, 2 * C_KV),
                            lambda bi, qi: (bi, jnp.maximum(qi - (n_tiles - 1 - p), 0), kv_blk))

    return pl.pallas_call(
        kern,
        grid=(b, t // tq),
        in_specs=[pl.BlockSpec((None, tq, C_WIDTH), lambda bi, qi: (bi, qi, 0))]
                 + [kv_spec(p) for p in range(n_tiles)]
                 + [pl.BlockSpec((C_HEADS, 1, (n_tiles + 1) * tq), lambda bi, qi: (0, 0, 0))],
        out_specs=pl.BlockSpec((None, tq, C_WIDTH), lambda bi, qi: (bi, qi, 0)),
        out_shape=jax.ShapeDtypeStruct((b, t, C_WIDTH), F32),
        scratch_shapes=[pltpu.VMEM((C_KV_HEADS, n_tiles * tq, C_HPG * tq), F32)],
        compiler_params=_cparams("parallel", "arbitrary"),
        name="win_attn_prompt",
    )(z, *([z] * n_tiles), rev)


def _win_attn_sample(z, win_full, rel_bias):
    b, tq, _ = z.shape
    lk = win_full.shape[1]
    dist = (lk - tq) + np.arange(tq)[:, None] - np.arange(lk)[None, :]
    valid = (dist >= 0) & (dist <= WIN)
    bias = jnp.transpose(_c_tab(rel_bias)[_t5_bucket_np(np.clip(dist, 0, None))], (2, 0, 1))
    bias = jnp.where(jnp.asarray(valid)[None], bias, NEG).astype(F32)
    kern = functools.partial(_tile_attn_kernel, nh=C_HEADS, rep=C_HPG, n_tiles=1, k_col=0, v_col=C_KV,
                             shared_kv=True, with_lse=False, lead_axis=0)
    return pl.pallas_call(
        kern,
        grid=(b,),
        in_specs=[pl.BlockSpec((None, tq, C_WIDTH), lambda bi: (bi, 0, 0)),
                  pl.BlockSpec((None, lk, 2 * C_KV), lambda bi: (bi, 0, 0)),
                  pl.BlockSpec((C_HEADS, tq, lk), lambda bi: (0, 0, 0))],
        out_specs=pl.BlockSpec((None, tq, C_WIDTH), lambda bi: (bi, 0, 0)),
        out_shape=jax.ShapeDtypeStruct((b, tq, C_WIDTH), F32),
        compiler_params=_cparams("parallel"),
        name="win_attn_sample",
    )(z, win_full, bias)


def _odd_mix_kernel(oc_ref, os_ref, ow_ref, gl_ref, gp_ref, mix_ref):
    gates = jax.nn.sigmoid(gl_ref[...])
    branches = (oc_ref, os_ref, ow_ref)
    for h in range(C_HEADS):
        cs = slice(h * HEAD_DIM, (h + 1) * HEAD_DIM)
        acc = gates[:, h:h + 1] * branches[0][:, cs]
        for br in range(1, 3):
            lane = br * C_HEADS + h
            acc = acc + gates[:, lane:lane + 1] * branches[br][:, cs]
        mix_ref[:, cs] = (acc * _silu(gp_ref[:, cs])).astype(BF16)


def _odd_mix(oc, osel, ow, z2, zgp, *, tm):
    m = z2.shape[0]
    wide = pl.BlockSpec((tm, C_WIDTH), lambda i: (i, 0))
    return pl.pallas_call(
        _odd_mix_kernel,
        grid=(m // tm,),
        in_specs=[wide, wide, wide, pl.BlockSpec((tm, LANES), lambda i: (i, ODD_GL0 // LANES)), wide],
        out_specs=wide,
        out_shape=jax.ShapeDtypeStruct((m, C_WIDTH), BF16),
        compiler_params=_cparams("parallel"),
        name="odd_mix",
    )(oc, osel, ow, z2, zgp)


def _odd_layer(x, past, win_hist, rel_bias, layer, norm_g, w_in, w_in_gp, cmp_pos, cmp_w1, cmp_b1, cmp_w2, w_out,
               *, tm, tn_in, tn_out, tq):
    b, t, d = x.shape
    m = b * t
    x2 = x.reshape(m, d)
    z2 = _norm_matmul(x2, norm_g[layer], w_in, layer, tm=tm, tn=tn_in, n_out=ODD_Z)
    zgp = _norm_matmul(x2, norm_g[layer], w_in_gp, layer, tm=tm, tn=tn_in)
    z = z2.reshape(b, t, ODD_Z)
    rows = z[:, :, ODD_KV0:ODD_KV0 + 4 * C_KV].reshape(b, t, 4, C_KV_HEADS, HEAD_DIM)
    win_new = z[:, :, ODD_KV0 + 4 * C_KV:ODD_KV0 + 6 * C_KV]
    if past is None:
        length = t
        a, bm = _cmp_partials_prompt(z, layer, cmp_pos, cmp_w1)
        qpos0 = 0
    else:
        cache, page_table = past
        p_len = page_table.shape[1] * (cache.shape[2] // CACHE_ROW_KINDS)
        length = p_len + t
        assert length // CMP_STRIDE == p_len // CMP_STRIDE and p_len % SEL_BLK == 0
        a, bm = _cmp_partials_paged(cache, layer, page_table, cmp_pos, cmp_w1)
        qpos0 = p_len
    n_sel = -(-length // SEL_BLK)
    oc, sel = _cmp_select(z, a, bm, layer, cmp_b1, cmp_w2, tq=tq, qpos0=qpos0, n_sel=n_sel)
    if past is None:
        osel = _sel_attn_prompt(z, sel, rel_bias)
        ow = _win_attn_prompt(z, rel_bias, tq=tq)
        new_win = win_new[:, t - min(WIN, t):].reshape(b, min(WIN, t), 2, C_KV_HEADS, HEAD_DIM)
    else:
        osel = _sel_attn_paged(z, sel, cache, layer, page_table, rel_bias)
        win_full = jnp.concatenate([win_hist.reshape(b, -1, 2 * C_KV), win_new], axis=1)
        ow = _win_attn_sample(z, win_full, rel_bias)
        lk = win_full.shape[1]
        keep = min(WIN, length)
        new_win = win_full[:, lk - keep:].reshape(b, keep, 2, C_KV_HEADS, HEAD_DIM)
    mix = _odd_mix(oc.reshape(m, C_WIDTH), osel.reshape(m, C_WIDTH), ow.reshape(m, C_WIDTH), z2, zgp,
                   tm=min(tm, 256))
    y = _matmul_res([mix], w_out, layer, x2, tm=tm, tn=tn_out)
    return y.reshape(b, t, d), rows, new_win


def _run_trunk(x, a_caches, conv_state, c_cache, c_win, page_table, rel_bias, norm_even, w_in_even, conv_w,
               conv_b, conv_ln_g, conv_ln_b, conv_pw_w, conv_pw_b, w_out_even, norm_odd, w_in_odd, w_in_odd_gp,
               cmp_pos, cmp_w1, cmp_b1, cmp_w2, w_out_odd, final_norm, *, tm, tt, tq):
    b, t, d = x.shape
    prompt = a_caches is None
    depth = norm_even.shape[0] + norm_odd.shape[0]
    new_a = [[] for _ in A_GROUPS]
    a_new = None
    new_conv, new_rows, new_win = [], [], []
    for depth_i in range(depth):
        i = depth_i // 2
        if depth_i % 2 == 0:
            conv_hist = jnp.zeros((b, CONV_W - 1, B_WIDTH), F32) if prompt else conv_state[i]
            x, hists, conv = _even_layer(x, a_caches, a_new, conv_hist, rel_bias, i, norm_even, w_in_even, conv_w,
                                         conv_b, conv_ln_g, conv_ln_b, conv_pw_w, conv_pw_b,
                                         w_out_even, tm=tm, tn_in=512, tn_out=512, tt=tt)
            if prompt:
                for g in range(A_NG):
                    new_a[g].append(hists[g])
            else:
                a_new = hists
            new_conv.append(conv)
        else:
            past = None if prompt else (c_cache, page_table)
            x, rows, win = _odd_layer(x, past, None if prompt else c_win[i], rel_bias, i, norm_odd,
                                      w_in_odd, w_in_odd_gp, cmp_pos, cmp_w1, cmp_b1, cmp_w2,
                                      w_out_odd, tm=tm, tn_in=512, tn_out=512, tq=tq)
            new_rows.append(rows)
            new_win.append(win)
    y = _rms_norm(x.reshape(b * t, d), final_norm, tm=min(tm, 256)).reshape(b, t, d)
    if prompt:
        a_out = [jnp.stack(a) for a in new_a]
    else:
        a_out = [n.reshape(c.shape) for n, c in zip(a_new, a_caches)]
    return (y, a_out, jnp.stack(new_conv), jnp.stack(new_rows), jnp.stack(new_win))


def kernel(x_prompt, x_sample, cache_a_kv0, cache_a_kv1, cache_a_kv2, state_b_conv, cache_c_kv, cache_c_win, page_table, rel_bias, norm_even, w_in_even, conv_w, conv_b, conv_ln_g, conv_ln_b, conv_pw_w, conv_pw_b, w_out_even, norm_odd, w_in_odd, cmp_pos, cmp_w1, cmp_b1, cmp_w2, w_out_odd, final_norm):
    w_in_odd_gp = w_in_odd[:, :, ODD_GP0:]
    weights = (rel_bias, norm_even, w_in_even, conv_w, conv_b, conv_ln_g, conv_ln_b, conv_pw_w, conv_pw_b,
               w_out_even, norm_odd, w_in_odd, w_in_odd_gp, cmp_pos, cmp_w1, cmp_b1, cmp_w2, w_out_odd, final_norm)
    y_p, a_p, conv_p, rows_p, win_p = _run_trunk(x_prompt, None, None, None, None, None, *weights,
                                                 tm=2048, tt=256, tq=128)
    n_odd, n_pool, page = cache_c_kv.shape[:3]
    c_cache = cache_c_kv.reshape(n_odd, n_pool, page * CACHE_ROW_KINDS, HEAD_DIM)
    db, dt = x_sample.shape[:2]
    y_s, a_s, conv_s, rows_s, win_s = _run_trunk(x_sample, (cache_a_kv0, cache_a_kv1, cache_a_kv2), state_b_conv,
                                                 c_cache, cache_c_win, page_table, *weights,
                                                 tm=db * dt, tt=dt, tq=dt)
    return (y_p, y_s, a_p[0], a_p[1], a_p[2], conv_p, rows_p, win_p, a_s[0], a_s[1], a_s[2], conv_s, rows_s, win_s)
```
